```python
import math
import jax, jax.numpy as jnp
from jax import lax
import numpy as np

D_MODEL = 1024
BATCH = 16
SEQ = 256
DEPTH = 4
DEC_BATCH = 4
DEC_SEQ = 2048
PAST_LEN = 512

GRID_W = 64
HEAD_DIM = 64
GROUP_W = D_MODEL // 4
N_HEADS_GRP = GROUP_W // HEAD_DIM
DA_SUB = HEAD_DIM // 2
CONV_K = 3
MLA_Q_RANK = 256
MLA_KV_RANK = 128
MLA_NOPE = 64
MLA_ROPE = 32
MLA_V = 64
MLA_SCALE = (MLA_NOPE + MLA_ROPE) ** -0.5
NA_WIN_H = 8
NA_WIN_W = 16
NA_COL_BLOCK = 16
NA_BAND_W = 2 * NA_WIN_W
ROPE_BASE = 10000.0
Q_BLOCK = 128
EPS = 1e-6
NEG_INF = -1e30
IN_SIZES = (GROUP_W,) * 8 + (MLA_Q_RANK, MLA_KV_RANK, MLA_ROPE, GROUP_W) + (GROUP_W,) * 4
D_IN = sum(IN_SIZES)

kernel_name = 'hybrid_parallel_groups_flow_step'


def rms_norm(x, g):
    xf = x.astype(jnp.float32)
    y = xf * lax.rsqrt(jnp.mean(xf * xf, axis=-1, keepdims=True) + EPS)
    return (y * g.astype(jnp.float32)).astype(x.dtype)


def adaln(cvec, w, b):
    m = jax.nn.silu(cvec) @ w + b
    return jnp.split(m, 3, axis=-1)


def in_projection(x, shift, scale, norm_g, w_in):
    h = rms_norm(x, norm_g) * (1.0 + scale) + shift
    u = h @ w_in
    parts, off = [], 0
    for n in IN_SIZES:
        parts.append(u[..., off:off + n])
        off += n
    return parts


def out_projection(ys, zs, w_out):
    y = jnp.concatenate([yb * jax.nn.silu(z) for yb, z in zip(ys, zs)], axis=-1)
    return y @ w_out


def axial_rope_tables(T, rot_dim):
    t = jnp.arange(T)
    rows = (t // GRID_W).astype(jnp.float32)
    cols = (t % GRID_W).astype(jnp.float32)
    half = rot_dim // 2
    inv = 1.0 / (ROPE_BASE ** (jnp.arange(0, half, 2, dtype=jnp.float32) / half))
    ar = rows[:, None] * inv
    ac = cols[:, None] * inv
    ang = jnp.concatenate([ar, ar, ac, ac], axis=-1)
    return jnp.cos(ang), jnp.sin(ang)


def apply_axial_rope(x, cos, sin):
    T, R = cos.shape
    q4 = R // 4
    xs = x.reshape(*x.shape[:-1], 2, 2, q4)
    rot = jnp.stack([-xs[..., 1, :], xs[..., 0, :]], axis=-2).reshape(x.shape)
    bshape = (1, T) + (1,) * (x.ndim - 3) + (R,)
    out = x.astype(jnp.float32) * cos.reshape(bshape) + rot.astype(jnp.float32) * sin.reshape(bshape)
    return out.astype(x.dtype)


def map_query_blocks(fn, *qs):
    bsz, S = qs[0].shape[:2]
    nb = S // Q_BLOCK
    blocks = tuple(jnp.swapaxes(q.reshape(bsz, nb, Q_BLOCK, *q.shape[2:]), 0, 1) for q in qs)
    out = lax.map(lambda blk: fn(*blk), blocks)
    out = jnp.swapaxes(out, 0, 1)
    return out.reshape(bsz, S, *out.shape[3:])


def dense_attend(q, k, v, scale):
    s = jnp.einsum('bqhd,bkhd->bhqk', q, k).astype(jnp.float32) * scale
    p = jax.nn.softmax(s, axis=-1)
    return jnp.einsum('bhqk,bkhe->bqhe', p.astype(v.dtype), v)


def diff_lambda(lam_vecs, lam_init):
    lv = lam_vecs.astype(jnp.float32)
    return jnp.exp(jnp.sum(lv[0] * lv[1])) - jnp.exp(jnp.sum(lv[2] * lv[3])) + lam_init


def diff_attend(q, k, v, lam, lam_init, subln_g):
    s = jnp.einsum('bqhcd,bkhcd->bchqk', q, k).astype(jnp.float32) * (DA_SUB ** -0.5)
    p = jax.nn.softmax(s, axis=-1)
    attn = p[:, 0] - lam * p[:, 1]
    o = jnp.einsum('bhqk,bkhe->bqhe', attn.astype(v.dtype), v)
    return rms_norm(o, subln_g) * (1.0 - lam_init)


def short_conv(u, w):
    return lax.conv_general_dilated(
        u, w[:, None, :], window_strides=(1,), padding=[(CONV_K // 2, CONV_K // 2)],
        dimension_numbers=('NWC', 'WIO', 'NWC'), feature_group_count=u.shape[-1])


def mla_compress(cq, ckv, p):
    bsz, T = cq.shape[:2]
    q = (rms_norm(cq, p['mla_q_norm_g']) @ p['mla_w_uq']).reshape(bsz, T, N_HEADS_GRP, MLA_NOPE + MLA_ROPE)
    return q[..., :MLA_NOPE], q[..., MLA_NOPE:], rms_norm(ckv, p['mla_kv_norm_g'])


def mla_expand(ckv_n, w_ukv):
    bsz, K = ckv_n.shape[:2]
    kv = (ckv_n @ w_ukv).reshape(bsz, K, N_HEADS_GRP, MLA_NOPE + MLA_V)
    return kv[..., :MLA_NOPE], kv[..., MLA_NOPE:]


def mla_attend(qn, qp, kn, kp, v):
    s = (jnp.einsum('bqhd,bkhd->bhqk', qn, kn)
         + jnp.einsum('bqhr,bkr->bhqk', qp, kp)).astype(jnp.float32) * MLA_SCALE
    p = jax.nn.softmax(s, axis=-1)
    return jnp.einsum('bhqk,bkhe->bqhe', p.astype(v.dtype), v)


def neighbourhood_attend(q, k, v, ctx_k, ctx_v, rpb):
    bsz, T, H, d = q.shape
    rows = T // GRID_W
    wr = min(NA_WIN_H, rows)
    ncb = GRID_W // NA_COL_BLOCK
    r = np.arange(rows)
    row_idx = np.clip(r - wr // 2, 0, rows - wr)[:, None] + np.arange(wr)
    j = np.arange(ncb)
    band_start = np.clip(j * NA_COL_BLOCK - NA_WIN_W // 2, 0, GRID_W - NA_BAND_W)
    col_idx = band_start[:, None] + np.arange(NA_BAND_W)
    qcol = j[:, None] * NA_COL_BLOCK + np.arange(NA_COL_BLOCK)
    win_start = np.clip(qcol - NA_WIN_W // 2, 0, GRID_W - NA_WIN_W)
    col_mask = ((col_idx[:, None, :] >= win_start[:, :, None])
                & (col_idx[:, None, :] < win_start[:, :, None] + NA_WIN_W))
    dy = row_idx - r[:, None] + NA_WIN_H - 1
    dx = np.clip(col_idx[:, None, :] - qcol[:, :, None] + NA_WIN_W - 1, 0, 2 * NA_WIN_W - 2)
    bias = rpb[:, dy[:, None, None, :, None], dx[None, :, :, None, :]].astype(jnp.float32)
    bias = jnp.where(col_mask[None, None, :, :, None, :], bias, NEG_INF)
    bias = bias.transpose(1, 2, 0, 3, 4, 5).reshape(rows, ncb, H, NA_COL_BLOCK, wr * NA_BAND_W)

    qg = q.reshape(bsz, rows, ncb, NA_COL_BLOCK, H, d)
    ri = row_idx[:, None, :, None]
    ci = col_idx[None, :, None, :]

    def gather_band(a):
        g = a.reshape(bsz, rows, GRID_W, H, d)[:, ri, ci]
        return g.transpose(0, 1, 2, 5, 3, 4, 6).reshape(bsz, rows, ncb, H, wr * NA_BAND_W, d)

    kb = gather_band(k)
    vb = gather_band(v)
    scale = d ** -0.5
    s_loc = jnp.einsum('brjqhd,brjhkd->brjhqk', qg, kb).astype(jnp.float32) * scale + bias[None]
    s_ctx = jnp.einsum('brjqhd,blhd->brjhql', qg, ctx_k).astype(jnp.float32) * scale
    p = jax.nn.softmax(jnp.concatenate([s_loc, s_ctx], axis=-1), axis=-1)
    k_loc = wr * NA_BAND_W
    o = (jnp.einsum('brjhqk,brjhkd->brjqhd', p[..., :k_loc].astype(v.dtype), vb)
         + jnp.einsum('brjhql,blhd->brjqhd', p[..., k_loc:].astype(v.dtype), ctx_v))
    return o.reshape(bsz, T, H, d)


def context_layer(x, mod, p, lam_init):
    bsz, L, _ = x.shape
    H = N_HEADS_GRP
    shift, scale, gate = mod
    (aq, ak, av, az, bb, bc, bh, bz, cq, ckv, ckpe, cz, dq, dk, dv, dz) = in_projection(
        x, shift, scale, p['norm_g'], p['w_in'])
    aq = aq.reshape(bsz, L, H, 2, DA_SUB)
    ak = ak.reshape(bsz, L, H, 2, DA_SUB)
    av = av.reshape(bsz, L, H, HEAD_DIM)
    lam = diff_lambda(p['da_lambda'], lam_init)
    ya = map_query_blocks(lambda qb: diff_attend(qb, ak, av, lam, lam_init, p['da_subln_g']), aq)
    yb = bb * short_conv(bc * bh, p['conv_w'])
    qn, qp, ckv_n = mla_compress(cq, ckv, p)
    kn, vc = mla_expand(ckv_n, p['mla_w_ukv'])
    yc = map_query_blocks(lambda a, b: mla_attend(a, b, kn, ckpe, vc), qn, qp)
    dq = dq.reshape(bsz, L, H, HEAD_DIM)
    dk = dk.reshape(bsz, L, H, HEAD_DIM)
    dv = dv.reshape(bsz, L, H, HEAD_DIM)
    yd = map_query_blocks(lambda qb: dense_attend(qb, dk, dv, HEAD_DIM ** -0.5), dq)
    y = out_projection([ya.reshape(bsz, L, GROUP_W), yb, yc.reshape(bsz, L, GROUP_W), yd.reshape(bsz, L, GROUP_W)],
                       [az, bz, cz, dz], p['w_out'])
    x = x + gate * y
    return x, (ak.reshape(bsz, L, H, HEAD_DIM), av, ckv_n, ckpe, dk, dv)


def latent_layer(x, mod, p, lam_init, ctx):
    bsz, T, _ = x.shape
    H = N_HEADS_GRP
    ctx_ak, ctx_av, ctx_ckv, ctx_kpe, ctx_dk, ctx_dv = ctx
    L = ctx_ak.shape[1]
    shift, scale, gate = mod
    (aq, ak, av, az, bb, bc, bh, bz, cq, ckv, ckpe, cz, dq, dk, dv, dz) = in_projection(
        x, shift, scale, p['norm_g'], p['w_in'])
    cos_a, sin_a = axial_rope_tables(T, DA_SUB)
    aq = apply_axial_rope(aq.reshape(bsz, T, H, 2, DA_SUB), cos_a, sin_a)
    ak = apply_axial_rope(ak.reshape(bsz, T, H, 2, DA_SUB), cos_a, sin_a)
    k_all = jnp.concatenate([ctx_ak.reshape(bsz, L, H, 2, DA_SUB), ak], axis=1)
    v_all = jnp.concatenate([ctx_av, av.reshape(bsz, T, H, HEAD_DIM)], axis=1)
    lam = diff_lambda(p['da_lambda'], lam_init)
    ya = map_query_blocks(lambda qb: diff_attend(qb, k_all, v_all, lam, lam_init, p['da_subln_g']), aq)
    yb = bb * short_conv(bc * bh, p['conv_w'])
    cos_c, sin_c = axial_rope_tables(T, MLA_ROPE)
    qn, qp, ckv_n = mla_compress(cq, ckv, p)
    qp = apply_axial_rope(qp, cos_c, sin_c)
    kp_all = jnp.concatenate([ctx_kpe, apply_axial_rope(ckpe, cos_c, sin_c)], axis=1)
    kn, vc = mla_expand(jnp.concatenate([ctx_ckv, ckv_n], axis=1), p['mla_w_ukv'])
    yc = map_query_blocks(lambda a, b: mla_attend(a, b, kn, kp_all, vc), qn, qp)
    yd = neighbourhood_attend(dq.reshape(bsz, T, H, HEAD_DIM), dk.reshape(bsz, T, H, HEAD_DIM),
                              dv.reshape(bsz, T, H, HEAD_DIM), ctx_dk, ctx_dv, p['na_rpb'])
    y = out_projection([ya.reshape(bsz, T, GROUP_W), yb, yc.reshape(bsz, T, GROUP_W), yd.reshape(bsz, T, GROUP_W)],
                       [az, bz, cz, dz], p['w_out'])
    return x + gate * y


def setup_inputs(seed: int = 0) -> dict:
    key = jax.random.key(seed)
    ks = jax.random.split(key, 24)
    H = N_HEADS_GRP

    def nrm(k, shape, s):
        return jax.random.normal(k, shape, jnp.float32) * s

    return {
        'x_prompt': nrm(ks[0], (BATCH, SEQ, D_MODEL), 1.0),
        'x_sample': nrm(ks[1], (DEC_BATCH, DEC_SEQ, D_MODEL), 1.0),
        'cache_a_k': nrm(ks[2], (DEC_BATCH, DEPTH, PAST_LEN, H, HEAD_DIM), 1.0),
        'cache_a_v': nrm(ks[3], (DEC_BATCH, DEPTH, PAST_LEN, H, HEAD_DIM), 1.0),
        'cache_c_kv': nrm(ks[4], (DEC_BATCH, DEPTH, PAST_LEN, MLA_KV_RANK), 1.0),
        'cache_c_kpe': nrm(ks[5], (DEC_BATCH, DEPTH, PAST_LEN, MLA_ROPE), 1.0),
        'cache_d_k': nrm(ks[6], (DEC_BATCH, DEPTH, PAST_LEN, H, HEAD_DIM), 1.0),
        'cache_d_v': nrm(ks[7], (DEC_BATCH, DEPTH, PAST_LEN, H, HEAD_DIM), 1.0),
        'c': nrm(ks[8], (DEC_BATCH, D_MODEL), 1.0),
        'c_ctx': nrm(ks[9], (D_MODEL,), 1.0),
        'ada_w': nrm(ks[10], (DEPTH, D_MODEL, 3 * D_MODEL), 0.5 * D_MODEL ** -0.5),
        'ada_b': nrm(ks[11], (DEPTH, 3 * D_MODEL), 0.01),
        'norm_g': 1.0 + nrm(ks[12], (DEPTH, D_MODEL), 0.05),
        'w_in': nrm(ks[13], (DEPTH, D_MODEL, D_IN), D_MODEL ** -0.5),
        'da_lambda': nrm(ks[14], (DEPTH, 4, DA_SUB), 0.1),
        'da_subln_g': 1.0 + nrm(ks[15], (DEPTH, HEAD_DIM), 0.05),
        'conv_w': nrm(ks[16], (DEPTH, CONV_K, GROUP_W), CONV_K ** -0.5),
        'mla_q_norm_g': 1.0 + nrm(ks[17], (DEPTH, MLA_Q_RANK), 0.05),
        'mla_w_uq': nrm(ks[18], (DEPTH, MLA_Q_RANK, H * (MLA_NOPE + MLA_ROPE)), MLA_Q_RANK ** -0.5),
        'mla_kv_norm_g': 1.0 + nrm(ks[19], (DEPTH, MLA_KV_RANK), 0.05),
        'mla_w_ukv': nrm(ks[20], (DEPTH, MLA_KV_RANK, H * (MLA_NOPE + MLA_V)), MLA_KV_RANK ** -0.5),
        'na_rpb': nrm(ks[21], (DEPTH, H, 2 * NA_WIN_H - 1, 2 * NA_WIN_W - 1), 0.1),
        'w_out': nrm(ks[22], (DEPTH, D_MODEL, D_MODEL), D_MODEL ** -0.5),
        'final_norm_g': 1.0 + nrm(ks[23], (D_MODEL,), 0.05),
    }


def reference(x_prompt, x_sample, cache_a_k, cache_a_v, cache_c_kv, cache_c_kpe, cache_d_k, cache_d_v,
              c, c_ctx, ada_w, ada_b, norm_g, w_in, da_lambda, da_subln_g, conv_w,
              mla_q_norm_g, mla_w_uq, mla_kv_norm_g, mla_w_ukv, na_rpb, w_out, final_norm_g):
    xp = x_prompt
    xs = x_sample
    s_ak, s_av, s_ckv, s_kpe, s_dk, s_dv = [], [], [], [], [], []
    for l in range(DEPTH):
        p = {
            'norm_g': norm_g[l], 'w_in': w_in[l], 'da_lambda': da_lambda[l], 'da_subln_g': da_subln_g[l],
            'conv_w': conv_w[l], 'mla_q_norm_g': mla_q_norm_g[l], 'mla_w_uq': mla_w_uq[l],
            'mla_kv_norm_g': mla_kv_norm_g[l], 'mla_w_ukv': mla_w_ukv[l], 'na_rpb': na_rpb[l],
            'w_out': w_out[l],
        }
        lam_init = 0.8 - 0.6 * math.exp(-0.3 * l)
        mod_ctx = adaln(c_ctx, ada_w[l], ada_b[l])
        xp, (ak, av, ckv_n, kpe, dk, dv) = context_layer(xp, mod_ctx, p, lam_init)
        s_ak.append(ak); s_av.append(av); s_ckv.append(ckv_n)
        s_kpe.append(kpe); s_dk.append(dk); s_dv.append(dv)
        shift, scale, gate = adaln(c, ada_w[l], ada_b[l])
        mod_lat = (shift[:, None, :], scale[:, None, :], gate[:, None, :])
        ctx_l = (cache_a_k[:, l], cache_a_v[:, l], cache_c_kv[:, l], cache_c_kpe[:, l],
                 cache_d_k[:, l], cache_d_v[:, l])
        xs = latent_layer(xs, mod_lat, p, lam_init, ctx_l)
    y_prompt = rms_norm(xp, final_norm_g)
    y_sample = rms_norm(xs, final_norm_g)
    state_a_k = jnp.stack(s_ak, axis=1)
    state_a_v = jnp.stack(s_av, axis=1)
    state_c_kv = jnp.stack(s_ckv, axis=1)
    state_c_kpe = jnp.stack(s_kpe, axis=1)
    state_d_k = jnp.stack(s_dk, axis=1)
    state_d_v = jnp.stack(s_dv, axis=1)
    return (y_prompt, y_sample, state_a_k, state_a_v, state_c_kv, state_c_kpe, state_d_k, state_d_v)
```

```python
import functools
import math

import jax
import jax.numpy as jnp
import numpy as np
from jax import lax
from jax.experimental import pallas as pl
from jax.experimental.pallas import tpu as pltpu

F32 = jnp.float32
BF16 = jnp.bfloat16

D_MODEL = 1024
DEPTH = 4
GRID_W = 64
HEAD_DIM = 64
GROUP_W = 256
N_HEADS_GRP = 4
DA_SUB = 32
MLA_Q_RANK = 256
MLA_KV_RANK = 128
MLA_NOPE = 64
MLA_ROPE = 32
MLA_V = 64
MLA_SCALE = (MLA_NOPE + MLA_ROPE) ** -0.5
DA_SCALE = DA_SUB ** -0.5
NA_SCALE = HEAD_DIM ** -0.5
NA_WIN_H = 8
NA_WIN_W = 16
ROPE_BASE = 10000.0
EPS = 1e-6
NEG_INF = -1e30

LANES = 128
VMEM_LIMIT = 56 * 1024 * 1024

W_AQ, W_AK, W_AV, W_AZ = 0, 256, 512, 768
W_BB, W_BC, W_BH, W_BZ = 1024, 1280, 1536, 1792
W_CQ, W_CKV, W_KPE, W_CZ = 2048, 2304, 2432, 2560
W_DQ, W_DK, W_DV, W_DZ = 2816, 3072, 3328, 3584
D_IN_P = 3840
KPE_LANE = 64

U_QA, U_KA, U_VA, U_SZA = 0, 256, 512, 768
U_BB, U_G, U_SBZ = 1024, 1280, 1536
U_QD, U_KD, U_VD, U_SDZ = 1792, 2048, 2304, 2560
U_VC, U_SCZ = 2816, 3072
U_QC, U_KC = 3328, 3840
U_W = 4352

LOCAL_ROWS = 12
Q_ROWS = 4
TQ = Q_ROWS * GRID_W


def _dot(a, b):
    return jnp.dot(a, b, preferred_element_type=F32)


def _dot_nt(a, b):
    return lax.dot_general(a, b, (((1,), (1,)), ((), ())), preferred_element_type=F32)


def _silu(z):
    return z * (1.0 / (1.0 + jnp.exp(-z)))


def _rms(x, g):
    return x * lax.rsqrt(jnp.mean(x * x, axis=-1, keepdims=True) + EPS) * g


def _rope128(x, cos, sin_neg, sin_pos):
    return x * cos + pltpu.roll(x, LANES - 8, 1) * sin_neg + pltpu.roll(x, 8, 1) * sin_pos


def _softmax_pv(scores, values):
    m = functools.reduce(jnp.maximum, [jnp.max(s, axis=-1, keepdims=True) for s in scores])
    ps = [jnp.exp(s - m) for s in scores]
    l = functools.reduce(jnp.add, [jnp.sum(p, axis=-1, keepdims=True) for p in ps])
    o = functools.reduce(jnp.add, [_dot(p.astype(BF16), v) for p, v in zip(ps, values)])
    return o / l


def _lane_mask(lo, width):
    lane = lax.broadcasted_iota(jnp.int32, (1, LANES), 1)
    return jnp.logical_and(lane >= lo, lane < lo + width)


def _params(n_axes):
    return pltpu.CompilerParams(dimension_semantics=("arbitrary",) * n_axes,
                                vmem_limit_bytes=VMEM_LIMIT)


def _mod_map(mod_row0, nblk):
    if mod_row0 == 0:
        return lambda i: (0, 0, 0)
    return lambda i: (mod_row0 + i // nblk, 0, 0)


def _ada_kernel(c_ref, w_ref, b_ref, o_ref):
    s = _silu(c_ref[...]).astype(BF16)
    o_ref[...] = _dot(s, w_ref[...].astype(BF16)) + b_ref[...]


def _ada_call(cvec8, ada_w, ada_b):
    nj = 3
    return pl.pallas_call(
        _ada_kernel,
        grid=(DEPTH, nj),
        in_specs=[pl.BlockSpec((8, D_MODEL), lambda l, j: (0, 0)),
                  pl.BlockSpec((None, D_MODEL, D_MODEL), lambda l, j: (l, 0, j)),
                  pl.BlockSpec((None, None, 1, D_MODEL), lambda l, j: (l, j, 0, 0))],
        out_specs=pl.BlockSpec((None, 8, D_MODEL), lambda l, j: (l, 0, j)),
        out_shape=jax.ShapeDtypeStruct((DEPTH, 8, 3 * D_MODEL), F32),
        compiler_params=_params(2),
        name="adaln",
    )(cvec8, ada_w, ada_b.reshape(DEPTH, nj, 1, D_MODEL))


def _ctxkv_kernel(ckv_ref, kpe_ref, wuk_ref, wuv_ref, kc_ref, vc_ref):
    ckv = ckv_ref[...].astype(BF16)
    kn = _dot(ckv, wuk_ref[...])
    kpe = kpe_ref[...]
    for hd in range(N_HEADS_GRP):
        sl = slice(LANES * hd, LANES * (hd + 1))
        kc_ref[:, sl] = (kn[:, sl] + kpe).astype(BF16)
    vc_ref[...] = _dot(ckv, wuv_ref[...]).astype(BF16)


def _ctxkv_call(cache_c_kv, kpe_pad, wuk, wuv):
    nb, _, past, _ = cache_c_kv.shape
    return pl.pallas_call(
        _ctxkv_kernel,
        grid=(DEPTH, nb),
        in_specs=[pl.BlockSpec((None, None, past, MLA_KV_RANK), lambda l, b: (b, l, 0, 0)),
                  pl.BlockSpec((None, None, past, LANES), lambda l, b: (b, l, 0, 0)),
                  pl.BlockSpec((None, MLA_KV_RANK, 4 * LANES), lambda l, b: (l, 0, 0)),
                  pl.BlockSpec((None, MLA_KV_RANK, GROUP_W), lambda l, b: (l, 0, 0))],
        out_specs=[pl.BlockSpec((None, None, past, 4 * LANES), lambda l, b: (b, l, 0, 0)),
                   pl.BlockSpec((None, None, past, GROUP_W), lambda l, b: (b, l, 0, 0))],
        out_shape=[jax.ShapeDtypeStruct((nb, DEPTH, past, 4 * LANES), BF16),
                   jax.ShapeDtypeStruct((nb, DEPTH, past, GROUP_W), BF16)],
        compiler_params=_params(2),
        name="ctx_mla_expand",
    )(cache_c_kv, kpe_pad, wuk, wuv)


def _in_kernel(*refs, rope, states):
    it = iter(refs)
    x_ref, mod_ref, ng_ref, w_ref, qng_ref, wuq_ref, kvng_ref, wuk_ref, wuv_ref = (
        next(it) for _ in range(9))
    if rope:
        ta = [next(it)[...] for _ in range(3)]
        tc = [next(it)[...] for _ in range(3)]
    u_ref = next(it)
    if states:
        sak_ref, sav_ref, sckv_ref, skpe_ref, sdk_ref, sdv_ref = (next(it) for _ in range(6))

    x = x_ref[...]
    h = _rms(x, ng_ref[...]) * (1.0 + mod_ref[1:2, :]) + mod_ref[0:1, :]
    hb = h.astype(BF16)

    def seg(off, n=GROUP_W):
        return _dot(hb, w_ref[:, off:off + n])

    def put(off, val):
        u_ref[:, off:off + val.shape[1]] = val.astype(BF16)

    aq, ak, av = seg(W_AQ), seg(W_AK), seg(W_AV)
    if states:
        sak_ref[...] = ak
        sav_ref[...] = av
    for j in range(GROUP_W // LANES):
        sl = slice(LANES * j, LANES * (j + 1))
        qj, kj = aq[:, sl], ak[:, sl]
        if rope:
            qj, kj = _rope128(qj, *ta), _rope128(kj, *ta)
        put(U_QA + LANES * j, qj * DA_SCALE)
        put(U_KA + LANES * j, kj)
    put(U_VA, av)
    put(U_SZA, _silu(seg(W_AZ)))

    put(U_BB, seg(W_BB))
    put(U_G, seg(W_BC) * seg(W_BH))
    put(U_SBZ, _silu(seg(W_BZ)))

    cqn = _rms(seg(W_CQ), qng_ref[...]).astype(BF16)
    q = _dot(cqn, wuq_ref[...])
    ckvn = _rms(seg(W_CKV, MLA_KV_RANK), kvng_ref[...])
    kpe = seg(W_KPE, LANES)
    if states:
        sckv_ref[...] = ckvn
        skpe_ref[...] = kpe
    ckvb = ckvn.astype(BF16)
    kn = _dot(ckvb, wuk_ref[...])
    kpe_r = _rope128(kpe, *tc) if rope else kpe
    for hd in range(N_HEADS_GRP):
        sl = slice(LANES * hd, LANES * (hd + 1))
        qh = q[:, sl]
        if rope:
            qh = _rope128(qh, *tc)
        put(U_QC + LANES * hd, qh * MLA_SCALE)
        put(U_KC + LANES * hd, kn[:, sl] + kpe_r)
    put(U_VC, _dot(ckvb, wuv_ref[...]))
    put(U_SCZ, _silu(seg(W_CZ)))

    dk, dv = seg(W_DK), seg(W_DV)
    if states:
        sdk_ref[...] = dk
        sdv_ref[...] = dv
    put(U_QD, seg(W_DQ) * NA_SCALE)
    put(U_KD, dk)
    put(U_VD, dv)
    put(U_SDZ, _silu(seg(W_DZ)))


def _in_call(x, mod, norm_g, w_in, q_norm_g, wuq, kv_norm_g, wuk, wuv, tables, *, seq, tm,
             mod_row0, states, name):
    t = x.shape[0]
    nblk = seq // tm
    rope = tables is not None
    const2 = lambda i: (0, 0)
    in_specs = [
        pl.BlockSpec((tm, D_MODEL), lambda i: (i, 0)),
        pl.BlockSpec((None, 3, D_MODEL), _mod_map(mod_row0, nblk)),
        pl.BlockSpec((1, D_MODEL), const2),
        pl.BlockSpec((D_MODEL, D_IN_P), const2),
        pl.BlockSpec((1, MLA_Q_RANK), const2),
        pl.BlockSpec((MLA_Q_RANK, 4 * LANES), const2),
        pl.BlockSpec((1, MLA_KV_RANK), const2),
        pl.BlockSpec((MLA_KV_RANK, 4 * LANES), const2),
        pl.BlockSpec((MLA_KV_RANK, GROUP_W), const2),
    ]
    args = [x, mod, norm_g, w_in, q_norm_g, wuq, kv_norm_g, wuk, wuv]
    if rope:
        in_specs += [pl.BlockSpec((tm, LANES), lambda i: (i % nblk, 0))] * 6
        args += list(tables)
    out_specs = [pl.BlockSpec((tm, U_W), lambda i: (i, 0))]
    out_shape = [jax.ShapeDtypeStruct((t, U_W), BF16)]
    if states:
        for w in (GROUP_W, GROUP_W, MLA_KV_RANK, LANES, GROUP_W, GROUP_W):
            out_specs.append(pl.BlockSpec((tm, w), lambda i: (i, 0)))
            out_shape.append(jax.ShapeDtypeStruct((t, w), F32))
    return pl.pallas_call(
        functools.partial(_in_kernel, rope=rope, states=states),
        grid=(t // tm,),
        in_specs=in_specs,
        out_specs=out_specs,
        out_shape=out_shape,
        compiler_params=_params(1),
        name=name,
    )(*args)


def _attn_a_kernel(*refs, has_ctx, lam_init):
    it = iter(refs)
    q_ref, k_ref, v_ref, sz_ref = (next(it) for _ in range(4))
    if has_ctx:
        kc_ref, vc_ref = next(it), next(it)
    lam_ref, g_ref, o_ref = next(it), next(it), next(it)

    q = q_ref[...]
    keys, vals = [k_ref[...]], [v_ref[...]]
    if has_ctx:
        keys.append(kc_ref[...].astype(BF16))
        vals.append(vc_ref[...].astype(BF16))
    lv = lam_ref[...]
    lam = (jnp.exp(jnp.sum(lv[0:1] * lv[1:2], keepdims=True))
           - jnp.exp(jnp.sum(lv[2:3] * lv[3:4], keepdims=True)) + lam_init)
    heads = []
    for e in range(2):
        maps = []
        for c in range(2):
            qm = jnp.where(_lane_mask(HEAD_DIM * e + DA_SUB * c, DA_SUB), q, jnp.zeros_like(q))
            maps.append(_softmax_pv([_dot_nt(qm, k) for k in keys], vals))
        heads.append(maps[0] - lam * maps[1])
    first = _lane_mask(0, HEAD_DIM)
    o = jnp.where(first, heads[0], heads[1])
    o2 = o * o
    ss0 = jnp.sum(jnp.where(first, o2, 0.0), axis=-1, keepdims=True)
    ss1 = jnp.sum(jnp.where(first, 0.0, o2), axis=-1, keepdims=True)
    ms = jnp.where(first, ss0, ss1) * (1.0 / HEAD_DIM)
    y = o * lax.rsqrt(ms + EPS) * g_ref[...] * (1.0 - lam_init)
    o_ref[...] = (y * sz_ref[...].astype(F32)).astype(BF16)


def _attn_a_call(u, cache_k, cache_v, lam_vecs, subln_g2, *, layer, lam_init, nb, seq, tq, name):
    has_ctx = cache_k is not None
    nq = seq // tq
    cb = lambda off: off // LANES
    in_specs = [
        pl.BlockSpec((tq, LANES), lambda b, hp, i: (b * nq + i, cb(U_QA) + hp)),
        pl.BlockSpec((seq, LANES), lambda b, hp, i: (b, cb(U_KA) + hp)),
        pl.BlockSpec((seq, LANES), lambda b, hp, i: (b, cb(U_VA) + hp)),
        pl.BlockSpec((tq, LANES), lambda b, hp, i: (b * nq + i, cb(U_SZA) + hp)),
    ]
    args = [u, u, u, u]
    if has_ctx:
        past = cache_k.shape[2]
        spec = pl.BlockSpec((None, None, past, LANES), lambda b, hp, i: (b, layer, 0, hp))
        in_specs += [spec, spec]
        args += [cache_k, cache_v]
    in_specs += [pl.BlockSpec((4, DA_SUB), lambda b, hp, i: (0, 0)),
                 pl.BlockSpec((1, LANES), lambda b, hp, i: (0, 0))]
    args += [lam_vecs, subln_g2]
    return pl.pallas_call(
        functools.partial(_attn_a_kernel, has_ctx=has_ctx, lam_init=lam_init),
        grid=(nb, 2, nq),
        in_specs=in_specs,
        out_specs=pl.BlockSpec((tq, LANES), lambda b, hp, i: (b * nq + i, hp)),
        out_shape=jax.ShapeDtypeStruct((nb * seq, GROUP_W), BF16),
        compiler_params=_params(3),
        name=name,
    )(*args)


def _attn_c_kernel(*refs, has_ctx):
    it = iter(refs)
    q_ref, k_ref, v_ref, sz_ref = (next(it) for _ in range(4))
    if has_ctx:
        kc_ref, vc_ref = next(it), next(it)
    o_ref = next(it)
    vals = [v_ref[...]] + ([vc_ref[...]] if has_ctx else [])
    heads = []
    for e in range(2):
        sl = slice(LANES * e, LANES * (e + 1))
        qe = q_ref[:, sl]
        scores = [_dot_nt(qe, k_ref[:, sl])]
        if has_ctx:
            scores.append(_dot_nt(qe, kc_ref[:, sl]))
        heads.append(_softmax_pv(scores, vals))
    o = jnp.where(_lane_mask(0, MLA_V), heads[0], heads[1])
    o_ref[...] = (o * sz_ref[...].astype(F32)).astype(BF16)


def _attn_c_call(u, kc_ctx, vc_ctx, *, layer, nb, seq, tq, name):
    has_ctx = kc_ctx is not None
    nq = seq // tq
    in_specs = [
        pl.BlockSpec((tq, 2 * LANES), lambda b, hp, i: (b * nq + i, U_QC // (2 * LANES) + hp)),
        pl.BlockSpec((seq, 2 * LANES), lambda b, hp, i: (b, U_KC // (2 * LANES) + hp)),
        pl.BlockSpec((seq, LANES), lambda b, hp, i: (b, U_VC // LANES + hp)),
        pl.BlockSpec((tq, LANES), lambda b, hp, i: (b * nq + i, U_SCZ // LANES + hp)),
    ]
    args = [u, u, u, u]
    if has_ctx:
        past = kc_ctx.shape[2]
        in_specs += [pl.BlockSpec((None, None, past, 2 * LANES), lambda b, hp, i: (b, layer, 0, hp)),
                     pl.BlockSpec((None, None, past, LANES), lambda b, hp, i: (b, layer, 0, hp))]
        args += [kc_ctx, vc_ctx]
    return pl.pallas_call(
        functools.partial(_attn_c_kernel, has_ctx=has_ctx),
        grid=(nb, 2, nq),
        in_specs=in_specs,
        out_specs=pl.BlockSpec((tq, LANES), lambda b, hp, i: (b * nq + i, hp)),
        out_shape=jax.ShapeDtypeStruct((nb * seq, GROUP_W), BF16),
        compiler_params=_params(3),
        name=name,
    )(*args)


def _attn_d_kernel(*refs, local):
    it = iter(refs)
    q_ref, k_ref, v_ref, sz_ref = (next(it) for _ in range(4))
    if local:
        kc_ref, vc_ref, bias_ref = next(it), next(it), next(it)
    o_ref = next(it)
    q = q_ref[...]
    if local:
        i = pl.program_id(2)
        row0 = jnp.clip(Q_ROWS * i - NA_WIN_H // 2, 0, k_ref.shape[0] // GRID_W - LOCAL_ROWS)
        start = pl.multiple_of(row0 * GRID_W, TQ)
        keys = [k_ref[pl.ds(start, LOCAL_ROWS * GRID_W), :], kc_ref[...].astype(BF16)]
        vals = [v_ref[pl.ds(start, LOCAL_ROWS * GRID_W), :], vc_ref[...].astype(BF16)]
    else:
        keys, vals = [k_ref[...]], [v_ref[...]]
    heads = []
    for e in range(2):
        qm = jnp.where(_lane_mask(HEAD_DIM * e, HEAD_DIM), q, jnp.zeros_like(q))
        scores = [_dot_nt(qm, k) for k in keys]
        if local:
            scores[0] = scores[0] + bias_ref[e]
        heads.append(_softmax_pv(scores, vals))
    o = jnp.where(_lane_mask(0, HEAD_DIM), heads[0], heads[1])
    o_ref[...] = (o * sz_ref[...].astype(F32)).astype(BF16)


def _attn_d_call(u, cache_k, cache_v, bias, *, layer, nb, seq, tq, name):
    local = cache_k is not None
    nq = seq // tq
    cb = lambda off: off // LANES
    in_specs = [
        pl.BlockSpec((tq, LANES), lambda b, hp, i: (b * nq + i, cb(U_QD) + hp)),
        pl.BlockSpec((seq, LANES), lambda b, hp, i: (b, cb(U_KD) + hp)),
        pl.BlockSpec((seq, LANES), lambda b, hp, i: (b, cb(U_VD) + hp)),
        pl.BlockSpec((tq, LANES), lambda b, hp, i: (b * nq + i, cb(U_SDZ) + hp)),
    ]
    args = [u, u, u, u]
    if local:
        past = cache_k.shape[2]
        spec = pl.BlockSpec((None, None, past, LANES), lambda b, hp, i: (b, layer, 0, hp))

        def bias_map(b, hp, i):
            pattern = jnp.where(i == 0, 0, jnp.where(i == nq - 1, 2, 1))
            return (pattern, hp, 0, 0)

        in_specs += [spec, spec,
                     pl.BlockSpec((None, 2, tq, LOCAL_ROWS * GRID_W), bias_map)]
        args += [cache_k, cache_v, bias]
    return pl.pallas_call(
        functools.partial(_attn_d_kernel, local=local),
        grid=(nb, 2, nq),
        in_specs=in_specs,
        out_specs=pl.BlockSpec((tq, LANES), lambda b, hp, i: (b * nq + i, hp)),
        out_shape=jax.ShapeDtypeStruct((nb * seq, GROUP_W), BF16),
        compiler_params=_params(3),
        name=name,
    )(*args)


def _local_bias_tables(rpb, rows):
    nq = rows // Q_ROWS
    a = np.arange(TQ)
    b = np.arange(LOCAL_ROWS * GRID_W)
    tabs = []
    for i in (0, 1, nq - 1):
        row0 = int(np.clip(Q_ROWS * i - NA_WIN_H // 2, 0, rows - LOCAL_ROWS))
        r = (Q_ROWS * i + a // GRID_W)[:, None]
        qc = (a % GRID_W)[:, None]
        kr = (row0 + b // GRID_W)[None, :]
        kc = (b % GRID_W)[None, :]
        rs = np.clip(r - NA_WIN_H // 2, 0, rows - NA_WIN_H)
        ws = np.clip(qc - NA_WIN_W // 2, 0, GRID_W - NA_WIN_W)
        valid = (kr >= rs) & (kr < rs + NA_WIN_H) & (kc >= ws) & (kc < ws + NA_WIN_W)
        dy = np.clip(kr - r + NA_WIN_H - 1, 0, 2 * NA_WIN_H - 2)
        dx = np.clip(kc - qc + NA_WIN_W - 1, 0, 2 * NA_WIN_W - 2)
        tabs.append(jnp.where(valid[None], rpb[:, dy, dx].astype(F32), NEG_INF))
    return jnp.stack(tabs)


def _out_kernel(x_ref, mod_ref, ya_ref, bb_ref, g_ref, gp_ref, gn_ref, sbz_ref, yc_ref, yd_ref,
                cw_ref, w_ref, fg_ref, o_ref, *, nblk, final):
    tm = x_ref.shape[0]
    pos = pl.program_id(0) % nblk
    has_prev = jnp.where(pos != 0, 1.0, 0.0)
    has_next = jnp.where(pos != nblk - 1, 1.0, 0.0)
    g = g_ref[...].astype(F32)
    rows = lax.broadcasted_iota(jnp.int32, (tm, 1), 0)
    halo = gp_ref.shape[0]
    g_prev = jnp.where(rows == 0, gp_ref[halo - 1:halo, :].astype(F32) * has_prev,
                       pltpu.roll(g, 1, 0))
    g_next = jnp.where(rows == tm - 1, gn_ref[0:1, :].astype(F32) * has_next,
                       pltpu.roll(g, tm - 1, 0))
    cw = cw_ref[...]
    conv = g_prev * cw[0:1] + g * cw[1:2] + g_next * cw[2:3]
    yb = (bb_ref[...].astype(F32) * conv * sbz_ref[...].astype(F32)).astype(BF16)
    proj = (_dot(ya_ref[...], w_ref[0:GROUP_W, :]) + _dot(yb, w_ref[GROUP_W:2 * GROUP_W, :])
            + _dot(yc_ref[...], w_ref[2 * GROUP_W:3 * GROUP_W, :])
            + _dot(yd_ref[...], w_ref[3 * GROUP_W:, :]))
    xn = x_ref[...] + mod_ref[2:3, :] * proj
    if final:
        xn = _rms(xn, fg_ref[...])
    o_ref[...] = xn


def _out_call(x, mod, u, ya, yc, yd, conv_w, w_out, final_g, *, seq, tm, mod_row0, final, name):
    t = x.shape[0]
    nblk = seq // tm
    halo = 16
    hb = tm // halo
    last = t // halo - 1
    cb = lambda off: off // GROUP_W
    row = lambda i: (i, 0)
    const2 = lambda i: (0, 0)
    in_specs = [
        pl.BlockSpec((tm, D_MODEL), row),
        pl.BlockSpec((None, 3, D_MODEL), _mod_map(mod_row0, nblk)),
        pl.BlockSpec((tm, GROUP_W), row),
        pl.BlockSpec((tm, GROUP_W), lambda i: (i, cb(U_BB))),
        pl.BlockSpec((tm, GROUP_W), lambda i: (i, cb(U_G))),
        pl.BlockSpec((halo, GROUP_W), lambda i: (jnp.maximum(i * hb - 1, 0), cb(U_G))),
        pl.BlockSpec((halo, GROUP_W), lambda i: (jnp.minimum((i + 1) * hb, last), cb(U_G))),
        pl.BlockSpec((tm, GROUP_W), lambda i: (i, cb(U_SBZ))),
        pl.BlockSpec((tm, GROUP_W), row),
        pl.BlockSpec((tm, GROUP_W), row),
        pl.BlockSpec((3, GROUP_W), const2),
        pl.BlockSpec((D_MODEL, D_MODEL), const2),
        pl.BlockSpec((1, D_MODEL), const2),
    ]
    return pl.pallas_call(
        functools.partial(_out_kernel, nblk=nblk, final=final),
        grid=(t // tm,),
        in_specs=in_specs,
        out_specs=pl.BlockSpec((tm, D_MODEL), row),
        out_shape=jax.ShapeDtypeStruct((t, D_MODEL), F32),
        compiler_params=_params(1),
        name=name,
    )(x, mod, ya, u, u, u, u, u, yc, yd, conv_w, w_out, final_g)


def _rope_tables(seq):
    t = jnp.arange(seq)
    rows = (t // GRID_W).astype(F32)
    cols = (t % GRID_W).astype(F32)
    half = DA_SUB // 2
    inv = 1.0 / (ROPE_BASE ** (jnp.arange(0, half, 2, dtype=F32) / half))
    ar = rows[:, None] * inv
    ac = cols[:, None] * inv
    ang = jnp.concatenate([ar, ar, ac, ac], axis=-1)
    cos, sin = jnp.cos(ang), jnp.sin(ang)
    first = (np.arange(DA_SUB) % 16 < 8)[None, :]
    sin_neg = jnp.where(first, -sin, 0.0)
    sin_pos = jnp.where(first, 0.0, sin)
    tile = lambda a: jnp.tile(a, (1, LANES // DA_SUB))

    def pad(a, fill):
        return jnp.concatenate([jnp.full((seq, KPE_LANE), fill, F32), a,
                                jnp.full((seq, LANES - KPE_LANE - MLA_ROPE), fill, F32)], axis=1)

    return ([tile(cos), tile(sin_neg), tile(sin_pos)],
            [pad(cos, 1.0), pad(sin_neg, 0.0), pad(sin_pos, 0.0)])


def _pad_heads(w, width, take):
    d, k, _ = w.shape
    w = w.reshape(d, k, N_HEADS_GRP, width)[..., :take]
    return jnp.pad(w, ((0, 0), (0, 0), (0, 0), (0, LANES - take))).reshape(d, k, N_HEADS_GRP * LANES)


def kernel(x_prompt, x_sample, cache_a_k, cache_a_v, cache_c_kv, cache_c_kpe, cache_d_k, cache_d_v,
           c, c_ctx, ada_w, ada_b, norm_g, w_in, da_lambda, da_subln_g, conv_w,
           mla_q_norm_g, mla_w_uq, mla_kv_norm_g, mla_w_ukv, na_rpb, w_out, final_norm_g):
    batch, seq, _ = x_prompt.shape
    dec_batch, dec_seq, _ = x_sample.shape
    past = cache_a_k.shape[2]
    assert dec_seq % TQ == 0 and dec_seq // GRID_W >= LOCAL_ROWS and seq % 16 == 0

    kpe_lo, kpe_hi = 2432, 2464
    w_in_p = jnp.concatenate(
        [w_in[..., :kpe_lo], jnp.zeros((DEPTH, D_MODEL, KPE_LANE), F32), w_in[..., kpe_lo:kpe_hi],
         jnp.zeros((DEPTH, D_MODEL, LANES - KPE_LANE - MLA_ROPE), F32), w_in[..., kpe_hi:]],
        axis=-1).astype(BF16)
    w_out_b = w_out.astype(BF16)
    wuq = _pad_heads(mla_w_uq, MLA_NOPE + MLA_ROPE, MLA_NOPE + MLA_ROPE).astype(BF16)
    wuk = _pad_heads(mla_w_ukv, MLA_NOPE + MLA_V, MLA_NOPE).astype(BF16)
    wuv = mla_w_ukv.reshape(DEPTH, MLA_KV_RANK, N_HEADS_GRP, MLA_NOPE + MLA_V)[..., MLA_NOPE:]
    wuv = wuv.reshape(DEPTH, MLA_KV_RANK, GROUP_W).astype(BF16)
    subln_g2 = jnp.tile(da_subln_g, (1, LANES // HEAD_DIM)).reshape(DEPTH, 1, LANES)
    tables_a, tables_c = _rope_tables(dec_seq)
    tables = tables_a + tables_c

    cvec8 = jnp.concatenate([c_ctx[None], c, jnp.zeros((8 - 1 - dec_batch, D_MODEL), F32)], axis=0)
    mod = _ada_call(cvec8, ada_w, ada_b).reshape(DEPTH, 8, 3, D_MODEL)

    ca_k = cache_a_k.reshape(dec_batch, DEPTH, past, GROUP_W)
    ca_v = cache_a_v.reshape(dec_batch, DEPTH, past, GROUP_W)
    cd_k = cache_d_k.reshape(dec_batch, DEPTH, past, GROUP_W)
    cd_v = cache_d_v.reshape(dec_batch, DEPTH, past, GROUP_W)
    kpe_pad = jnp.pad(cache_c_kpe, ((0, 0), (0, 0), (0, 0), (KPE_LANE, LANES - KPE_LANE - MLA_ROPE)))
    kc_ctx, vc_ctx = _ctxkv_call(cache_c_kv, kpe_pad, wuk, wuv)

    xp = x_prompt.reshape(batch * seq, D_MODEL)
    xs = x_sample.reshape(dec_batch * dec_seq, D_MODEL)
    states = []
    for l in range(DEPTH):
        lam_init = 0.8 - 0.6 * math.exp(-0.3 * l)
        final = l == DEPTH - 1
        ng = norm_g[l].reshape(1, D_MODEL)
        qng = mla_q_norm_g[l].reshape(1, MLA_Q_RANK)
        kvng = mla_kv_norm_g[l].reshape(1, MLA_KV_RANK)
        fg = final_norm_g.reshape(1, D_MODEL)

        u, *st = _in_call(xp, mod[l], ng, w_in_p[l], qng, wuq[l], kvng, wuk[l], wuv[l], None,
                          seq=seq, tm=seq, mod_row0=0, states=True, name=f"ctx_in_{l}")
        states.append(st)
        ya = _attn_a_call(u, None, None, da_lambda[l], subln_g2[l], layer=l, lam_init=lam_init,
                          nb=batch, seq=seq, tq=seq, name=f"ctx_attn_a_{l}")
        yc = _attn_c_call(u, None, None, layer=l, nb=batch, seq=seq, tq=seq, name=f"ctx_attn_c_{l}")
        yd = _attn_d_call(u, None, None, None, layer=l, nb=batch, seq=seq, tq=seq,
                          name=f"ctx_attn_d_{l}")
        xp = _out_call(xp, mod[l], u, ya, yc, yd, conv_w[l], w_out_b[l], fg, seq=seq, tm=seq,
                       mod_row0=0, final=final, name=f"ctx_out_{l}")

        (u,) = _in_call(xs, mod[l], ng, w_in_p[l], qng, wuq[l], kvng, wuk[l], wuv[l], tables,
                        seq=dec_seq, tm=512, mod_row0=1, states=False, name=f"lat_in_{l}")
        ya = _attn_a_call(u, ca_k, ca_v, da_lambda[l], subln_g2[l], layer=l, lam_init=lam_init,
                          nb=dec_batch, seq=dec_seq, tq=TQ, name=f"lat_attn_a_{l}")
        yc = _attn_c_call(u, kc_ctx, vc_ctx, layer=l, nb=dec_batch, seq=dec_seq, tq=TQ,
                          name=f"lat_attn_c_{l}")
        bias = _local_bias_tables(na_rpb[l], dec_seq // GRID_W)
        yd = _attn_d_call(u, cd_k, cd_v, bias, layer=l, nb=dec_batch, seq=dec_seq, tq=TQ,
                          name=f"lat_attn_d_{l}")
        xs = _out_call(xs, mod[l], u, ya, yc, yd, conv_w[l], w_out_b[l], fg, seq=dec_seq, tm=512,
                       mod_row0=1, final=final, name=f"lat_out_{l}")

    def stack(idx, shape):
        return jnp.stack([s[idx] for s in states], axis=0).reshape(
            (DEPTH, batch, seq) + shape).swapaxes(0, 1)

    state_a_k = stack(0, (N_HEADS_GRP, HEAD_DIM))
    state_a_v = stack(1, (N_HEADS_GRP, HEAD_DIM))
    state_c_kv = stack(2, (MLA_KV_RANK,))
    state_c_kpe = stack(3, (LANES,))[..., KPE_LANE:KPE_LANE + MLA_ROPE]
    state_d_k = stack(4, (N_HEADS_GRP, HEAD_DIM))
    state_d_v = stack(5, (N_HEADS_GRP, HEAD_DIM))
    return (xp.reshape(batch, seq, D_MODEL), xs.reshape(dec_batch, dec_seq, D_MODEL),
            state_a_k, state_a_v, state_c_kv, state_c_kpe, state_d_k, state_d_v)
```

```python
import functools
import math

import jax
import jax.numpy as jnp
import numpy as np
from jax import lax
from jax.experimental import pallas as pl
from jax.experimental.pallas import tpu as pltpu

F32 = jnp.float32
BF16 = jnp.bfloat16

D_MODEL = 1024
DEPTH = 4
GRID_W = 64
HEAD_DIM = 64
GROUP_W = 256
N_HEADS_GRP = 4
DA_SUB = 32
MLA_Q_RANK = 256
MLA_KV_RANK = 128
MLA_NOPE = 64
MLA_ROPE = 32
MLA_V = 64
MLA_SCALE = (MLA_NOPE + MLA_ROPE) ** -0.5
DA_SCALE = DA_SUB ** -0.5
NA_SCALE = HEAD_DIM ** -0.5
NA_WIN_H = 8
NA_WIN_W = 16
ROPE_BASE = 10000.0
EPS = 1e-6
NEG_INF = -1e30

LANES = 128
VMEM_LIMIT = 56 * 1024 * 1024

W_AQ, W_AK, W_AV, W_AZ = 0, 256, 512, 768
W_BB, W_BC, W_BH, W_BZ = 1024, 1280, 1536, 1792
W_CQ, W_CKV, W_KPE, W_CZ = 2048, 2304, 2432, 2560
W_DQ, W_DK, W_DV, W_DZ = 2816, 3072, 3328, 3584
D_IN_P = 3840
KPE_LANE = 64

U_QA, U_KA, U_VA, U_SZA = 0, 256, 512, 768
U_BB, U_G, U_SBZ = 1024, 1280, 1536
U_QD, U_KD, U_VD, U_SDZ = 1792, 2048, 2304, 2560
U_VC, U_SCZ = 2816, 3072
U_QC, U_KC = 3328, 3840
U_W = 4352

LOCAL_ROWS = 12
Q_ROWS = 4
TQ = Q_ROWS * GRID_W


def _dot(a, b):
    return jnp.dot(a, b, preferred_element_type=F32)


def _dot_nt(a, b):
    return lax.dot_general(a, b, (((1,), (1,)), ((), ())), preferred_element_type=F32)


def _silu(z):
    return z * (1.0 / (1.0 + jnp.exp(-z)))


def _rms(x, g):
    return x * lax.rsqrt(jnp.mean(x * x, axis=-1, keepdims=True) + EPS) * g


def _rope128(x, cos, sin_neg, sin_pos):
    return x * cos + pltpu.roll(x, LANES - 8, 1) * sin_neg + pltpu.roll(x, 8, 1) * sin_pos


def _softmax_pv(scores, values):
    m = functools.reduce(jnp.maximum, [jnp.max(s, axis=-1, keepdims=True) for s in scores])
    ps = [jnp.exp(s - m) for s in scores]
    l = functools.reduce(jnp.add, [jnp.sum(p, axis=-1, keepdims=True) for p in ps])
    o = functools.reduce(jnp.add, [_dot(p.astype(BF16), v) for p, v in zip(ps, values)])
    return o / l


def _lane_mask(lo, width):
    lane = lax.broadcasted_iota(jnp.int32, (1, LANES), 1)
    return jnp.logical_and(lane >= lo, lane < lo + width)


def _params(n_axes):
    return pltpu.CompilerParams(dimension_semantics=("arbitrary",) * n_axes,
                                vmem_limit_bytes=VMEM_LIMIT)


def _mod_map(mod_row0, nblk):
    if mod_row0 == 0:
        return lambda i: (0, 0, 0)
    return lambda i: (mod_row0 + i // nblk, 0, 0)


def _ada_kernel(c_ref, w_ref, b_ref, o_ref):
    s = _silu(c_ref[...]).astype(BF16)
    o_ref[...] = _dot(s, w_ref[...].astype(BF16)) + b_ref[...]


def _ada_call(cvec8, ada_w, ada_b):
    nj = 3
    return pl.pallas_call(
        _ada_kernel,
        grid=(DEPTH, nj),
        in_specs=[pl.BlockSpec((8, D_MODEL), lambda l, j: (0, 0)),
                  pl.BlockSpec((None, D_MODEL, D_MODEL), lambda l, j: (l, 0, j)),
                  pl.BlockSpec((None, None, 1, D_MODEL), lambda l, j: (l, j, 0, 0))],
        out_specs=pl.BlockSpec((None, 8, D_MODEL), lambda l, j: (l, 0, j)),
        out_shape=jax.ShapeDtypeStruct((DEPTH, 8, 3 * D_MODEL), F32),
        compiler_params=_params(2),
        name="adaln",
    )(cvec8, ada_w, ada_b.reshape(DEPTH, nj, 1, D_MODEL))


def _ctxkv_kernel(ckv_ref, kpe_ref, wuk_ref, wuv_ref, kc_ref, vc_ref):
    ckv = ckv_ref[...].astype(BF16)
    kn = _dot(ckv, wuk_ref[...])
    kpe = kpe_ref[...]
    for hd in range(N_HEADS_GRP):
        sl = slice(LANES * hd, LANES * (hd + 1))
        kc_ref[:, sl] = (kn[:, sl] + kpe).astype(BF16)
    vc_ref[...] = _dot(ckv, wuv_ref[...]).astype(BF16)


def _ctxkv_call(cache_c_kv, kpe_pad, wuk, wuv):
    nb, _, past, _ = cache_c_kv.shape
    return pl.pallas_call(
        _ctxkv_kernel,
        grid=(DEPTH, nb),
        in_specs=[pl.BlockSpec((None, None, past, MLA_KV_RANK), lambda l, b: (b, l, 0, 0)),
                  pl.BlockSpec((None, None, past, LANES), lambda l, b: (b, l, 0, 0)),
                  pl.BlockSpec((None, MLA_KV_RANK, 4 * LANES), lambda l, b: (l, 0, 0)),
                  pl.BlockSpec((None, MLA_KV_RANK, GROUP_W), lambda l, b: (l, 0, 0))],
        out_specs=[pl.BlockSpec((None, None, past, 4 * LANES), lambda l, b: (b, l, 0, 0)),
                   pl.BlockSpec((None, None, past, GROUP_W), lambda l, b: (b, l, 0, 0))],
        out_shape=[jax.ShapeDtypeStruct((nb, DEPTH, past, 4 * LANES), BF16),
                   jax.ShapeDtypeStruct((nb, DEPTH, past, GROUP_W), BF16)],
        compiler_params=_params(2),
        name="ctx_mla_expand",
    )(cache_c_kv, kpe_pad, wuk, wuv)


def _in_kernel(*refs, rope, states):
    it = iter(refs)
    x_ref, mod_ref, ng_ref, w_ref, qng_ref, wuq_ref, kvng_ref, wuk_ref, wuv_ref = (
        next(it) for _ in range(9))
    if rope:
        ta = [next(it)[...] for _ in range(3)]
        tc = [next(it)[...] for _ in range(3)]
    u_ref = next(it)
    if states:
        sak_ref, sav_ref, sckv_ref, skpe_ref, sdk_ref, sdv_ref = (next(it) for _ in range(6))

    x = x_ref[...]
    h = _rms(x, ng_ref[...]) * (1.0 + mod_ref[1:2, :]) + mod_ref[0:1, :]
    hb = h.astype(BF16)

    def seg(off, n=GROUP_W):
        return _dot(hb, w_ref[:, off:off + n])

    def put(off, val):
        u_ref[:, off:off + val.shape[1]] = val.astype(BF16)

    aq, ak, av = seg(W_AQ), seg(W_AK), seg(W_AV)
    if states:
        sak_ref[...] = ak
        sav_ref[...] = av
    for j in range(GROUP_W // LANES):
        sl = slice(LANES * j, LANES * (j + 1))
        qj, kj = aq[:, sl], ak[:, sl]
        if rope:
            qj, kj = _rope128(qj, *ta), _rope128(kj, *ta)
        put(U_QA + LANES * j, qj * DA_SCALE)
        put(U_KA + LANES * j, kj)
    put(U_VA, av)
    put(U_SZA, _silu(seg(W_AZ)))

    put(U_BB, seg(W_BB))
    put(U_G, seg(W_BC) * seg(W_BH))
    put(U_SBZ, _silu(seg(W_BZ)))

    cqn = _rms(seg(W_CQ), qng_ref[...]).astype(BF16)
    q = _dot(cqn, wuq_ref[...])
    ckvn = _rms(seg(W_CKV, MLA_KV_RANK), kvng_ref[...])
    kpe = seg(W_KPE, LANES)
    if states:
        sckv_ref[...] = ckvn
        skpe_ref[...] = kpe
    ckvb = ckvn.astype(BF16)
    kn = _dot(ckvb, wuk_ref[...])
    kpe_r = _rope128(kpe, *tc) if rope else kpe
    for hd in range(N_HEADS_GRP):
        sl = slice(LANES * hd, LANES * (hd + 1))
        qh = q[:, sl]
        if rope:
            qh = _rope128(qh, *tc)
        put(U_QC + LANES * hd, qh * MLA_SCALE)
        put(U_KC + LANES * hd, kn[:, sl] + kpe_r)
    put(U_VC, _dot(ckvb, wuv_ref[...]))
    put(U_SCZ, _silu(seg(W_CZ)))

    dk, dv = seg(W_DK), seg(W_DV)
    if states:
        sdk_ref[...] = dk
        sdv_ref[...] = dv
    put(U_QD, seg(W_DQ) * NA_SCALE)
    put(U_KD, dk)
    put(U_VD, dv)
    put(U_SDZ, _silu(seg(W_DZ)))


def _in_call(x, mod, norm_g, w_in, q_norm_g, wuq, kv_norm_g, wuk, wuv, tables, *, seq, tm,
             mod_row0, states, name):
    t = x.shape[0]
    nblk = seq // tm
    rope = tables is not None
    const2 = lambda i: (0, 0)
    in_specs = [
        pl.BlockSpec((tm, D_MODEL), lambda i: (i, 0)),
        pl.BlockSpec((None, 3, D_MODEL), _mod_map(mod_row0, nblk)),
        pl.BlockSpec((1, D_MODEL), const2),
        pl.BlockSpec((D_MODEL, D_IN_P), const2),
        pl.BlockSpec((1, MLA_Q_RANK), const2),
        pl.BlockSpec((MLA_Q_RANK, 4 * LANES), const2),
        pl.BlockSpec((1, MLA_KV_RANK), const2),
        pl.BlockSpec((MLA_KV_RANK, 4 * LANES), const2),
        pl.BlockSpec((MLA_KV_RANK, GROUP_W), const2),
    ]
    args = [x, mod, norm_g, w_in, q_norm_g, wuq, kv_norm_g, wuk, wuv]
    if rope:
        in_specs += [pl.BlockSpec((tm, LANES), lambda i: (i % nblk, 0))] * 6
        args += list(tables)
    out_specs = [pl.BlockSpec((tm, U_W), lambda i: (i, 0))]
    out_shape = [jax.ShapeDtypeStruct((t, U_W), BF16)]
    if states:
        for w in (GROUP_W, GROUP_W, MLA_KV_RANK, LANES, GROUP_W, GROUP_W):
            out_specs.append(pl.BlockSpec((tm, w), lambda i: (i, 0)))
            out_shape.append(jax.ShapeDtypeStruct((t, w), F32))
    return pl.pallas_call(
        functools.partial(_in_kernel, rope=rope, states=states),
        grid=(t // tm,),
        in_specs=in_specs,
        out_specs=out_specs,
        out_shape=out_shape,
        compiler_params=_params(1),
        name=name,
    )(*args)


def _attn_a_kernel(*refs, has_ctx, lam_init):
    it = iter(refs)
    q_ref, k_ref, v_ref, sz_ref = (next(it) for _ in range(4))
    if has_ctx:
        kc_ref, vc_ref = next(it), next(it)
    lam_ref, g_ref, o_ref = next(it), next(it), next(it)

    q = q_ref[...]
    keys, vals = [k_ref[...]], [v_ref[...]]
    if has_ctx:
        keys.append(kc_ref[...].astype(BF16))
        vals.append(vc_ref[...].astype(BF16))
    lv = lam_ref[...]
    lam = (jnp.exp(jnp.sum(lv[0:1] * lv[1:2], keepdims=True))
           - jnp.exp(jnp.sum(lv[2:3] * lv[3:4], keepdims=True)) + lam_init)
    heads = []
    for e in range(2):
        maps = []
        for c in range(2):
            qm = jnp.where(_lane_mask(HEAD_DIM * e + DA_SUB * c, DA_SUB), q, jnp.zeros_like(q))
            maps.append(_softmax_pv([_dot_nt(qm, k) for k in keys], vals))
        heads.append(maps[0] - lam * maps[1])
    first = _lane_mask(0, HEAD_DIM)
    o = jnp.where(first, heads[0], heads[1])
    o2 = o * o
    ss0 = jnp.sum(jnp.where(first, o2, 0.0), axis=-1, keepdims=True)
    ss1 = jnp.sum(jnp.where(first, 0.0, o2), axis=-1, keepdims=True)
    ms = jnp.where(first, ss0, ss1) * (1.0 / HEAD_DIM)
    y = o * lax.rsqrt(ms + EPS) * g_ref[...] * (1.0 - lam_init)
    o_ref[...] = (y * sz_ref[...].astype(F32)).astype(BF16)


def _attn_a_call(u, cache_k, cache_v, lam_vecs, subln_g2, *, layer, lam_init, nb, seq, tq, name):
    has_ctx = cache_k is not None
    nq = seq // tq
    cb = lambda off: off // LANES
    in_specs = [
        pl.BlockSpec((tq, LANES), lambda b, hp, i: (b * nq + i, cb(U_QA) + hp)),
        pl.BlockSpec((seq, LANES), lambda b, hp, i: (b, cb(U_KA) + hp)),
        pl.BlockSpec((seq, LANES), lambda b, hp, i: (b, cb(U_VA) + hp)),
        pl.BlockSpec((tq, LANES), lambda b, hp, i: (b * nq + i, cb(U_SZA) + hp)),
    ]
    args = [u, u, u, u]
    if has_ctx:
        past = cache_k.shape[2]
        spec = pl.BlockSpec((None, None, past, LANES), lambda b, hp, i: (b, layer, 0, hp))
        in_specs += [spec, spec]
        args += [cache_k, cache_v]
    in_specs += [pl.BlockSpec((4, DA_SUB), lambda b, hp, i: (0, 0)),
                 pl.BlockSpec((1, LANES), lambda b, hp, i: (0, 0))]
    args += [lam_vecs, subln_g2]
    return pl.pallas_call(
        functools.partial(_attn_a_kernel, has_ctx=has_ctx, lam_init=lam_init),
        grid=(nb, 2, nq),
        in_specs=in_specs,
        out_specs=pl.BlockSpec((tq, LANES), lambda b, hp, i: (b * nq + i, hp)),
        out_shape=jax.ShapeDtypeStruct((nb * seq, GROUP_W), BF16),
        compiler_params=_params(3),
        name=name,
    )(*args)


def _attn_c_kernel(*refs, has_ctx):
    it = iter(refs)
    q_ref, k_ref, v_ref, sz_ref = (next(it) for _ in range(4))
    if has_ctx:
        kc_ref, vc_ref = next(it), next(it)
    o_ref = next(it)
    vals = [v_ref[...]] + ([vc_ref[...]] if has_ctx else [])
    heads = []
    for e in range(2):
        sl = slice(LANES * e, LANES * (e + 1))
        qe = q_ref[:, sl]
        scores = [_dot_nt(qe, k_ref[:, sl])]
        if has_ctx:
            scores.append(_dot_nt(qe, kc_ref[:, sl]))
        heads.append(_softmax_pv(scores, vals))
    o = jnp.where(_lane_mask(0, MLA_V), heads[0], heads[1])
    o_ref[...] = (o * sz_ref[...].astype(F32)).astype(BF16)


def _attn_c_call(u, kc_ctx, vc_ctx, *, layer, nb, seq, tq, name):
    has_ctx = kc_ctx is not None
    nq = seq // tq
    in_specs = [
        pl.BlockSpec((tq, 2 * LANES), lambda b, hp, i: (b * nq + i, U_QC // (2 * LANES) + hp)),
        pl.BlockSpec((seq, 2 * LANES), lambda b, hp, i: (b, U_KC // (2 * LANES) + hp)),
        pl.BlockSpec((seq, LANES), lambda b, hp, i: (b, U_VC // LANES + hp)),
        pl.BlockSpec((tq, LANES), lambda b, hp, i: (b * nq + i, U_SCZ // LANES + hp)),
    ]
    args = [u, u, u, u]
    if has_ctx:
        past = kc_ctx.shape[2]
        in_specs += [pl.BlockSpec((None, None, past, 2 * LANES), lambda b, hp, i: (b, layer, 0, hp)),
                     pl.BlockSpec((None, None, past, LANES), lambda b, hp, i: (b, layer, 0, hp))]
        args += [kc_ctx, vc_ctx]
    return pl.pallas_call(
        functools.partial(_attn_c_kernel, has_ctx=has_ctx),
        grid=(nb, 2, nq),
        in_specs=in_specs,
        out_specs=pl.BlockSpec((tq, LANES), lambda b, hp, i: (b * nq + i, hp)),
        out_shape=jax.ShapeDtypeStruct((nb * seq, GROUP_W), BF16),
        compiler_params=_params(3),
        name=name,
    )(*args)


def _attn_d_kernel(*refs, local):
    it = iter(refs)
    q_ref, k_ref, v_ref, sz_ref = (next(it) for _ in range(4))
    if local:
        kc_ref, vc_ref, bias_ref = next(it), next(it), next(it)
    o_ref = next(it)
    q = q_ref[...]
    if local:
        i = pl.program_id(2)
        row0 = jnp.clip(Q_ROWS * i - NA_WIN_H // 2, 0, k_ref.shape[0] // GRID_W - LOCAL_ROWS)
        start = pl.multiple_of(row0 * GRID_W, TQ)
        keys = [k_ref[pl.ds(start, LOCAL_ROWS * GRID_W), :], kc_ref[...].astype(BF16)]
        vals = [v_ref[pl.ds(start, LOCAL_ROWS * GRID_W), :], vc_ref[...].astype(BF16)]
    else:
        keys, vals = [k_ref[...]], [v_ref[...]]
    heads = []
    for e in range(2):
        qm = jnp.where(_lane_mask(HEAD_DIM * e, HEAD_DIM), q, jnp.zeros_like(q))
        scores = [_dot_nt(qm, k) for k in keys]
        if local:
            scores[0] = scores[0] + bias_ref[e]
        heads.append(_softmax_pv(scores, vals))
    o = jnp.where(_lane_mask(0, HEAD_DIM), heads[0], heads[1])
    o_ref[...] = (o * sz_ref[...].astype(F32)).astype(BF16)


def _attn_d_call(u, cache_k, cache_v, bias, *, layer, nb, seq, tq, name):
    local = cache_k is not None
    nq = seq // tq
    cb = lambda off: off // LANES
    in_specs = [
        pl.BlockSpec((tq, LANES), lambda b, hp, i: (b * nq + i, cb(U_QD) + hp)),
        pl.BlockSpec((seq, LANES), lambda b, hp, i: (b, cb(U_KD) + hp)),
        pl.BlockSpec((seq, LANES), lambda b, hp, i: (b, cb(U_VD) + hp)),
        pl.BlockSpec((tq, LANES), lambda b, hp, i: (b * nq + i, cb(U_SDZ) + hp)),
    ]
    args = [u, u, u, u]
    if local:
        past = cache_k.shape[2]
        spec = pl.BlockSpec((None, None, past, LANES), lambda b, hp, i: (b, layer, 0, hp))

        def bias_map(b, hp, i):
            pattern = jnp.where(i == 0, 0, jnp.where(i == nq - 1, 2, 1))
            return (pattern, hp, 0, 0)

        in_specs += [spec, spec,
                     pl.BlockSpec((None, 2, tq, LOCAL_ROWS * GRID_W), bias_map)]
        args += [cache_k, cache_v, bias]
    return pl.pallas_call(
        functools.partial(_attn_d_kernel, local=local),
        grid=(nb, 2, nq),
        in_specs=in_specs,
        out_specs=pl.BlockSpec((tq, LANES), lambda b, hp, i: (b * nq + i, hp)),
        out_shape=jax.ShapeDtypeStruct((nb * seq, GROUP_W), BF16),
        compiler_params=_params(3),
        name=name,
    )(*args)


def _local_bias_tables(rpb, rows):
    nq = rows // Q_ROWS
    n_dy, n_dx = 2 * NA_WIN_H - 1, 2 * NA_WIN_W - 1
    qc = np.arange(GRID_W)[:, None]
    kc = np.arange(GRID_W)[None, :]
    ws = np.clip(qc - NA_WIN_W // 2, 0, GRID_W - NA_WIN_W)
    col_ok = (kc >= ws) & (kc < ws + NA_WIN_W)
    oh_dx = ((kc - qc + NA_WIN_W - 1)[None] == np.arange(n_dx)[:, None, None]) & col_ok[None]
    row_ok, oh_dy = [], []
    for i in (0, 1, nq - 1):
        row0 = int(np.clip(Q_ROWS * i - NA_WIN_H // 2, 0, rows - LOCAL_ROWS))
        r = (Q_ROWS * i + np.arange(Q_ROWS))[:, None]
        kr = (row0 + np.arange(LOCAL_ROWS))[None, :]
        rs = np.clip(r - NA_WIN_H // 2, 0, rows - NA_WIN_H)
        ok = (kr >= rs) & (kr < rs + NA_WIN_H)
        row_ok.append(ok)
        oh_dy.append(((kr - r + NA_WIN_H - 1)[..., None] == np.arange(n_dy)) & ok[..., None])
    row_ok, oh_dy = np.stack(row_ok), np.stack(oh_dy)
    hi = lax.Precision.HIGHEST
    cols = jnp.einsum("lhyd,dqc->lhyqc", rpb.astype(F32), jnp.asarray(oh_dx, F32), precision=hi)
    bias = jnp.einsum("pjky,lhyqc->lphjqkc", jnp.asarray(oh_dy, F32), cols, precision=hi)
    valid = row_ok[:, :, None, :, None] & col_ok[None, None, :, None, :]
    bias = jnp.where(valid[None, :, None], bias, NEG_INF)
    return bias.reshape(DEPTH, 3, N_HEADS_GRP, TQ, LOCAL_ROWS * GRID_W)


def _out_kernel(x_ref, mod_ref, ya_ref, bb_ref, g_ref, gp_ref, gn_ref, sbz_ref, yc_ref, yd_ref,
                cw_ref, w_ref, fg_ref, o_ref, *, nblk, final):
    tm = x_ref.shape[0]
    pos = pl.program_id(0) % nblk
    has_prev = jnp.where(pos != 0, 1.0, 0.0)
    has_next = jnp.where(pos != nblk - 1, 1.0, 0.0)
    g = g_ref[...].astype(F32)
    rows = lax.broadcasted_iota(jnp.int32, (tm, 1), 0)
    halo = gp_ref.shape[0]
    g_prev = jnp.where(rows == 0, gp_ref[halo - 1:halo, :].astype(F32) * has_prev,
                       pltpu.roll(g, 1, 0))
    g_next = jnp.where(rows == tm - 1, gn_ref[0:1, :].astype(F32) * has_next,
                       pltpu.roll(g, tm - 1, 0))
    cw = cw_ref[...]
    conv = g_prev * cw[0:1] + g * cw[1:2] + g_next * cw[2:3]
    yb = (bb_ref[...].astype(F32) * conv * sbz_ref[...].astype(F32)).astype(BF16)
    proj = (_dot(ya_ref[...], w_ref[0:GROUP_W, :]) + _dot(yb, w_ref[GROUP_W:2 * GROUP_W, :])
            + _dot(yc_ref[...], w_ref[2 * GROUP_W:3 * GROUP_W, :])
            + _dot(yd_ref[...], w_ref[3 * GROUP_W:, :]))
    xn = x_ref[...] + mod_ref[2:3, :] * proj
    if final:
        xn = _rms(xn, fg_ref[...])
    o_ref[...] = xn


def _out_call(x, mod, u, ya, yc, yd, conv_w, w_out, final_g, *, seq, tm, mod_row0, final, name):
    t = x.shape[0]
    nblk = seq // tm
    halo = 16
    hb = tm // halo
    last = t // halo - 1
    cb = lambda off: off // GROUP_W
    row = lambda i: (i, 0)
    const2 = lambda i: (0, 0)
    in_specs = [
        pl.BlockSpec((tm, D_MODEL), row),
        pl.BlockSpec((None, 3, D_MODEL), _mod_map(mod_row0, nblk)),
        pl.BlockSpec((tm, GROUP_W), row),
        pl.BlockSpec((tm, GROUP_W), lambda i: (i, cb(U_BB))),
        pl.BlockSpec((tm, GROUP_W), lambda i: (i, cb(U_G))),
        pl.BlockSpec((halo, GROUP_W), lambda i: (jnp.maximum(i * hb - 1, 0), cb(U_G))),
        pl.BlockSpec((halo, GROUP_W), lambda i: (jnp.minimum((i + 1) * hb, last), cb(U_G))),
        pl.BlockSpec((tm, GROUP_W), lambda i: (i, cb(U_SBZ))),
        pl.BlockSpec((tm, GROUP_W), row),
        pl.BlockSpec((tm, GROUP_W), row),
        pl.BlockSpec((3, GROUP_W), const2),
        pl.BlockSpec((D_MODEL, D_MODEL), const2),
        pl.BlockSpec((1, D_MODEL), const2),
    ]
    return pl.pallas_call(
        functools.partial(_out_kernel, nblk=nblk, final=final),
        grid=(t // tm,),
        in_specs=in_specs,
        out_specs=pl.BlockSpec((tm, D_MODEL), row),
        out_shape=jax.ShapeDtypeStruct((t, D_MODEL), F32),
        compiler_params=_params(1),
        name=name,
    )(x, mod, ya, u, u, u, u, u, yc, yd, conv_w, w_out, final_g)


def _rope_tables(seq):
    t = jnp.arange(seq)
    rows = (t // GRID_W).astype(F32)
    cols = (t % GRID_W).astype(F32)
    half = DA_SUB // 2
    inv = 1.0 / (ROPE_BASE ** (jnp.arange(0, half, 2, dtype=F32) / half))
    ar = rows[:, None] * inv
    ac = cols[:, None] * inv
    ang = jnp.concatenate([ar, ar, ac, ac], axis=-1)
    cos, sin = jnp.cos(ang), jnp.sin(ang)
    first = (np.arange(DA_SUB) % 16 < 8)[None, :]
    sin_neg = jnp.where(first, -sin, 0.0)
    sin_pos = jnp.where(first, 0.0, sin)
    tile = lambda a: jnp.tile(a, (1, LANES // DA_SUB))

    def pad(a, fill):
        return jnp.concatenate([jnp.full((seq, KPE_LANE), fill, F32), a,
                                jnp.full((seq, LANES - KPE_LANE - MLA_ROPE), fill, F32)], axis=1)

    return ([tile(cos), tile(sin_neg), tile(sin_pos)],
            [pad(cos, 1.0), pad(sin_neg, 0.0), pad(sin_pos, 0.0)])


def _pad_heads(w, width, take):
    d, k, _ = w.shape
    w = w.reshape(d, k, N_HEADS_GRP, width)[..., :take]
    return jnp.pad(w, ((0, 0), (0, 0), (0, 0), (0, LANES - take))).reshape(d, k, N_HEADS_GRP * LANES)


def kernel(x_prompt, x_sample, cache_a_k, cache_a_v, cache_c_kv, cache_c_kpe, cache_d_k, cache_d_v,
           c, c_ctx, ada_w, ada_b, norm_g, w_in, da_lambda, da_subln_g, conv_w,
           mla_q_norm_g, mla_w_uq, mla_kv_norm_g, mla_w_ukv, na_rpb, w_out, final_norm_g):
    batch, seq, _ = x_prompt.shape
    dec_batch, dec_seq, _ = x_sample.shape
    past = cache_a_k.shape[2]
    assert dec_seq % TQ == 0 and dec_seq // GRID_W >= LOCAL_ROWS and seq % 16 == 0

    kpe_lo, kpe_hi = 2432, 2464
    w_in_p = jnp.concatenate(
        [w_in[..., :kpe_lo], jnp.zeros((DEPTH, D_MODEL, KPE_LANE), F32), w_in[..., kpe_lo:kpe_hi],
         jnp.zeros((DEPTH, D_MODEL, LANES - KPE_LANE - MLA_ROPE), F32), w_in[..., kpe_hi:]],
        axis=-1).astype(BF16)
    w_out_b = w_out.astype(BF16)
    wuq = _pad_heads(mla_w_uq, MLA_NOPE + MLA_ROPE, MLA_NOPE + MLA_ROPE).astype(BF16)
    wuk = _pad_heads(mla_w_ukv, MLA_NOPE + MLA_V, MLA_NOPE).astype(BF16)
    wuv = mla_w_ukv.reshape(DEPTH, MLA_KV_RANK, N_HEADS_GRP, MLA_NOPE + MLA_V)[..., MLA_NOPE:]
    wuv = wuv.reshape(DEPTH, MLA_KV_RANK, GROUP_W).astype(BF16)
    subln_g2 = jnp.tile(da_subln_g, (1, LANES // HEAD_DIM)).reshape(DEPTH, 1, LANES)
    tables_a, tables_c = _rope_tables(dec_seq)
    tables = tables_a + tables_c

    cvec8 = jnp.concatenate([c_ctx[None], c, jnp.zeros((8 - 1 - dec_batch, D_MODEL), F32)], axis=0)
    mod = _ada_call(cvec8, ada_w, ada_b).reshape(DEPTH, 8, 3, D_MODEL)

    ca_k = cache_a_k.reshape(dec_batch, DEPTH, past, GROUP_W)
    ca_v = cache_a_v.reshape(dec_batch, DEPTH, past, GROUP_W)
    cd_k = cache_d_k.reshape(dec_batch, DEPTH, past, GROUP_W)
    cd_v = cache_d_v.reshape(dec_batch, DEPTH, past, GROUP_W)
    kpe_pad = jnp.pad(cache_c_kpe, ((0, 0), (0, 0), (0, 0), (KPE_LANE, LANES - KPE_LANE - MLA_ROPE)))
    kc_ctx, vc_ctx = _ctxkv_call(cache_c_kv, kpe_pad, wuk, wuv)
    bias = _local_bias_tables(na_rpb, dec_seq // GRID_W)

    xp = x_prompt.reshape(batch * seq, D_MODEL)
    xs = x_sample.reshape(dec_batch * dec_seq, D_MODEL)
    states = []
    for l in range(DEPTH):
        lam_init = 0.8 - 0.6 * math.exp(-0.3 * l)
        final = l == DEPTH - 1
        ng = norm_g[l].reshape(1, D_MODEL)
        qng = mla_q_norm_g[l].reshape(1, MLA_Q_RANK)
        kvng = mla_kv_norm_g[l].reshape(1, MLA_KV_RANK)
        fg = final_norm_g.reshape(1, D_MODEL)

        u, *st = _in_call(xp, mod[l], ng, w_in_p[l], qng, wuq[l], kvng, wuk[l], wuv[l], None,
                          seq=seq, tm=seq, mod_row0=0, states=True, name=f"ctx_in_{l}")
        states.append(st)
        ya = _attn_a_call(u, None, None, da_lambda[l], subln_g2[l], layer=l, lam_init=lam_init,
                          nb=batch, seq=seq, tq=seq, name=f"ctx_attn_a_{l}")
        yc = _attn_c_call(u, None, None, layer=l, nb=batch, seq=seq, tq=seq, name=f"ctx_attn_c_{l}")
        yd = _attn_d_call(u, None, None, None, layer=l, nb=batch, seq=seq, tq=seq,
                          name=f"ctx_attn_d_{l}")
        xp = _out_call(xp, mod[l], u, ya, yc, yd, conv_w[l], w_out_b[l], fg, seq=seq, tm=seq,
                       mod_row0=0, final=final, name=f"ctx_out_{l}")

        (u,) = _in_call(xs, mod[l], ng, w_in_p[l], qng, wuq[l], kvng, wuk[l], wuv[l], tables,
                        seq=dec_seq, tm=512, mod_row0=1, states=False, name=f"lat_in_{l}")
        ya = _attn_a_call(u, ca_k, ca_v, da_lambda[l], subln_g2[l], layer=l, lam_init=lam_init,
                          nb=dec_batch, seq=dec_seq, tq=TQ, name=f"lat_attn_a_{l}")
        yc = _attn_c_call(u, kc_ctx, vc_ctx, layer=l, nb=dec_batch, seq=dec_seq, tq=TQ,
                          name=f"lat_attn_c_{l}")
        yd = _attn_d_call(u, cd_k, cd_v, bias[l], layer=l, nb=dec_batch, seq=dec_seq, tq=TQ,
                          name=f"lat_attn_d_{l}")
        xs = _out_call(xs, mod[l], u, ya, yc, yd, conv_w[l], w_out_b[l], fg, seq=dec_seq, tm=512,
                       mod_row0=1, final=final, name=f"lat_out_{l}")

    def stack(idx, shape):
        return jnp.stack([s[idx] for s in states], axis=0).reshape(
            (DEPTH, batch, seq) + shape).swapaxes(0, 1)

    state_a_k = stack(0, (N_HEADS_GRP, HEAD_DIM))
    state_a_v = stack(1, (N_HEADS_GRP, HEAD_DIM))
    state_c_kv = stack(2, (MLA_KV_RANK,))
    state_c_kpe = stack(3, (LANES,))[..., KPE_LANE:KPE_LANE + MLA_ROPE]
    state_d_k = stack(4, (N_HEADS_GRP, HEAD_DIM))
    state_d_v = stack(5, (N_HEADS_GRP, HEAD_DIM))
    return (xp.reshape(batch, seq, D_MODEL), xs.reshape(dec_batch, dec_seq, D_MODEL),
            state_a_k, state_a_v, state_c_kv, state_c_kpe, state_d_k, state_d_v)
```

```python
import functools
import math

import jax
import jax.numpy as jnp
import numpy as np
from jax import lax
from jax.experimental import pallas as pl
from jax.experimental.pallas import tpu as pltpu

F32 = jnp.float32
BF16 = jnp.bfloat16

D_MODEL = 1024
DEPTH = 4
GRID_W = 64
HEAD_DIM = 64
GROUP_W = 256
N_HEADS_GRP = 4
DA_SUB = 32
MLA_Q_RANK = 256
MLA_KV_RANK = 128
MLA_NOPE = 64
MLA_ROPE = 32
MLA_V = 64
LOG2E = math.log2(math.e)
MLA_SCALE = (MLA_NOPE + MLA_ROPE) ** -0.5
DA_SCALE = DA_SUB ** -0.5
NA_SCALE = HEAD_DIM ** -0.5
NA_WIN_H = 8
NA_WIN_W = 16
ROPE_BASE = 10000.0
EPS = 1e-6
NEG_INF = -1e30

LANES = 128
VMEM_LIMIT = 56 * 1024 * 1024

W_AQ, W_AK, W_AV, W_AZ = 0, 256, 512, 768
W_BB, W_BC, W_BH, W_BZ = 1024, 1280, 1536, 1792
W_CQ, W_CKV, W_KPE, W_CZ = 2048, 2304, 2432, 2560
W_DQ, W_DK, W_DV, W_DZ = 2816, 3072, 3328, 3584
D_IN_P = 3840
KPE_LANE = 64

U_QA, U_KA, U_VA, U_SZA = 0, 256, 512, 768
U_BB, U_G, U_SBZ = 1024, 1280, 1536
U_QD, U_KD, U_VD, U_SDZ = 1792, 2048, 2304, 2560
U_VC, U_SCZ = 2816, 3072
U_QC, U_KC = 3328, 3840
U_W = 4352

Q_ROWS = 4
TQ = Q_ROWS * GRID_W
LOCAL_CHUNKS = 3
LOCAL_ROWS = LOCAL_CHUNKS * Q_ROWS


def _dot(a, b):
    return jnp.dot(a, b, preferred_element_type=F32)


def _dot_nt(a, b):
    return lax.dot_general(a, b, (((1,), (1,)), ((), ())), preferred_element_type=F32)


def _silu(z):
    return z * (1.0 / (1.0 + jnp.exp(-z)))


def _rms(x, g):
    return x * lax.rsqrt(jnp.mean(x * x, axis=-1, keepdims=True) + EPS) * g


def _rope128(x, cos, sin_neg, sin_pos):
    return x * cos + pltpu.roll(x, LANES - 8, 1) * sin_neg + pltpu.roll(x, 8, 1) * sin_pos


def _scores(job):
    q, keys, biases, _ = job
    out = []
    for k, b in zip(keys, biases):
        s = _dot_nt(q, k)
        out.append(s if b is None else s + b)
    return out


def _softmax_pv(scores, values):
    m = functools.reduce(jnp.maximum, [jnp.max(s, axis=-1, keepdims=True) for s in scores])
    ps = [jnp.exp2(s - m) for s in scores]
    l = functools.reduce(jnp.add, [jnp.sum(p, axis=-1, keepdims=True) for p in ps])
    o = functools.reduce(jnp.add, [_dot(p.astype(BF16), v) for p, v in zip(ps, values)])
    return o / l


def _attend_all(jobs):
    outs = []
    nxt = _scores(jobs[0])
    for i, job in enumerate(jobs):
        cur = nxt
        if i + 1 < len(jobs):
            nxt = _scores(jobs[i + 1])
        outs.append(_softmax_pv(cur, job[3]))
    return outs


def _lane_mask(lo, width):
    lane = lax.broadcasted_iota(jnp.int32, (1, LANES), 1)
    return jnp.logical_and(lane >= lo, lane < lo + width)


def _params(n_axes):
    return pltpu.CompilerParams(dimension_semantics=("arbitrary",) * n_axes,
                                vmem_limit_bytes=VMEM_LIMIT)


def _mod_map(layer, mod_row0, nblk):
    if mod_row0 == 0:
        return lambda i: (layer, 0, 0, 0)
    return lambda i: (layer, mod_row0 + i // nblk, 0, 0)


def _layer_spec(shape, layer):
    zeros = (0,) * len(shape)
    return pl.BlockSpec((None,) + tuple(shape), lambda *_: (layer,) + zeros)


def _ada_kernel(c_ref, w_ref, b_ref, o_ref):
    s = _silu(c_ref[...]).astype(BF16)
    o_ref[...] = _dot(s, w_ref[...].astype(BF16)) + b_ref[...]


def _ada_call(cvec8, ada_w, ada_b):
    nj = 3
    return pl.pallas_call(
        _ada_kernel,
        grid=(DEPTH, nj),
        in_specs=[pl.BlockSpec((8, D_MODEL), lambda l, j: (0, 0)),
                  pl.BlockSpec((None, D_MODEL, D_MODEL), lambda l, j: (l, 0, j)),
                  pl.BlockSpec((None, None, 1, D_MODEL), lambda l, j: (l, j, 0, 0))],
        out_specs=pl.BlockSpec((None, 8, D_MODEL), lambda l, j: (l, 0, j)),
        out_shape=jax.ShapeDtypeStruct((DEPTH, 8, 3 * D_MODEL), F32),
        compiler_params=_params(2),
        name="adaln",
    )(cvec8, ada_w, ada_b.reshape(DEPTH, nj, 1, D_MODEL))


def _cache_kernel(ak_ref, av_ref, ckv_ref, kpe_ref, dk_ref, dv_ref, wuk_ref, wuv_ref,
                  ka_ref, va_ref, kc_ref, vc_ref, kd_ref, vd_ref):
    ka_ref[...] = ak_ref[...].astype(BF16)
    va_ref[...] = av_ref[...].astype(BF16)
    kd_ref[...] = dk_ref[...].astype(BF16)
    vd_ref[...] = dv_ref[...].astype(BF16)
    ckv = ckv_ref[...].astype(BF16)
    kn = _dot(ckv, wuk_ref[...])
    kpe = kpe_ref[...]
    for hd in range(N_HEADS_GRP):
        sl = slice(LANES * hd, LANES * (hd + 1))
        kc_ref[:, sl] = (kn[:, sl] + kpe).astype(BF16)
    vc_ref[...] = _dot(ckv, wuv_ref[...]).astype(BF16)


def _cache_call(ca_k, ca_v, c_kv, kpe_pad, cd_k, cd_v, wuk, wuv):
    nb, _, past, _ = c_kv.shape
    cache = lambda w: pl.BlockSpec((None, None, past, w), lambda l, b: (b, l, 0, 0))
    widths = (GROUP_W, GROUP_W, 4 * LANES, GROUP_W, GROUP_W, GROUP_W)
    return pl.pallas_call(
        _cache_kernel,
        grid=(DEPTH, nb),
        in_specs=[cache(GROUP_W), cache(GROUP_W), cache(MLA_KV_RANK), cache(LANES),
                  cache(GROUP_W), cache(GROUP_W),
                  pl.BlockSpec((None, MLA_KV_RANK, 4 * LANES), lambda l, b: (l, 0, 0)),
                  pl.BlockSpec((None, MLA_KV_RANK, GROUP_W), lambda l, b: (l, 0, 0))],
        out_specs=[cache(w) for w in widths],
        out_shape=[jax.ShapeDtypeStruct((nb, DEPTH, past, w), BF16) for w in widths],
        compiler_params=_params(2),
        name="cache_prep",
    )(ca_k, ca_v, c_kv, kpe_pad, cd_k, cd_v, wuk, wuv)


def _in_kernel(*refs, rope, states):
    it = iter(refs)
    x_ref, mod_ref, ng_ref, w_ref, qng_ref, wuq_ref, kvng_ref, wuk_ref, wuv_ref = (
        next(it) for _ in range(9))
    if rope:
        ta = [next(it)[...] for _ in range(3)]
        tc = [next(it)[...] for _ in range(3)]
    u_ref = next(it)
    if states:
        sak_ref, sav_ref, sckv_ref, skpe_ref, sdk_ref, sdv_ref = (next(it) for _ in range(6))

    x = x_ref[...]
    h = _rms(x, ng_ref[...]) * (1.0 + mod_ref[1:2, :]) + mod_ref[0:1, :]
    hb = h.astype(BF16)

    def seg(off, n=GROUP_W):
        return _dot(hb, w_ref[:, off:off + n])

    def put(off, val):
        u_ref[:, off:off + val.shape[1]] = val.astype(BF16)

    aq, ak, av = seg(W_AQ), seg(W_AK), seg(W_AV)
    if states:
        sak_ref[...] = ak
        sav_ref[...] = av
    for j in range(GROUP_W // LANES):
        sl = slice(LANES * j, LANES * (j + 1))
        qj, kj = aq[:, sl], ak[:, sl]
        if rope:
            qj, kj = _rope128(qj, *ta), _rope128(kj, *ta)
        put(U_QA + LANES * j, qj * (DA_SCALE * LOG2E))
        put(U_KA + LANES * j, kj)
    put(U_VA, av)
    put(U_SZA, _silu(seg(W_AZ)))

    put(U_BB, seg(W_BB))
    put(U_G, seg(W_BC) * seg(W_BH))
    put(U_SBZ, _silu(seg(W_BZ)))

    cqn = _rms(seg(W_CQ), qng_ref[...]).astype(BF16)
    q = _dot(cqn, wuq_ref[...])
    ckvn = _rms(seg(W_CKV, MLA_KV_RANK), kvng_ref[...])
    kpe = seg(W_KPE, LANES)
    if states:
        sckv_ref[...] = ckvn
        skpe_ref[...] = kpe
    ckvb = ckvn.astype(BF16)
    kn = _dot(ckvb, wuk_ref[...])
    kpe_r = _rope128(kpe, *tc) if rope else kpe
    for hd in range(N_HEADS_GRP):
        sl = slice(LANES * hd, LANES * (hd + 1))
        qh = q[:, sl]
        if rope:
            qh = _rope128(qh, *tc)
        put(U_QC + LANES * hd, qh * (MLA_SCALE * LOG2E))
        put(U_KC + LANES * hd, kn[:, sl] + kpe_r)
    put(U_VC, _dot(ckvb, wuv_ref[...]))
    put(U_SCZ, _silu(seg(W_CZ)))

    dk, dv = seg(W_DK), seg(W_DV)
    if states:
        sdk_ref[...] = dk
        sdv_ref[...] = dv
    put(U_QD, seg(W_DQ) * (NA_SCALE * LOG2E))
    put(U_KD, dk)
    put(U_VD, dv)
    put(U_SDZ, _silu(seg(W_DZ)))


def _in_call(x, mod, norm_g, w_in, q_norm_g, wuq, kv_norm_g, wuk, wuv, tables, *, layer, seq, tm,
             mod_row0, states, name):
    t = x.shape[0]
    nblk = seq // tm
    rope = tables is not None
    in_specs = [
        pl.BlockSpec((tm, D_MODEL), lambda i: (i, 0)),
        pl.BlockSpec((None, None, 3, D_MODEL), _mod_map(layer, mod_row0, nblk)),
        _layer_spec((1, D_MODEL), layer),
        _layer_spec((D_MODEL, D_IN_P), layer),
        _layer_spec((1, MLA_Q_RANK), layer),
        _layer_spec((MLA_Q_RANK, 4 * LANES), layer),
        _layer_spec((1, MLA_KV_RANK), layer),
        _layer_spec((MLA_KV_RANK, 4 * LANES), layer),
        _layer_spec((MLA_KV_RANK, GROUP_W), layer),
    ]
    args = [x, mod, norm_g, w_in, q_norm_g, wuq, kv_norm_g, wuk, wuv]
    if rope:
        in_specs += [pl.BlockSpec((tm, LANES), lambda i: (i % nblk, 0))] * 6
        args += list(tables)
    out_specs = [pl.BlockSpec((tm, U_W), lambda i: (i, 0))]
    out_shape = [jax.ShapeDtypeStruct((t, U_W), BF16)]
    if states:
        for w in (GROUP_W, GROUP_W, MLA_KV_RANK, LANES, GROUP_W, GROUP_W):
            out_specs.append(pl.BlockSpec((tm, w), lambda i: (i, 0)))
            out_shape.append(jax.ShapeDtypeStruct((t, w), F32))
    return pl.pallas_call(
        functools.partial(_in_kernel, rope=rope, states=states),
        grid=(t // tm,),
        in_specs=in_specs,
        out_specs=out_specs,
        out_shape=out_shape,
        compiler_params=_params(1),
        name=name,
    )(*args)


def _attn_kernel(*refs, cached, lam_init):
    it = iter(refs)
    nxt = lambda n: [next(it) for _ in range(n)]
    qa_ref, ka_ref, va_ref, sza_ref = nxt(4)
    qc_ref, kc_ref, vc_ref, szc_ref = nxt(4)
    qd_ref, szd_ref = nxt(2)
    n_local = LOCAL_CHUNKS if cached else 1
    kd_refs, vd_refs = nxt(n_local), nxt(n_local)
    if cached:
        kax_ref, vax_ref, kcx_ref, vcx_ref, kdx_ref, vdx_ref, bias_ref = nxt(7)
    lam_ref, g_ref = nxt(2)
    oa_ref, oc_ref, od_ref = nxt(3)

    extra = lambda ref: [ref[...]] if cached else []
    jobs = []
    qa = qa_ref[...]
    keys, vals = [ka_ref[...]] + extra(kax_ref if cached else None), None
    vals = [va_ref[...]] + extra(vax_ref if cached else None)
    for e in range(2):
        for c in range(2):
            qm = jnp.where(_lane_mask(HEAD_DIM * e + DA_SUB * c, DA_SUB), qa, jnp.zeros_like(qa))
            jobs.append((qm, keys, [None] * len(keys), vals))
    vals = [vc_ref[...]] + extra(vcx_ref if cached else None)
    for e in range(2):
        sl = slice(LANES * e, LANES * (e + 1))
        keys = [kc_ref[:, sl]] + ([kcx_ref[:, sl]] if cached else [])
        jobs.append((qc_ref[:, sl], keys, [None] * len(keys), vals))
    qd = qd_ref[...]
    keys = [r[...] for r in kd_refs] + extra(kdx_ref if cached else None)
    vals = [r[...] for r in vd_refs] + extra(vdx_ref if cached else None)
    for e in range(2):
        qm = jnp.where(_lane_mask(HEAD_DIM * e, HEAD_DIM), qd, jnp.zeros_like(qd))
        biases = [None] * len(keys)
        if cached:
            biases = [bias_ref[e, :, TQ * j:TQ * (j + 1)] for j in range(n_local)] + [None]
        jobs.append((qm, keys, biases, vals))

    outs = _attend_all(jobs)
    first = _lane_mask(0, HEAD_DIM)

    lv = lam_ref[...]
    lam = (jnp.exp(jnp.sum(lv[0:1] * lv[1:2], keepdims=True))
           - jnp.exp(jnp.sum(lv[2:3] * lv[3:4], keepdims=True)) + lam_init)
    o = jnp.where(first, outs[0] - lam * outs[1], outs[2] - lam * outs[3])
    o2 = o * o
    ss0 = jnp.sum(jnp.where(first, o2, 0.0), axis=-1, keepdims=True)
    ss1 = jnp.sum(jnp.where(first, 0.0, o2), axis=-1, keepdims=True)
    ms = jnp.where(first, ss0, ss1) * (1.0 / HEAD_DIM)
    y = o * lax.rsqrt(ms + EPS) * g_ref[...] * (1.0 - lam_init)
    oa_ref[...] = (y * sza_ref[...].astype(F32)).astype(BF16)
    oc_ref[...] = (jnp.where(first, outs[4], outs[5]) * szc_ref[...].astype(F32)).astype(BF16)
    od_ref[...] = (jnp.where(first, outs[6], outs[7]) * szd_ref[...].astype(F32)).astype(BF16)


def _attn_call(u, cache, bias, lam_vecs, subln_g2, *, layer, lam_init, nb, seq, tq, name):
    cached = cache is not None
    nq = seq // tq

    def tok(off, w=LANES):
        return pl.BlockSpec((tq, w), lambda b, hp, i: (b * nq + i, off // w + hp))

    def seq_rows(off, w=LANES):
        return pl.BlockSpec((seq, w), lambda b, hp, i: (b, off // w + hp))

    in_specs = [tok(U_QA), seq_rows(U_KA), seq_rows(U_VA), tok(U_SZA),
                tok(U_QC, 2 * LANES), seq_rows(U_KC, 2 * LANES), seq_rows(U_VC), tok(U_SCZ),
                tok(U_QD), tok(U_SDZ)]
    args = [u] * 10
    if cached:
        ka_x, va_x, kc_x, vc_x, kd_x, vd_x = cache
        past = ka_x.shape[2]
        base = lambda i: jnp.clip(i - 1, 0, nq - LOCAL_CHUNKS)
        for off in (U_KD, U_VD):
            for j in range(LOCAL_CHUNKS):
                in_specs.append(pl.BlockSpec(
                    (tq, LANES),
                    lambda b, hp, i, j=j, off=off: (b * nq + base(i) + j, off // LANES + hp)))
                args.append(u)
        kx = lambda w: pl.BlockSpec((None, None, past, w), lambda b, hp, i: (b, layer, 0, hp))

        def bias_map(b, hp, i):
            pattern = jnp.where(i == 0, 0, jnp.where(i == nq - 1, 2, 1))
            return (layer, pattern, hp, 0, 0)

        in_specs += [kx(LANES), kx(LANES), kx(2 * LANES), kx(LANES), kx(LANES), kx(LANES),
                     pl.BlockSpec((None, None, 2, tq, LOCAL_CHUNKS * tq), bias_map)]
        args += [ka_x, va_x, kc_x, vc_x, kd_x, vd_x, bias]
    else:
        in_specs += [seq_rows(U_KD), seq_rows(U_VD)]
        args += [u, u]
    in_specs += [_layer_spec((4, DA_SUB), layer), _layer_spec((1, LANES), layer)]
    args += [lam_vecs, subln_g2]
    out_spec = pl.BlockSpec((tq, LANES), lambda b, hp, i: (b * nq + i, hp))
    out_shape = jax.ShapeDtypeStruct((nb * seq, GROUP_W), BF16)
    return pl.pallas_call(
        functools.partial(_attn_kernel, cached=cached, lam_init=lam_init),
        grid=(nb, 2, nq),
        in_specs=in_specs,
        out_specs=[out_spec] * 3,
        out_shape=[out_shape] * 3,
        compiler_params=_params(3),
        name=name,
    )(*args)


def _local_bias_tables(rpb, rows):
    nq = rows // Q_ROWS
    n_dy, n_dx = 2 * NA_WIN_H - 1, 2 * NA_WIN_W - 1
    qc = np.arange(GRID_W)[:, None]
    kc = np.arange(GRID_W)[None, :]
    ws = np.clip(qc - NA_WIN_W // 2, 0, GRID_W - NA_WIN_W)
    col_ok = (kc >= ws) & (kc < ws + NA_WIN_W)
    oh_dx = ((kc - qc + NA_WIN_W - 1)[None] == np.arange(n_dx)[:, None, None]) & col_ok[None]
    row_ok, oh_dy = [], []
    for i in (0, 1, nq - 1):
        row0 = Q_ROWS * int(np.clip(i - 1, 0, nq - LOCAL_CHUNKS))
        r = (Q_ROWS * i + np.arange(Q_ROWS))[:, None]
        kr = (row0 + np.arange(LOCAL_ROWS))[None, :]
        rs = np.clip(r - NA_WIN_H // 2, 0, rows - NA_WIN_H)
        ok = (kr >= rs) & (kr < rs + NA_WIN_H)
        row_ok.append(ok)
        oh_dy.append(((kr - r + NA_WIN_H - 1)[..., None] == np.arange(n_dy)) & ok[..., None])
    row_ok, oh_dy = np.stack(row_ok), np.stack(oh_dy)
    hi = lax.Precision.HIGHEST
    cols = jnp.einsum("lhyd,dqc->lhyqc", rpb.astype(F32), jnp.asarray(oh_dx, F32), precision=hi)
    bias = jnp.einsum("pjky,lhyqc->lphjqkc", jnp.asarray(oh_dy, F32), cols, precision=hi)
    valid = row_ok[:, :, None, :, None] & col_ok[None, None, :, None, :]
    bias = jnp.where(valid[None, :, None], bias * LOG2E, NEG_INF)
    return bias.reshape(DEPTH, 3, N_HEADS_GRP, TQ, LOCAL_ROWS * GRID_W)


def _out_kernel(x_ref, mod_ref, ya_ref, bb_ref, g_ref, gp_ref, gn_ref, sbz_ref, yc_ref, yd_ref,
                cw_ref, w_ref, fg_ref, o_ref, *, nblk, final):
    tm = x_ref.shape[0]
    pos = pl.program_id(0) % nblk
    has_prev = jnp.where(pos != 0, 1.0, 0.0)
    has_next = jnp.where(pos != nblk - 1, 1.0, 0.0)
    g = g_ref[...].astype(F32)
    rows = lax.broadcasted_iota(jnp.int32, (tm, 1), 0)
    halo = gp_ref.shape[0]
    g_prev = jnp.where(rows == 0, gp_ref[halo - 1:halo, :].astype(F32) * has_prev,
                       pltpu.roll(g, 1, 0))
    g_next = jnp.where(rows == tm - 1, gn_ref[0:1, :].astype(F32) * has_next,
                       pltpu.roll(g, tm - 1, 0))
    cw = cw_ref[...]
    conv = g_prev * cw[0:1] + g * cw[1:2] + g_next * cw[2:3]
    yb = (bb_ref[...].astype(F32) * conv * sbz_ref[...].astype(F32)).astype(BF16)
    proj = (_dot(ya_ref[...], w_ref[0:GROUP_W, :]) + _dot(yb, w_ref[GROUP_W:2 * GROUP_W, :])
            + _dot(yc_ref[...], w_ref[2 * GROUP_W:3 * GROUP_W, :])
            + _dot(yd_ref[...], w_ref[3 * GROUP_W:, :]))
    xn = x_ref[...] + mod_ref[2:3, :] * proj
    if final:
        xn = _rms(xn, fg_ref[...])
    o_ref[...] = xn


def _out_call(x, mod, u, ya, yc, yd, conv_w, w_out, final_g, *, layer, seq, tm, mod_row0, final,
              name):
    t = x.shape[0]
    nblk = seq // tm
    halo = 16
    hb = tm // halo
    last = t // halo - 1
    cb = lambda off: off // GROUP_W
    row = lambda i: (i, 0)
    in_specs = [
        pl.BlockSpec((tm, D_MODEL), row),
        pl.BlockSpec((None, None, 3, D_MODEL), _mod_map(layer, mod_row0, nblk)),
        pl.BlockSpec((tm, GROUP_W), row),
        pl.BlockSpec((tm, GROUP_W), lambda i: (i, cb(U_BB))),
        pl.BlockSpec((tm, GROUP_W), lambda i: (i, cb(U_G))),
        pl.BlockSpec((halo, GROUP_W), lambda i: (jnp.maximum(i * hb - 1, 0), cb(U_G))),
        pl.BlockSpec((halo, GROUP_W), lambda i: (jnp.minimum((i + 1) * hb, last), cb(U_G))),
        pl.BlockSpec((tm, GROUP_W), lambda i: (i, cb(U_SBZ))),
        pl.BlockSpec((tm, GROUP_W), row),
        pl.BlockSpec((tm, GROUP_W), row),
        _layer_spec((3, GROUP_W), layer),
        _layer_spec((D_MODEL, D_MODEL), layer),
        pl.BlockSpec((1, D_MODEL), lambda i: (0, 0)),
    ]
    return pl.pallas_call(
        functools.partial(_out_kernel, nblk=nblk, final=final),
        grid=(t // tm,),
        in_specs=in_specs,
        out_specs=pl.BlockSpec((tm, D_MODEL), row),
        out_shape=jax.ShapeDtypeStruct((t, D_MODEL), F32),
        compiler_params=_params(1),
        name=name,
    )(x, mod, ya, u, u, u, u, u, yc, yd, conv_w, w_out, final_g)


def _rope_tables(seq):
    t = jnp.arange(seq)
    rows = (t // GRID_W).astype(F32)
    cols = (t % GRID_W).astype(F32)
    half = DA_SUB // 2
    inv = 1.0 / (ROPE_BASE ** (jnp.arange(0, half, 2, dtype=F32) / half))
    ar = rows[:, None] * inv
    ac = cols[:, None] * inv
    ang = jnp.concatenate([ar, ar, ac, ac], axis=-1)
    cos, sin = jnp.cos(ang), jnp.sin(ang)
    first = (np.arange(DA_SUB) % 16 < 8)[None, :]
    sin_neg = jnp.where(first, -sin, 0.0)
    sin_pos = jnp.where(first, 0.0, sin)
    tile = lambda a: jnp.tile(a, (1, LANES // DA_SUB))

    def pad(a, fill):
        return jnp.concatenate([jnp.full((seq, KPE_LANE), fill, F32), a,
                                jnp.full((seq, LANES - KPE_LANE - MLA_ROPE), fill, F32)], axis=1)

    return ([tile(cos), tile(sin_neg), tile(sin_pos)],
            [pad(cos, 1.0), pad(sin_neg, 0.0), pad(sin_pos, 0.0)])


def _pad_heads(w, width, take):
    d, k, _ = w.shape
    w = w.reshape(d, k, N_HEADS_GRP, width)[..., :take]
    return jnp.pad(w, ((0, 0), (0, 0), (0, 0), (0, LANES - take))).reshape(d, k, N_HEADS_GRP * LANES)


def kernel(x_prompt, x_sample, cache_a_k, cache_a_v, cache_c_kv, cache_c_kpe, cache_d_k, cache_d_v,
           c, c_ctx, ada_w, ada_b, norm_g, w_in, da_lambda, da_subln_g, conv_w,
           mla_q_norm_g, mla_w_uq, mla_kv_norm_g, mla_w_ukv, na_rpb, w_out, final_norm_g):
    batch, seq, _ = x_prompt.shape
    dec_batch, dec_seq, _ = x_sample.shape
    past = cache_a_k.shape[2]
    assert dec_seq % TQ == 0 and dec_seq // TQ >= LOCAL_CHUNKS and seq % LANES == 0

    kpe_lo, kpe_hi = 2432, 2464
    w_in_p = jnp.concatenate(
        [w_in[..., :kpe_lo], jnp.zeros((DEPTH, D_MODEL, KPE_LANE), F32), w_in[..., kpe_lo:kpe_hi],
         jnp.zeros((DEPTH, D_MODEL, LANES - KPE_LANE - MLA_ROPE), F32), w_in[..., kpe_hi:]],
        axis=-1).astype(BF16)
    w_out_b = w_out.astype(BF16)
    wuq = _pad_heads(mla_w_uq, MLA_NOPE + MLA_ROPE, MLA_NOPE + MLA_ROPE).astype(BF16)
    wuk = _pad_heads(mla_w_ukv, MLA_NOPE + MLA_V, MLA_NOPE).astype(BF16)
    wuv = mla_w_ukv.reshape(DEPTH, MLA_KV_RANK, N_HEADS_GRP, MLA_NOPE + MLA_V)[..., MLA_NOPE:]
    wuv = wuv.reshape(DEPTH, MLA_KV_RANK, GROUP_W).astype(BF16)
    subln_g2 = jnp.tile(da_subln_g, (1, LANES // HEAD_DIM)).reshape(DEPTH, 1, LANES)
    ng = norm_g.reshape(DEPTH, 1, D_MODEL)
    qng = mla_q_norm_g.reshape(DEPTH, 1, MLA_Q_RANK)
    kvng = mla_kv_norm_g.reshape(DEPTH, 1, MLA_KV_RANK)
    fg = final_norm_g.reshape(1, D_MODEL)
    tables_a, tables_c = _rope_tables(dec_seq)
    tables = tables_a + tables_c

    cvec8 = jnp.concatenate([c_ctx[None], c, jnp.zeros((8 - 1 - dec_batch, D_MODEL), F32)], axis=0)
    mod = _ada_call(cvec8, ada_w, ada_b).reshape(DEPTH, 8, 3, D_MODEL)

    flat = lambda a: a.reshape(dec_batch, DEPTH, past, GROUP_W)
    kpe_pad = jnp.pad(cache_c_kpe, ((0, 0), (0, 0), (0, 0), (KPE_LANE, LANES - KPE_LANE - MLA_ROPE)))
    cache = _cache_call(flat(cache_a_k), flat(cache_a_v), cache_c_kv, kpe_pad,
                        flat(cache_d_k), flat(cache_d_v), wuk, wuv)
    bias = _local_bias_tables(na_rpb, dec_seq // GRID_W)

    xp = x_prompt.reshape(batch * seq, D_MODEL)
    xs = x_sample.reshape(dec_batch * dec_seq, D_MODEL)
    states = []
    for l in range(DEPTH):
        lam_init = 0.8 - 0.6 * math.exp(-0.3 * l)
        final = l == DEPTH - 1

        u, *st = _in_call(xp, mod, ng, w_in_p, qng, wuq, kvng, wuk, wuv, None, layer=l,
                          seq=seq, tm=seq, mod_row0=0, states=True, name=f"ctx_in_{l}")
        states.append(st)
        ya, yc, yd = _attn_call(u, None, None, da_lambda, subln_g2, layer=l,
                                lam_init=lam_init, nb=batch, seq=seq, tq=seq, name=f"ctx_attn_{l}")
        xp = _out_call(xp, mod, u, ya, yc, yd, conv_w, w_out_b, fg, layer=l, seq=seq, tm=seq,
                       mod_row0=0, final=final, name=f"ctx_out_{l}")

        (u,) = _in_call(xs, mod, ng, w_in_p, qng, wuq, kvng, wuk, wuv, tables, layer=l,
                        seq=dec_seq, tm=512, mod_row0=1, states=False, name=f"lat_in_{l}")
        ya, yc, yd = _attn_call(u, cache, bias, da_lambda, subln_g2, layer=l,
                                lam_init=lam_init, nb=dec_batch, seq=dec_seq, tq=TQ,
                                name=f"lat_attn_{l}")
        xs = _out_call(xs, mod, u, ya, yc, yd, conv_w, w_out_b, fg, layer=l, seq=dec_seq, tm=512,
                       mod_row0=1, final=final, name=f"lat_out_{l}")

    def stack(idx, shape):
        return jnp.stack([s[idx] for s in states], axis=0).reshape(
            (DEPTH, batch, seq) + shape).swapaxes(0, 1)

    state_a_k = stack(0, (N_HEADS_GRP, HEAD_DIM))
    state_a_v = stack(1, (N_HEADS_GRP, HEAD_DIM))
    state_c_kv = stack(2, (MLA_KV_RANK,))
    state_c_kpe = stack(3, (LANES,))[..., KPE_LANE:KPE_LANE + MLA_ROPE]
    state_d_k = stack(4, (N_HEADS_GRP, HEAD_DIM))
    state_d_v = stack(5, (N_HEADS_GRP, HEAD_DIM))
    return (xp.reshape(batch, seq, D_MODEL), xs.reshape(dec_batch, dec_seq, D_MODEL),
            state_a_k, state_a_v, state_c_kv, state_c_kpe, state_d_k, state_d_v)
```

```python
import functools
import math

import jax
import jax.numpy as jnp
import numpy as np
from jax import lax
from jax.experimental import pallas as pl
from jax.experimental.pallas import tpu as pltpu

F32 = jnp.float32
BF16 = jnp.bfloat16

D_MODEL = 1024
DEPTH = 4
GRID_W = 64
HEAD_DIM = 64
GROUP_W = 256
N_HEADS_GRP = 4
DA_SUB = 32
MLA_Q_RANK = 256
MLA_KV_RANK = 128
MLA_NOPE = 64
MLA_ROPE = 32
MLA_V = 64
LOG2E = math.log2(math.e)
MLA_SCALE = (MLA_NOPE + MLA_ROPE) ** -0.5
DA_SCALE = DA_SUB ** -0.5
NA_SCALE = HEAD_DIM ** -0.5
NA_WIN_H = 8
NA_WIN_W = 16
ROPE_BASE = 10000.0
EPS = 1e-6
NEG_INF = -1e30

LANES = 128
N_PAIRS = GROUP_W // LANES
VMEM_LIMIT = 56 * 1024 * 1024

W_AQ, W_AK, W_AV, W_AZ = 0, 256, 512, 768
W_BB, W_BC, W_BH, W_BZ = 1024, 1280, 1536, 1792
W_CQ, W_CKV, W_KPE, W_CZ = 2048, 2304, 2432, 2560
W_DQ, W_DK, W_DV, W_DZ = 2816, 3072, 3328, 3584
D_IN_P = 3840
KPE_LANE = 64

T_QA, T_SZA, T_QC, T_SCZ, T_QD, T_SDZ, TOK_W = 0, 128, 256, 512, 640, 768, 896
S_KA, S_VA, S_KC, S_VC, SEQ_W = 0, 128, 256, 512, 640
L_KD, L_VD, LOC_W = 0, 128, 256
X_KA, X_VA, X_KC, X_VC, X_KD, X_VD, CACHE_W = 0, 128, 256, 512, 640, 768, 896
B_BB, B_G, B_SBZ, CONV_W = 0, 256, 512, 768

Q_ROWS = 4
TQ = Q_ROWS * GRID_W
LOCAL_CHUNKS = 3
LOCAL_ROWS = LOCAL_CHUNKS * Q_ROWS


def _dot(a, b):
    return jnp.dot(a, b, preferred_element_type=F32)


def _dot_nt(a, b):
    return lax.dot_general(a, b, (((1,), (1,)), ((), ())), preferred_element_type=F32)


def _silu(z):
    return z * (1.0 / (1.0 + jnp.exp(-z)))


def _rms(x, g):
    return x * lax.rsqrt(jnp.mean(x * x, axis=-1, keepdims=True) + EPS) * g


def _rope128(x, cos, sin_neg, sin_pos):
    return x * cos + pltpu.roll(x, LANES - 8, 1) * sin_neg + pltpu.roll(x, 8, 1) * sin_pos


def _scores(job):
    out = []
    for q, keys, biases in job[0]:
        blocks = []
        for k, b in zip(keys, biases):
            s = _dot_nt(q, k)
            blocks.append(s if b is None else s + b)
        out.append(blocks)
    return out


def _softmax_parts(blocks):
    m = functools.reduce(jnp.maximum, [jnp.max(s, axis=-1, keepdims=True) for s in blocks])
    ps = [jnp.exp2(s - m) for s in blocks]
    l = functools.reduce(jnp.add, [jnp.sum(p, axis=-1, keepdims=True) for p in ps])
    return ps, l


def _weighted_values(job, scores):
    _, values, lam = job
    if len(scores) == 1:
        ps, l = _softmax_parts(scores[0])
        o = functools.reduce(jnp.add, [_dot(p.astype(BF16), v) for p, v in zip(ps, values)])
        return o / l
    (p1, l1), (p2, l2) = _softmax_parts(scores[0]), _softmax_parts(scores[1])
    r1, r2 = 1.0 / l1, lam / l2
    return functools.reduce(
        jnp.add, [_dot((a * r1 - b * r2).astype(BF16), v) for a, b, v in zip(p1, p2, values)])


LOOKAHEAD = 1


def _attend_all(jobs):
    outs = []
    pending = [_scores(job) for job in jobs[:LOOKAHEAD]]
    for i, job in enumerate(jobs):
        if i + LOOKAHEAD < len(jobs):
            pending.append(_scores(jobs[i + LOOKAHEAD]))
        outs.append(_weighted_values(job, pending.pop(0)))
    return outs


def _lane_mask(lo, width):
    lane = lax.broadcasted_iota(jnp.int32, (1, LANES), 1)
    return jnp.logical_and(lane >= lo, lane < lo + width)


def _params(n_axes):
    return pltpu.CompilerParams(dimension_semantics=("arbitrary",) * n_axes,
                                vmem_limit_bytes=VMEM_LIMIT)


def _mod_map(layer, mod_row0, nblk):
    if mod_row0 == 0:
        return lambda i: (layer, 0, 0, 0)
    return lambda i: (layer, mod_row0 + i // nblk, 0, 0)


def _layer_spec(shape, layer):
    zeros = (0,) * len(shape)
    return pl.BlockSpec((None,) + tuple(shape), lambda *_: (layer,) + zeros)


def _ada_kernel(c_ref, w_ref, b_ref, o_ref):
    s = _silu(c_ref[...]).astype(BF16)
    o_ref[...] = _dot(s, w_ref[...].astype(BF16)) + b_ref[...]


def _ada_call(cvec8, ada_w, ada_b):
    nj = 3
    return pl.pallas_call(
        _ada_kernel,
        grid=(DEPTH, nj),
        in_specs=[pl.BlockSpec((8, D_MODEL), lambda l, j: (0, 0)),
                  pl.BlockSpec((None, D_MODEL, D_MODEL), lambda l, j: (l, 0, j)),
                  pl.BlockSpec((None, None, 1, D_MODEL), lambda l, j: (l, j, 0, 0))],
        out_specs=pl.BlockSpec((None, 8, D_MODEL), lambda l, j: (l, 0, j)),
        out_shape=jax.ShapeDtypeStruct((DEPTH, 8, 3 * D_MODEL), F32),
        compiler_params=_params(2),
        name="adaln",
    )(cvec8, ada_w, ada_b.reshape(DEPTH, nj, 1, D_MODEL))


def _cache_kernel(ak_ref, av_ref, ckv_ref, kpe_ref, dk_ref, dv_ref, wuk_ref, wuv_ref, o_ref):
    ckv = ckv_ref[...].astype(BF16)
    kn = _dot(ckv, wuk_ref[...])
    vc = _dot(ckv, wuv_ref[...])
    kpe = kpe_ref[...]
    for hp in range(N_PAIRS):
        half = slice(LANES * hp, LANES * (hp + 1))
        o_ref[hp, :, X_KA:X_KA + LANES] = ak_ref[:, half].astype(BF16)
        o_ref[hp, :, X_VA:X_VA + LANES] = av_ref[:, half].astype(BF16)
        for e in range(2):
            sl = slice(LANES * (2 * hp + e), LANES * (2 * hp + e + 1))
            o_ref[hp, :, X_KC + LANES * e:X_KC + LANES * (e + 1)] = (kn[:, sl] + kpe).astype(BF16)
        o_ref[hp, :, X_VC:X_VC + LANES] = vc[:, half].astype(BF16)
        o_ref[hp, :, X_KD:X_KD + LANES] = dk_ref[:, half].astype(BF16)
        o_ref[hp, :, X_VD:X_VD + LANES] = dv_ref[:, half].astype(BF16)


def _cache_call(ca_k, ca_v, c_kv, kpe_pad, cd_k, cd_v, wuk, wuv):
    nb, _, past, _ = c_kv.shape
    cache = lambda w: pl.BlockSpec((None, None, past, w), lambda l, b: (b, l, 0, 0))
    return pl.pallas_call(
        _cache_kernel,
        grid=(DEPTH, nb),
        in_specs=[cache(GROUP_W), cache(GROUP_W), cache(MLA_KV_RANK), cache(LANES),
                  cache(GROUP_W), cache(GROUP_W),
                  pl.BlockSpec((None, MLA_KV_RANK, 4 * LANES), lambda l, b: (l, 0, 0)),
                  pl.BlockSpec((None, MLA_KV_RANK, GROUP_W), lambda l, b: (l, 0, 0))],
        out_specs=pl.BlockSpec((None, None, N_PAIRS, past, CACHE_W), lambda l, b: (b, l, 0, 0, 0)),
        out_shape=jax.ShapeDtypeStruct((nb, DEPTH, N_PAIRS, past, CACHE_W), BF16),
        compiler_params=_params(2),
        name="cache_prep",
    )(ca_k, ca_v, c_kv, kpe_pad, cd_k, cd_v, wuk, wuv)


def _in_kernel(*refs, rope, states):
    it = iter(refs)
    x_ref, mod_ref, ng_ref, w_ref, qng_ref, wuq_ref, kvng_ref, wuk_ref, wuv_ref = (
        next(it) for _ in range(9))
    if rope:
        ta = [next(it)[...] for _ in range(3)]
        tc = [next(it)[...] for _ in range(3)]
    tok_ref, seq_ref, loc_ref, conv_ref = (next(it) for _ in range(4))
    if states:
        sak_ref, sav_ref, sckv_ref, skpe_ref, sdk_ref, sdv_ref = (next(it) for _ in range(6))

    x = x_ref[...]
    h = _rms(x, ng_ref[...]) * (1.0 + mod_ref[1:2, :]) + mod_ref[0:1, :]
    hb = h.astype(BF16)

    def seg(off, n=GROUP_W):
        return _dot(hb, w_ref[:, off:off + n])

    def put(ref, width, off, val):
        for hp in range(N_PAIRS):
            ref[:, width * hp + off:width * hp + off + LANES] = (
                val[:, LANES * hp:LANES * (hp + 1)].astype(BF16))

    cqn = _rms(seg(W_CQ), qng_ref[...]).astype(BF16)
    ckvn = _rms(seg(W_CKV, MLA_KV_RANK), kvng_ref[...])
    kpe = seg(W_KPE, LANES)
    if states:
        sckv_ref[...] = ckvn
        skpe_ref[...] = kpe
    ckvb = ckvn.astype(BF16)

    aq, ak, av = seg(W_AQ), seg(W_AK), seg(W_AV)
    if states:
        sak_ref[...] = ak
        sav_ref[...] = av
    if rope:
        aq = jnp.concatenate([_rope128(aq[:, :LANES], *ta), _rope128(aq[:, LANES:], *ta)], axis=1)
        ak = jnp.concatenate([_rope128(ak[:, :LANES], *ta), _rope128(ak[:, LANES:], *ta)], axis=1)
    put(tok_ref, TOK_W, T_QA, aq * (DA_SCALE * LOG2E))
    put(seq_ref, SEQ_W, S_KA, ak)
    put(seq_ref, SEQ_W, S_VA, av)
    put(tok_ref, TOK_W, T_SZA, _silu(seg(W_AZ)))

    conv_ref[:, B_BB:B_BB + GROUP_W] = seg(W_BB).astype(BF16)
    conv_ref[:, B_G:B_G + GROUP_W] = (seg(W_BC) * seg(W_BH)).astype(BF16)
    conv_ref[:, B_SBZ:B_SBZ + GROUP_W] = _silu(seg(W_BZ)).astype(BF16)

    dk, dv = seg(W_DK), seg(W_DV)
    if states:
        sdk_ref[...] = dk
        sdv_ref[...] = dv
    put(tok_ref, TOK_W, T_QD, seg(W_DQ) * (NA_SCALE * LOG2E))
    put(loc_ref, LOC_W, L_KD, dk)
    put(loc_ref, LOC_W, L_VD, dv)
    put(tok_ref, TOK_W, T_SDZ, _silu(seg(W_DZ)))
    put(tok_ref, TOK_W, T_SCZ, _silu(seg(W_CZ)))

    q = _dot(cqn, wuq_ref[...])
    kn = _dot(ckvb, wuk_ref[...])
    vc = _dot(ckvb, wuv_ref[...])
    kpe_r = _rope128(kpe, *tc) if rope else kpe
    for hd in range(N_HEADS_GRP):
        sl = slice(LANES * hd, LANES * (hd + 1))
        hp, e = divmod(hd, 2)
        qh = q[:, sl]
        if rope:
            qh = _rope128(qh, *tc)
        o = LANES * e
        tok_ref[:, TOK_W * hp + T_QC + o:TOK_W * hp + T_QC + o + LANES] = (
            qh * (MLA_SCALE * LOG2E)).astype(BF16)
        seq_ref[:, SEQ_W * hp + S_KC + o:SEQ_W * hp + S_KC + o + LANES] = (
            kn[:, sl] + kpe_r).astype(BF16)
    put(seq_ref, SEQ_W, S_VC, vc)


def _in_call(x, mod, norm_g, w_in, q_norm_g, wuq, kv_norm_g, wuk, wuv, tables, *, layer, seq, tm,
             mod_row0, states, name):
    t = x.shape[0]
    nblk = seq // tm
    rope = tables is not None
    in_specs = [
        pl.BlockSpec((tm, D_MODEL), lambda i: (i, 0)),
        pl.BlockSpec((None, None, 3, D_MODEL), _mod_map(layer, mod_row0, nblk)),
        _layer_spec((1, D_MODEL), layer),
        _layer_spec((D_MODEL, D_IN_P), layer),
        _layer_spec((1, MLA_Q_RANK), layer),
        _layer_spec((MLA_Q_RANK, 4 * LANES), layer),
        _layer_spec((1, MLA_KV_RANK), layer),
        _layer_spec((MLA_KV_RANK, 4 * LANES), layer),
        _layer_spec((MLA_KV_RANK, GROUP_W), layer),
    ]
    args = [x, mod, norm_g, w_in, q_norm_g, wuq, kv_norm_g, wuk, wuv]
    if rope:
        in_specs += [pl.BlockSpec((tm, LANES), lambda i: (i % nblk, 0))] * 6
        args += list(tables)
    widths = [N_PAIRS * TOK_W, N_PAIRS * SEQ_W, N_PAIRS * LOC_W, CONV_W]
    dtypes = [BF16] * 4
    if states:
        widths += [GROUP_W, GROUP_W, MLA_KV_RANK, LANES, GROUP_W, GROUP_W]
        dtypes += [F32] * 6
    return pl.pallas_call(
        functools.partial(_in_kernel, rope=rope, states=states),
        grid=(t // tm,),
        in_specs=in_specs,
        out_specs=[pl.BlockSpec((tm, w), lambda i: (i, 0)) for w in widths],
        out_shape=[jax.ShapeDtypeStruct((t, w), d) for w, d in zip(widths, dtypes)],
        compiler_params=_params(1),
        name=name,
    )(*args)


def _attn_kernel(*refs, cached, lam_init):
    it = iter(refs)
    tok_ref, seq_ref = next(it), next(it)
    loc_refs = [next(it) for _ in range(LOCAL_CHUNKS if cached else 1)]
    if cached:
        x_ref, bias_ref = next(it), next(it)
    lam_ref, g_ref = next(it), next(it)
    oa_ref, oc_ref, od_ref = next(it), next(it), next(it)

    def col(ref, off, w=LANES):
        return ref[:, off:off + w]

    lv = lam_ref[...]
    lam = (jnp.exp(jnp.sum(lv[0:1] * lv[1:2], keepdims=True))
           - jnp.exp(jnp.sum(lv[2:3] * lv[3:4], keepdims=True)) + lam_init)

    jobs = []
    qa = col(tok_ref, T_QA)
    keys = [col(seq_ref, S_KA)] + ([col(x_ref, X_KA)] if cached else [])
    vals = [col(seq_ref, S_VA)] + ([col(x_ref, X_VA)] if cached else [])
    nob = [None] * len(keys)
    for e in range(2):
        maps = []
        for c in range(2):
            qm = jnp.where(_lane_mask(HEAD_DIM * e + DA_SUB * c, DA_SUB), qa, jnp.zeros_like(qa))
            maps.append((qm, keys, nob))
        jobs.append((maps, vals, lam))
    vals = [col(seq_ref, S_VC)] + ([col(x_ref, X_VC)] if cached else [])
    for e in range(2):
        keys = [col(seq_ref, S_KC + LANES * e)] + ([col(x_ref, X_KC + LANES * e)] if cached else [])
        jobs.append(([(col(tok_ref, T_QC + LANES * e), keys, nob)], vals, None))
    qd = col(tok_ref, T_QD)
    keys = [col(r, L_KD) for r in loc_refs] + ([col(x_ref, X_KD)] if cached else [])
    vals = [col(r, L_VD) for r in loc_refs] + ([col(x_ref, X_VD)] if cached else [])
    for e in range(2):
        qm = jnp.where(_lane_mask(HEAD_DIM * e, HEAD_DIM), qd, jnp.zeros_like(qd))
        biases = [None] * len(keys)
        if cached:
            biases = [bias_ref[e, :, TQ * j:TQ * (j + 1)] for j in range(len(loc_refs))] + [None]
        jobs.append(([(qm, keys, biases)], vals, None))

    outs = _attend_all(jobs)
    first = _lane_mask(0, HEAD_DIM)

    o = jnp.where(first, outs[0], outs[1])
    o2 = o * o
    ss0 = jnp.sum(jnp.where(first, o2, 0.0), axis=-1, keepdims=True)
    ss1 = jnp.sum(jnp.where(first, 0.0, o2), axis=-1, keepdims=True)
    ms = jnp.where(first, ss0, ss1) * (1.0 / HEAD_DIM)
    y = o * lax.rsqrt(ms + EPS) * g_ref[...] * (1.0 - lam_init)
    oa_ref[...] = (y * col(tok_ref, T_SZA).astype(F32)).astype(BF16)
    oc_ref[...] = (jnp.where(first, outs[2], outs[3]) * col(tok_ref, T_SCZ).astype(F32)).astype(BF16)
    od_ref[...] = (jnp.where(first, outs[4], outs[5]) * col(tok_ref, T_SDZ).astype(F32)).astype(BF16)


def _attn_call(u_tok, u_seq, u_loc, cache, bias, lam_vecs, subln_g2, *, layer, lam_init, nb, seq,
               tq, name):
    cached = cache is not None
    nq = seq // tq
    in_specs = [pl.BlockSpec((tq, TOK_W), lambda b, hp, i: (b * nq + i, hp)),
                pl.BlockSpec((seq, SEQ_W), lambda b, hp, i: (b, hp))]
    args = [u_tok, u_seq]
    if cached:
        past = cache.shape[3]
        base = lambda i: jnp.clip(i - 1, 0, nq - LOCAL_CHUNKS)
        for j in range(LOCAL_CHUNKS):
            in_specs.append(pl.BlockSpec(
                (tq, LOC_W), lambda b, hp, i, j=j: (b * nq + base(i) + j, hp)))
            args.append(u_loc)

        def bias_map(b, hp, i):
            pattern = jnp.where(i == 0, 0, jnp.where(i == nq - 1, 2, 1))
            return (layer, pattern, hp, 0, 0)

        in_specs += [pl.BlockSpec((None, None, None, past, CACHE_W),
                                  lambda b, hp, i: (b, layer, hp, 0, 0)),
                     pl.BlockSpec((None, None, 2, tq, LOCAL_CHUNKS * tq), bias_map)]
        args += [cache, bias]
    else:
        in_specs.append(pl.BlockSpec((seq, LOC_W), lambda b, hp, i: (b, hp)))
        args.append(u_loc)
    in_specs += [_layer_spec((4, DA_SUB), layer), _layer_spec((1, LANES), layer)]
    args += [lam_vecs, subln_g2]
    out_spec = pl.BlockSpec((tq, LANES), lambda b, hp, i: (b * nq + i, hp))
    out_shape = jax.ShapeDtypeStruct((nb * seq, GROUP_W), BF16)
    return pl.pallas_call(
        functools.partial(_attn_kernel, cached=cached, lam_init=lam_init),
        grid=(nb, N_PAIRS, nq),
        in_specs=in_specs,
        out_specs=[out_spec] * 3,
        out_shape=[out_shape] * 3,
        compiler_params=_params(3),
        name=name,
    )(*args)


def _local_bias_tables(rpb, rows):
    nq = rows // Q_ROWS
    n_dy, n_dx = 2 * NA_WIN_H - 1, 2 * NA_WIN_W - 1
    qc = np.arange(GRID_W)[:, None]
    kc = np.arange(GRID_W)[None, :]
    ws = np.clip(qc - NA_WIN_W // 2, 0, GRID_W - NA_WIN_W)
    col_ok = (kc >= ws) & (kc < ws + NA_WIN_W)
    oh_dx = ((kc - qc + NA_WIN_W - 1)[None] == np.arange(n_dx)[:, None, None]) & col_ok[None]
    row_ok, oh_dy = [], []
    for i in (0, 1, nq - 1):
        row0 = Q_ROWS * int(np.clip(i - 1, 0, nq - LOCAL_CHUNKS))
        r = (Q_ROWS * i + np.arange(Q_ROWS))[:, None]
        kr = (row0 + np.arange(LOCAL_ROWS))[None, :]
        rs = np.clip(r - NA_WIN_H // 2, 0, rows - NA_WIN_H)
        ok = (kr >= rs) & (kr < rs + NA_WIN_H)
        row_ok.append(ok)
        oh_dy.append(((kr - r + NA_WIN_H - 1)[..., None] == np.arange(n_dy)) & ok[..., None])
    row_ok, oh_dy = np.stack(row_ok), np.stack(oh_dy)
    hi = lax.Precision.HIGHEST
    cols = jnp.einsum("lhyd,dqc->lhyqc", rpb.astype(F32), jnp.asarray(oh_dx, F32), precision=hi)
    bias = jnp.einsum("pjky,lhyqc->lphjqkc", jnp.asarray(oh_dy, F32), cols, precision=hi)
    valid = row_ok[:, :, None, :, None] & col_ok[None, None, :, None, :]
    bias = jnp.where(valid[None, :, None], bias * LOG2E, NEG_INF)
    return bias.reshape(DEPTH, 3, N_HEADS_GRP, TQ, LOCAL_ROWS * GRID_W)


def _out_kernel(x_ref, mod_ref, ya_ref, bb_ref, g_ref, gp_ref, gn_ref, sbz_ref, yc_ref, yd_ref,
                cw_ref, w_ref, fg_ref, o_ref, *, nblk, final):
    tm = x_ref.shape[0]
    pos = pl.program_id(0) % nblk
    has_prev = jnp.where(pos != 0, 1.0, 0.0)
    has_next = jnp.where(pos != nblk - 1, 1.0, 0.0)
    g = g_ref[...].astype(F32)
    rows = lax.broadcasted_iota(jnp.int32, (tm, 1), 0)
    halo = gp_ref.shape[0]
    g_prev = jnp.where(rows == 0, gp_ref[halo - 1:halo, :].astype(F32) * has_prev,
                       pltpu.roll(g, 1, 0))
    g_next = jnp.where(rows == tm - 1, gn_ref[0:1, :].astype(F32) * has_next,
                       pltpu.roll(g, tm - 1, 0))
    cw = cw_ref[...]
    conv = g_prev * cw[0:1] + g * cw[1:2] + g_next * cw[2:3]
    yb = (bb_ref[...].astype(F32) * conv * sbz_ref[...].astype(F32)).astype(BF16)
    proj = (_dot(ya_ref[...], w_ref[0:GROUP_W, :]) + _dot(yc_ref[...], w_ref[2 * GROUP_W:3 * GROUP_W, :])
            + _dot(yd_ref[...], w_ref[3 * GROUP_W:, :]) + _dot(yb, w_ref[GROUP_W:2 * GROUP_W, :]))
    xn = x_ref[...] + mod_ref[2:3, :] * proj
    if final:
        xn = _rms(xn, fg_ref[...])
    o_ref[...] = xn


def _out_call(x, mod, u_conv, ya, yc, yd, conv_w, w_out, final_g, *, layer, seq, tm, mod_row0,
              final, name):
    t = x.shape[0]
    nblk = seq // tm
    halo = 16
    hb = tm // halo
    last = t // halo - 1
    cb = lambda off: off // GROUP_W
    row = lambda i: (i, 0)
    in_specs = [
        pl.BlockSpec((tm, D_MODEL), row),
        pl.BlockSpec((None, None, 3, D_MODEL), _mod_map(layer, mod_row0, nblk)),
        pl.BlockSpec((tm, GROUP_W), row),
        pl.BlockSpec((tm, GROUP_W), lambda i: (i, cb(B_BB))),
        pl.BlockSpec((tm, GROUP_W), lambda i: (i, cb(B_G))),
        pl.BlockSpec((halo, GROUP_W), lambda i: (jnp.maximum(i * hb - 1, 0), cb(B_G))),
        pl.BlockSpec((halo, GROUP_W), lambda i: (jnp.minimum((i + 1) * hb, last), cb(B_G))),
        pl.BlockSpec((tm, GROUP_W), lambda i: (i, cb(B_SBZ))),
        pl.BlockSpec((tm, GROUP_W), row),
        pl.BlockSpec((tm, GROUP_W), row),
        _layer_spec((3, GROUP_W), layer),
        _layer_spec((D_MODEL, D_MODEL), layer),
        pl.BlockSpec((1, D_MODEL), lambda i: (0, 0)),
    ]
    return pl.pallas_call(
        functools.partial(_out_kernel, nblk=nblk, final=final),
        grid=(t // tm,),
        in_specs=in_specs,
        out_specs=pl.BlockSpec((tm, D_MODEL), row),
        out_shape=jax.ShapeDtypeStruct((t, D_MODEL), F32),
        compiler_params=_params(1),
        name=name,
    )(x, mod, ya, u_conv, u_conv, u_conv, u_conv, u_conv, yc, yd, conv_w, w_out, final_g)


def _rope_tables(seq):
    t = jnp.arange(seq)
    rows = (t // GRID_W).astype(F32)
    cols = (t % GRID_W).astype(F32)
    half = DA_SUB // 2
    inv = 1.0 / (ROPE_BASE ** (jnp.arange(0, half, 2, dtype=F32) / half))
    ar = rows[:, None] * inv
    ac = cols[:, None] * inv
    ang = jnp.concatenate([ar, ar, ac, ac], axis=-1)
    cos, sin = jnp.cos(ang), jnp.sin(ang)
    first = (np.arange(DA_SUB) % 16 < 8)[None, :]
    sin_neg = jnp.where(first, -sin, 0.0)
    sin_pos = jnp.where(first, 0.0, sin)
    tile = lambda a: jnp.tile(a, (1, LANES // DA_SUB))

    def pad(a, fill):
        return jnp.concatenate([jnp.full((seq, KPE_LANE), fill, F32), a,
                                jnp.full((seq, LANES - KPE_LANE - MLA_ROPE), fill, F32)], axis=1)

    return ([tile(cos), tile(sin_neg), tile(sin_pos)],
            [pad(cos, 1.0), pad(sin_neg, 0.0), pad(sin_pos, 0.0)])


def _pad_heads(w, width, take):
    d, k, _ = w.shape
    w = w.reshape(d, k, N_HEADS_GRP, width)[..., :take]
    return jnp.pad(w, ((0, 0), (0, 0), (0, 0), (0, LANES - take))).reshape(d, k, N_HEADS_GRP * LANES)


def kernel(x_prompt, x_sample, cache_a_k, cache_a_v, cache_c_kv, cache_c_kpe, cache_d_k, cache_d_v,
           c, c_ctx, ada_w, ada_b, norm_g, w_in, da_lambda, da_subln_g, conv_w,
           mla_q_norm_g, mla_w_uq, mla_kv_norm_g, mla_w_ukv, na_rpb, w_out, final_norm_g):
    batch, seq, _ = x_prompt.shape
    dec_batch, dec_seq, _ = x_sample.shape
    past = cache_a_k.shape[2]
    assert dec_seq % TQ == 0 and dec_seq // TQ >= LOCAL_CHUNKS and seq % LANES == 0

    kpe_lo, kpe_hi = 2432, 2464
    w_in_p = jnp.concatenate(
        [w_in[..., :kpe_lo], jnp.zeros((DEPTH, D_MODEL, KPE_LANE), F32), w_in[..., kpe_lo:kpe_hi],
         jnp.zeros((DEPTH, D_MODEL, LANES - KPE_LANE - MLA_ROPE), F32), w_in[..., kpe_hi:]],
        axis=-1).astype(BF16)
    w_out_b = w_out.astype(BF16)
    wuq = _pad_heads(mla_w_uq, MLA_NOPE + MLA_ROPE, MLA_NOPE + MLA_ROPE).astype(BF16)
    wuk = _pad_heads(mla_w_ukv, MLA_NOPE + MLA_V, MLA_NOPE).astype(BF16)
    wuv = mla_w_ukv.reshape(DEPTH, MLA_KV_RANK, N_HEADS_GRP, MLA_NOPE + MLA_V)[..., MLA_NOPE:]
    wuv = wuv.reshape(DEPTH, MLA_KV_RANK, GROUP_W).astype(BF16)
    subln_g2 = jnp.tile(da_subln_g, (1, LANES // HEAD_DIM)).reshape(DEPTH, 1, LANES)
    ng = norm_g.reshape(DEPTH, 1, D_MODEL)
    qng = mla_q_norm_g.reshape(DEPTH, 1, MLA_Q_RANK)
    kvng = mla_kv_norm_g.reshape(DEPTH, 1, MLA_KV_RANK)
    fg = final_norm_g.reshape(1, D_MODEL)
    tables_a, tables_c = _rope_tables(dec_seq)
    tables = tables_a + tables_c

    cvec8 = jnp.concatenate([c_ctx[None], c, jnp.zeros((8 - 1 - dec_batch, D_MODEL), F32)], axis=0)
    mod = _ada_call(cvec8, ada_w, ada_b).reshape(DEPTH, 8, 3, D_MODEL)

    flat = lambda a: a.reshape(dec_batch, DEPTH, past, GROUP_W)
    kpe_pad = jnp.pad(cache_c_kpe, ((0, 0), (0, 0), (0, 0), (KPE_LANE, LANES - KPE_LANE - MLA_ROPE)))
    cache = _cache_call(flat(cache_a_k), flat(cache_a_v), cache_c_kv, kpe_pad,
                        flat(cache_d_k), flat(cache_d_v), wuk, wuv)
    bias = _local_bias_tables(na_rpb, dec_seq // GRID_W)

    xp = x_prompt.reshape(batch * seq, D_MODEL)
    xs = x_sample.reshape(dec_batch * dec_seq, D_MODEL)
    states = []
    for l in range(DEPTH):
        lam_init = 0.8 - 0.6 * math.exp(-0.3 * l)
        final = l == DEPTH - 1

        tok, sq, loc, cv, *st = _in_call(xp, mod, ng, w_in_p, qng, wuq, kvng, wuk, wuv, None,
                                         layer=l, seq=seq, tm=seq, mod_row0=0, states=True,
                                         name=f"ctx_in_{l}")
        states.append(st)
        ya, yc, yd = _attn_call(tok, sq, loc, None, None, da_lambda, subln_g2, layer=l,
                                lam_init=lam_init, nb=batch, seq=seq, tq=seq, name=f"ctx_attn_{l}")
        xp = _out_call(xp, mod, cv, ya, yc, yd, conv_w, w_out_b, fg, layer=l, seq=seq, tm=seq,
                       mod_row0=0, final=final, name=f"ctx_out_{l}")

        tok, sq, loc, cv = _in_call(xs, mod, ng, w_in_p, qng, wuq, kvng, wuk, wuv, tables,
                                    layer=l, seq=dec_seq, tm=512, mod_row0=1, states=False,
                                    name=f"lat_in_{l}")
        ya, yc, yd = _attn_call(tok, sq, loc, cache, bias, da_lambda, subln_g2, layer=l,
                                lam_init=lam_init, nb=dec_batch, seq=dec_seq, tq=TQ,
                                name=f"lat_attn_{l}")
        xs = _out_call(xs, mod, cv, ya, yc, yd, conv_w, w_out_b, fg, layer=l, seq=dec_seq, tm=512,
                       mod_row0=1, final=final, name=f"lat_out_{l}")

    def stack(idx, shape):
        return jnp.stack([s[idx] for s in states], axis=0).reshape(
            (DEPTH, batch, seq) + shape).swapaxes(0, 1)

    state_a_k = stack(0, (N_HEADS_GRP, HEAD_DIM))
    state_a_v = stack(1, (N_HEADS_GRP, HEAD_DIM))
    state_c_kv = stack(2, (MLA_KV_RANK,))
    state_c_kpe = stack(3, (LANES,))[..., KPE_LANE:KPE_LANE + MLA_ROPE]
    state_d_k = stack(4, (N_HEADS_GRP, HEAD_DIM))
    state_d_v = stack(5, (N_HEADS_GRP, HEAD_DIM))
    return (xp.reshape(batch, seq, D_MODEL), xs.reshape(dec_batch, dec_seq, D_MODEL),
            state_a_k, state_a_v, state_c_kv, state_c_kpe, state_d_k, state_d_v)
```

```python
import functools
import math

import jax
import jax.numpy as jnp
import numpy as np
from jax import lax
from jax.experimental import pallas as pl
from jax.experimental.pallas import tpu as pltpu

F32 = jnp.float32
BF16 = jnp.bfloat16

D_MODEL = 1024
DEPTH = 4
GRID_W = 64
HEAD_DIM = 64
GROUP_W = 256
N_HEADS_GRP = 4
DA_SUB = 32
MLA_Q_RANK = 256
MLA_KV_RANK = 128
MLA_NOPE = 64
MLA_ROPE = 32
MLA_V = 64
LOG2E = math.log2(math.e)
MLA_SCALE = (MLA_NOPE + MLA_ROPE) ** -0.5
DA_SCALE = DA_SUB ** -0.5
NA_SCALE = HEAD_DIM ** -0.5
NA_WIN_H = 8
NA_WIN_W = 16
ROPE_BASE = 10000.0
EPS = 1e-6
NEG_INF = -1e30

LANES = 128
N_PAIRS = GROUP_W // LANES
VMEM_LIMIT = 56 * 1024 * 1024

W_AQ, W_AK, W_AV, W_AZ = 0, 256, 512, 768
W_BB, W_BC, W_BH, W_BZ = 1024, 1280, 1536, 1792
W_CQ, W_CKV, W_KPE, W_CZ = 2048, 2304, 2432, 2560
W_DQ, W_DK, W_DV, W_DZ = 2816, 3072, 3328, 3584
D_IN_P = 3840
KPE_LANE = 64

T_QA, T_SZA, T_QC, T_SCZ, T_QD, T_SDZ, TOK_W = 0, 128, 256, 512, 640, 768, 896
S_KA, S_VA, S_KC, S_VC, SEQ_W = 0, 128, 256, 512, 640
L_KD, L_VD, LOC_W = 0, 128, 256
X_KA, X_VA, X_KC, X_VC, X_KD, X_VD, CACHE_W = 0, 128, 256, 512, 640, 768, 896
B_BB, B_G, B_SBZ, CONV_W = 0, 256, 512, 768

Q_ROWS = 4
TQ = Q_ROWS * GRID_W
LOCAL_CHUNKS = 3
LOCAL_ROWS = LOCAL_CHUNKS * Q_ROWS


def _dot(a, b):
    return jnp.dot(a, b, preferred_element_type=F32)


def _dot_nt(a, b):
    return lax.dot_general(a, b, (((1,), (1,)), ((), ())), preferred_element_type=F32)


def _silu(z):
    return z * (1.0 / (1.0 + jnp.exp(-z)))


def _rms(x, g):
    return x * lax.rsqrt(jnp.mean(x * x, axis=-1, keepdims=True) + EPS) * g


def _rope128(x, cos, sin_neg, sin_pos):
    return x * cos + pltpu.roll(x, LANES - 8, 1) * sin_neg + pltpu.roll(x, 8, 1) * sin_pos


def _scores(job):
    out = []
    for q, keys, biases in job[0]:
        blocks = []
        for k, b in zip(keys, biases):
            s = _dot_nt(q, k)
            blocks.append(s if b is None else s + b)
        out.append(blocks)
    return out


def _softmax_parts(blocks):
    m = functools.reduce(jnp.maximum, [jnp.max(s, axis=-1, keepdims=True) for s in blocks])
    ps = [jnp.exp2(s - m) for s in blocks]
    l = functools.reduce(jnp.add, [jnp.sum(p, axis=-1, keepdims=True) for p in ps])
    return ps, l


def _weighted_values(job, scores):
    _, values, lam = job
    if len(scores) == 1:
        ps, l = _softmax_parts(scores[0])
        o = functools.reduce(jnp.add, [_dot(p.astype(BF16), v) for p, v in zip(ps, values)])
        return o / l
    (p1, l1), (p2, l2) = _softmax_parts(scores[0]), _softmax_parts(scores[1])
    r1, r2 = 1.0 / l1, lam / l2
    return functools.reduce(
        jnp.add, [_dot((a * r1 - b * r2).astype(BF16), v) for a, b, v in zip(p1, p2, values)])


LOOKAHEAD = 1


def _attend_all(jobs):
    outs = []
    pending = [_scores(job) for job in jobs[:LOOKAHEAD]]
    for i, job in enumerate(jobs):
        if i + LOOKAHEAD < len(jobs):
            pending.append(_scores(jobs[i + LOOKAHEAD]))
        outs.append(_weighted_values(job, pending.pop(0)))
    return outs


def _lane_mask(lo, width):
    lane = lax.broadcasted_iota(jnp.int32, (1, LANES), 1)
    return jnp.logical_and(lane >= lo, lane < lo + width)


def _params(n_axes):
    return pltpu.CompilerParams(dimension_semantics=("arbitrary",) * n_axes,
                                vmem_limit_bytes=VMEM_LIMIT)


def _mod_map(layer, mod_row0, nblk):
    if mod_row0 == 0:
        return lambda i: (layer, 0, 0, 0)
    return lambda i: (layer, mod_row0 + i // nblk, 0, 0)


def _layer_spec(shape, layer):
    zeros = (0,) * len(shape)
    return pl.BlockSpec((None,) + tuple(shape), lambda *_: (layer,) + zeros)


def _ada_kernel(c_ref, w_ref, b_ref, o_ref):
    s = _silu(c_ref[...]).astype(BF16)
    o_ref[...] = _dot(s, w_ref[...].astype(BF16)) + b_ref[...]


def _ada_call(cvec8, ada_w, ada_b):
    nj = 3
    return pl.pallas_call(
        _ada_kernel,
        grid=(DEPTH, nj),
        in_specs=[pl.BlockSpec((8, D_MODEL), lambda l, j: (0, 0)),
                  pl.BlockSpec((None, D_MODEL, D_MODEL), lambda l, j: (l, 0, j)),
                  pl.BlockSpec((None, None, 1, D_MODEL), lambda l, j: (l, j, 0, 0))],
        out_specs=pl.BlockSpec((None, 8, D_MODEL), lambda l, j: (l, 0, j)),
        out_shape=jax.ShapeDtypeStruct((DEPTH, 8, 3 * D_MODEL), F32),
        compiler_params=_params(2),
        name="adaln",
    )(cvec8, ada_w, ada_b.reshape(DEPTH, nj, 1, D_MODEL))


def _cache_kernel(ak_ref, av_ref, ckv_ref, kpe_ref, dk_ref, dv_ref, wuk_ref, wuv_ref, o_ref):
    ckv = ckv_ref[...].astype(BF16)
    kn = _dot(ckv, wuk_ref[...])
    vc = _dot(ckv, wuv_ref[...])
    kpe = kpe_ref[...]
    for hp in range(N_PAIRS):
        half = slice(LANES * hp, LANES * (hp + 1))
        o_ref[hp, :, X_KA:X_KA + LANES] = ak_ref[:, half].astype(BF16)
        o_ref[hp, :, X_VA:X_VA + LANES] = av_ref[:, half].astype(BF16)
        for e in range(2):
            sl = slice(LANES * (2 * hp + e), LANES * (2 * hp + e + 1))
            o_ref[hp, :, X_KC + LANES * e:X_KC + LANES * (e + 1)] = (kn[:, sl] + kpe).astype(BF16)
        o_ref[hp, :, X_VC:X_VC + LANES] = vc[:, half].astype(BF16)
        o_ref[hp, :, X_KD:X_KD + LANES] = dk_ref[:, half].astype(BF16)
        o_ref[hp, :, X_VD:X_VD + LANES] = dv_ref[:, half].astype(BF16)


def _cache_call(ca_k, ca_v, c_kv, kpe_pad, cd_k, cd_v, wuk, wuv):
    nb, _, past, _ = c_kv.shape
    cache = lambda w: pl.BlockSpec((None, None, past, w), lambda l, b: (b, l, 0, 0))
    return pl.pallas_call(
        _cache_kernel,
        grid=(DEPTH, nb),
        in_specs=[cache(GROUP_W), cache(GROUP_W), cache(MLA_KV_RANK), cache(LANES),
                  cache(GROUP_W), cache(GROUP_W),
                  pl.BlockSpec((None, MLA_KV_RANK, 4 * LANES), lambda l, b: (l, 0, 0)),
                  pl.BlockSpec((None, MLA_KV_RANK, GROUP_W), lambda l, b: (l, 0, 0))],
        out_specs=pl.BlockSpec((None, None, N_PAIRS, past, CACHE_W), lambda l, b: (b, l, 0, 0, 0)),
        out_shape=jax.ShapeDtypeStruct((nb, DEPTH, N_PAIRS, past, CACHE_W), BF16),
        compiler_params=_params(2),
        name="cache_prep",
    )(ca_k, ca_v, c_kv, kpe_pad, cd_k, cd_v, wuk, wuv)


def _in_kernel(*refs, rope, states):
    it = iter(refs)
    x_ref, mod_ref, ng_ref, w_ref, qng_ref, wuq_ref, kvng_ref, wuk_ref, wuv_ref = (
        next(it) for _ in range(9))
    if rope:
        ta = [next(it)[...] for _ in range(3)]
        tc = [next(it)[...] for _ in range(3)]
    if states == "update":
        for _ in range(6):
            next(it)
    tok_ref, seq_ref, loc_ref, conv_ref = (next(it) for _ in range(4))
    if states:
        sak_ref, sav_ref, sckv_ref, skpe_ref, sdk_ref, sdv_ref = (next(it) for _ in range(6))

    x = x_ref[...]
    h = _rms(x, ng_ref[...]) * (1.0 + mod_ref[1:2, :]) + mod_ref[0:1, :]
    hb = h.astype(BF16)

    def seg(off, n=GROUP_W):
        return _dot(hb, w_ref[:, off:off + n])

    def put(ref, width, off, val):
        for hp in range(N_PAIRS):
            ref[:, width * hp + off:width * hp + off + LANES] = (
                val[:, LANES * hp:LANES * (hp + 1)].astype(BF16))

    cqn = _rms(seg(W_CQ), qng_ref[...]).astype(BF16)
    ckvn = _rms(seg(W_CKV, MLA_KV_RANK), kvng_ref[...])
    kpe = seg(W_KPE, LANES)
    if states:
        sckv_ref[...] = ckvn
        skpe_ref[...] = kpe.T[KPE_LANE:KPE_LANE + MLA_ROPE, :]
    ckvb = ckvn.astype(BF16)

    aq, ak, av = seg(W_AQ), seg(W_AK), seg(W_AV)
    if states:
        sak_ref[...] = ak.T
        sav_ref[...] = av.T
    if rope:
        aq = jnp.concatenate([_rope128(aq[:, :LANES], *ta), _rope128(aq[:, LANES:], *ta)], axis=1)
        ak = jnp.concatenate([_rope128(ak[:, :LANES], *ta), _rope128(ak[:, LANES:], *ta)], axis=1)
    put(tok_ref, TOK_W, T_QA, aq * (DA_SCALE * LOG2E))
    put(seq_ref, SEQ_W, S_KA, ak)
    put(seq_ref, SEQ_W, S_VA, av)
    put(tok_ref, TOK_W, T_SZA, _silu(seg(W_AZ)))

    conv_ref[:, B_BB:B_BB + GROUP_W] = seg(W_BB).astype(BF16)
    conv_ref[:, B_G:B_G + GROUP_W] = (seg(W_BC) * seg(W_BH)).astype(BF16)
    conv_ref[:, B_SBZ:B_SBZ + GROUP_W] = _silu(seg(W_BZ)).astype(BF16)

    dk, dv = seg(W_DK), seg(W_DV)
    if states:
        sdk_ref[...] = dk.T
        sdv_ref[...] = dv.T
    put(tok_ref, TOK_W, T_QD, seg(W_DQ) * (NA_SCALE * LOG2E))
    put(loc_ref, LOC_W, L_KD, dk)
    put(loc_ref, LOC_W, L_VD, dv)
    put(tok_ref, TOK_W, T_SDZ, _silu(seg(W_DZ)))
    put(tok_ref, TOK_W, T_SCZ, _silu(seg(W_CZ)))

    q = _dot(cqn, wuq_ref[...])
    kn = _dot(ckvb, wuk_ref[...])
    vc = _dot(ckvb, wuv_ref[...])
    kpe_r = _rope128(kpe, *tc) if rope else kpe
    for hd in range(N_HEADS_GRP):
        sl = slice(LANES * hd, LANES * (hd + 1))
        hp, e = divmod(hd, 2)
        qh = q[:, sl]
        if rope:
            qh = _rope128(qh, *tc)
        o = LANES * e
        tok_ref[:, TOK_W * hp + T_QC + o:TOK_W * hp + T_QC + o + LANES] = (
            qh * (MLA_SCALE * LOG2E)).astype(BF16)
        seq_ref[:, SEQ_W * hp + S_KC + o:SEQ_W * hp + S_KC + o + LANES] = (
            kn[:, sl] + kpe_r).astype(BF16)
    put(seq_ref, SEQ_W, S_VC, vc)


def _state_layout(batch, seq, tm):
    nblk = seq // tm
    feat = lambda rows: ((batch, DEPTH, rows, seq), (None, None, rows, tm),
                         lambda layer: lambda i: (i // nblk, layer, 0, i % nblk))
    tokm = ((batch, DEPTH, seq, MLA_KV_RANK), (None, None, tm, MLA_KV_RANK),
            lambda layer: lambda i: (i // nblk, layer, i % nblk, 0))
    return [feat(GROUP_W), feat(GROUP_W), tokm, feat(MLA_ROPE), feat(GROUP_W), feat(GROUP_W)]


def _in_call(x, mod, norm_g, w_in, q_norm_g, wuq, kv_norm_g, wuk, wuv, tables, *, layer, seq, tm,
             mod_row0, states, name):
    t = x.shape[0]
    nblk = seq // tm
    rope = tables is not None
    in_specs = [
        pl.BlockSpec((tm, D_MODEL), lambda i: (i, 0)),
        pl.BlockSpec((None, None, 3, D_MODEL), _mod_map(layer, mod_row0, nblk)),
        _layer_spec((1, D_MODEL), layer),
        _layer_spec((D_MODEL, D_IN_P), layer),
        _layer_spec((1, MLA_Q_RANK), layer),
        _layer_spec((MLA_Q_RANK, 4 * LANES), layer),
        _layer_spec((1, MLA_KV_RANK), layer),
        _layer_spec((MLA_KV_RANK, 4 * LANES), layer),
        _layer_spec((MLA_KV_RANK, GROUP_W), layer),
    ]
    args = [x, mod, norm_g, w_in, q_norm_g, wuq, kv_norm_g, wuk, wuv]
    if rope:
        in_specs += [pl.BlockSpec((tm, LANES), lambda i: (i % nblk, 0))] * 6
        args += list(tables)
    widths = [N_PAIRS * TOK_W, N_PAIRS * SEQ_W, N_PAIRS * LOC_W, CONV_W]
    out_specs = [pl.BlockSpec((tm, w), lambda i: (i, 0)) for w in widths]
    out_shape = [jax.ShapeDtypeStruct((t, w), BF16) for w in widths]
    aliases, mode = {}, False
    if states is not None:
        mode = "update" if states else "create"
        for k, (shape, block, index) in enumerate(_state_layout(t // seq, seq, tm)):
            out_specs.append(pl.BlockSpec(block, index(layer)))
            out_shape.append(jax.ShapeDtypeStruct(shape, F32))
            if states:
                aliases[len(args)] = len(widths) + k
                in_specs.append(pl.BlockSpec(memory_space=pl.ANY))
                args.append(states[k])
    return pl.pallas_call(
        functools.partial(_in_kernel, rope=rope, states=mode),
        grid=(t // tm,),
        in_specs=in_specs,
        out_specs=out_specs,
        out_shape=out_shape,
        input_output_aliases=aliases,
        compiler_params=_params(1),
        name=name,
    )(*args)


def _attn_kernel(*refs, cached, lam_init):
    it = iter(refs)
    tok_ref, seq_ref = next(it), next(it)
    loc_refs = [next(it) for _ in range(LOCAL_CHUNKS if cached else 1)]
    if cached:
        x_ref, bias_ref = next(it), next(it)
    lam_ref, g_ref = next(it), next(it)
    oa_ref, oc_ref, od_ref = next(it), next(it), next(it)

    def col(ref, off, w=LANES):
        return ref[:, off:off + w]

    lv = lam_ref[...]
    lam = (jnp.exp(jnp.sum(lv[0:1] * lv[1:2], keepdims=True))
           - jnp.exp(jnp.sum(lv[2:3] * lv[3:4], keepdims=True)) + lam_init)

    jobs = []
    qa = col(tok_ref, T_QA)
    keys = [col(seq_ref, S_KA)] + ([col(x_ref, X_KA)] if cached else [])
    vals = [col(seq_ref, S_VA)] + ([col(x_ref, X_VA)] if cached else [])
    nob = [None] * len(keys)
    for e in range(2):
        maps = []
        for c in range(2):
            qm = jnp.where(_lane_mask(HEAD_DIM * e + DA_SUB * c, DA_SUB), qa, jnp.zeros_like(qa))
            maps.append((qm, keys, nob))
        jobs.append((maps, vals, lam))
    vals = [col(seq_ref, S_VC)] + ([col(x_ref, X_VC)] if cached else [])
    for e in range(2):
        keys = [col(seq_ref, S_KC + LANES * e)] + ([col(x_ref, X_KC + LANES * e)] if cached else [])
        jobs.append(([(col(tok_ref, T_QC + LANES * e), keys, nob)], vals, None))
    qd = col(tok_ref, T_QD)
    keys = [col(r, L_KD) for r in loc_refs] + ([col(x_ref, X_KD)] if cached else [])
    vals = [col(r, L_VD) for r in loc_refs] + ([col(x_ref, X_VD)] if cached else [])
    for e in range(2):
        qm = jnp.where(_lane_mask(HEAD_DIM * e, HEAD_DIM), qd, jnp.zeros_like(qd))
        biases = [None] * len(keys)
        if cached:
            biases = [bias_ref[e, :, TQ * j:TQ * (j + 1)] for j in range(len(loc_refs))] + [None]
        jobs.append(([(qm, keys, biases)], vals, None))

    outs = _attend_all(jobs)
    first = _lane_mask(0, HEAD_DIM)

    o = jnp.where(first, outs[0], outs[1])
    o2 = o * o
    ss0 = jnp.sum(jnp.where(first, o2, 0.0), axis=-1, keepdims=True)
    ss1 = jnp.sum(jnp.where(first, 0.0, o2), axis=-1, keepdims=True)
    ms = jnp.where(first, ss0, ss1) * (1.0 / HEAD_DIM)
    y = o * lax.rsqrt(ms + EPS) * g_ref[...] * (1.0 - lam_init)
    oa_ref[...] = (y * col(tok_ref, T_SZA).astype(F32)).astype(BF16)
    oc_ref[...] = (jnp.where(first, outs[2], outs[3]) * col(tok_ref, T_SCZ).astype(F32)).astype(BF16)
    od_ref[...] = (jnp.where(first, outs[4], outs[5]) * col(tok_ref, T_SDZ).astype(F32)).astype(BF16)


def _attn_call(u_tok, u_seq, u_loc, cache, bias, lam_vecs, subln_g2, *, layer, lam_init, nb, seq,
               tq, name):
    cached = cache is not None
    nq = seq // tq
    in_specs = [pl.BlockSpec((tq, TOK_W), lambda b, hp, i: (b * nq + i, hp)),
                pl.BlockSpec((seq, SEQ_W), lambda b, hp, i: (b, hp))]
    args = [u_tok, u_seq]
    if cached:
        past = cache.shape[3]
        base = lambda i: jnp.clip(i - 1, 0, nq - LOCAL_CHUNKS)
        for j in range(LOCAL_CHUNKS):
            in_specs.append(pl.BlockSpec(
                (tq, LOC_W), lambda b, hp, i, j=j: (b * nq + base(i) + j, hp)))
            args.append(u_loc)

        def bias_map(b, hp, i):
            pattern = jnp.where(i == 0, 0, jnp.where(i == nq - 1, 2, 1))
            return (layer, pattern, hp, 0, 0)

        in_specs += [pl.BlockSpec((None, None, None, past, CACHE_W),
                                  lambda b, hp, i: (b, layer, hp, 0, 0)),
                     pl.BlockSpec((None, None, 2, tq, LOCAL_CHUNKS * tq), bias_map)]
        args += [cache, bias]
    else:
        in_specs.append(pl.BlockSpec((seq, LOC_W), lambda b, hp, i: (b, hp)))
        args.append(u_loc)
    in_specs += [_layer_spec((4, DA_SUB), layer), _layer_spec((1, LANES), layer)]
    args += [lam_vecs, subln_g2]
    out_spec = pl.BlockSpec((tq, LANES), lambda b, hp, i: (b * nq + i, hp))
    out_shape = jax.ShapeDtypeStruct((nb * seq, GROUP_W), BF16)
    return pl.pallas_call(
        functools.partial(_attn_kernel, cached=cached, lam_init=lam_init),
        grid=(nb, N_PAIRS, nq),
        in_specs=in_specs,
        out_specs=[out_spec] * 3,
        out_shape=[out_shape] * 3,
        compiler_params=_params(3),
        name=name,
    )(*args)


def _local_bias_tables(rpb, rows):
    nq = rows // Q_ROWS
    n_dy, n_dx = 2 * NA_WIN_H - 1, 2 * NA_WIN_W - 1
    qc = np.arange(GRID_W)[:, None]
    kc = np.arange(GRID_W)[None, :]
    ws = np.clip(qc - NA_WIN_W // 2, 0, GRID_W - NA_WIN_W)
    col_ok = (kc >= ws) & (kc < ws + NA_WIN_W)
    oh_dx = ((kc - qc + NA_WIN_W - 1)[None] == np.arange(n_dx)[:, None, None]) & col_ok[None]
    tile_idx = []
    for i in (0, 1, nq - 1):
        row0 = Q_ROWS * int(np.clip(i - 1, 0, nq - LOCAL_CHUNKS))
        r = (Q_ROWS * i + np.arange(Q_ROWS))[:, None]
        kr = (row0 + np.arange(LOCAL_ROWS))[None, :]
        rs = np.clip(r - NA_WIN_H // 2, 0, rows - NA_WIN_H)
        ok = (kr >= rs) & (kr < rs + NA_WIN_H)
        tile_idx.append(np.where(ok, kr - r + NA_WIN_H - 1, n_dy))
    tile_idx = np.stack(tile_idx)
    hi = lax.Precision.HIGHEST
    cols = jnp.einsum("lhyd,dqc->lhyqc", rpb.astype(F32), jnp.asarray(oh_dx, F32), precision=hi)
    cols = jnp.where(col_ok, cols * LOG2E, NEG_INF)
    cols = jnp.concatenate(
        [cols, jnp.full((DEPTH, N_HEADS_GRP, 1, GRID_W, GRID_W), NEG_INF, F32)], axis=2)
    cols = jnp.concatenate([cols, cols], axis=-1)

    def build(cols_ref, o_ref):
        left = _lane_mask(0, GRID_W)
        for p in range(3):
            for j in range(Q_ROWS):
                for m in range(LOCAL_ROWS // 2):
                    a, b = int(tile_idx[p, j, 2 * m]), int(tile_idx[p, j, 2 * m + 1])
                    o_ref[p, GRID_W * j:GRID_W * (j + 1), LANES * m:LANES * (m + 1)] = jnp.where(
                        left, cols_ref[a], cols_ref[b])

    return pl.pallas_call(
        build,
        grid=(DEPTH, N_HEADS_GRP),
        in_specs=[pl.BlockSpec((None, None, n_dy + 1, GRID_W, LANES), lambda l, h: (l, h, 0, 0, 0))],
        out_specs=pl.BlockSpec((None, 3, None, TQ, LOCAL_ROWS * GRID_W), lambda l, h: (l, 0, h, 0, 0)),
        out_shape=jax.ShapeDtypeStruct((DEPTH, 3, N_HEADS_GRP, TQ, LOCAL_ROWS * GRID_W), F32),
        compiler_params=_params(2),
        name="local_bias",
    )(cols)


def _out_kernel(x_ref, mod_ref, ya_ref, bb_ref, g_ref, gp_ref, gn_ref, sbz_ref, yc_ref, yd_ref,
                cw_ref, w_ref, fg_ref, o_ref, *, nblk, final):
    tm = x_ref.shape[0]
    pos = pl.program_id(0) % nblk
    has_prev = jnp.where(pos != 0, 1.0, 0.0)
    has_next = jnp.where(pos != nblk - 1, 1.0, 0.0)
    g = g_ref[...].astype(F32)
    rows = lax.broadcasted_iota(jnp.int32, (tm, 1), 0)
    halo = gp_ref.shape[0]
    g_prev = jnp.where(rows == 0, gp_ref[halo - 1:halo, :].astype(F32) * has_prev,
                       pltpu.roll(g, 1, 0))
    g_next = jnp.where(rows == tm - 1, gn_ref[0:1, :].astype(F32) * has_next,
                       pltpu.roll(g, tm - 1, 0))
    cw = cw_ref[...]
    conv = g_prev * cw[0:1] + g * cw[1:2] + g_next * cw[2:3]
    yb = (bb_ref[...].astype(F32) * conv * sbz_ref[...].astype(F32)).astype(BF16)
    proj = (_dot(ya_ref[...], w_ref[0:GROUP_W, :]) + _dot(yc_ref[...], w_ref[2 * GROUP_W:3 * GROUP_W, :])
            + _dot(yd_ref[...], w_ref[3 * GROUP_W:, :]) + _dot(yb, w_ref[GROUP_W:2 * GROUP_W, :]))
    xn = x_ref[...] + mod_ref[2:3, :] * proj
    if final:
        xn = _rms(xn, fg_ref[...])
    o_ref[...] = xn


def _out_call(x, mod, u_conv, ya, yc, yd, conv_w, w_out, final_g, *, layer, seq, tm, mod_row0,
              final, name):
    t = x.shape[0]
    nblk = seq // tm
    halo = 16
    hb = tm // halo
    last = t // halo - 1
    cb = lambda off: off // GROUP_W
    row = lambda i: (i, 0)
    in_specs = [
        pl.BlockSpec((tm, D_MODEL), row),
        pl.BlockSpec((None, None, 3, D_MODEL), _mod_map(layer, mod_row0, nblk)),
        pl.BlockSpec((tm, GROUP_W), row),
        pl.BlockSpec((tm, GROUP_W), lambda i: (i, cb(B_BB))),
        pl.BlockSpec((tm, GROUP_W), lambda i: (i, cb(B_G))),
        pl.BlockSpec((halo, GROUP_W), lambda i: (jnp.maximum(i * hb - 1, 0), cb(B_G))),
        pl.BlockSpec((halo, GROUP_W), lambda i: (jnp.minimum((i + 1) * hb, last), cb(B_G))),
        pl.BlockSpec((tm, GROUP_W), lambda i: (i, cb(B_SBZ))),
        pl.BlockSpec((tm, GROUP_W), row),
        pl.BlockSpec((tm, GROUP_W), row),
        _layer_spec((3, GROUP_W), layer),
        _layer_spec((D_MODEL, D_MODEL), layer),
        pl.BlockSpec((1, D_MODEL), lambda i: (0, 0)),
    ]
    return pl.pallas_call(
        functools.partial(_out_kernel, nblk=nblk, final=final),
        grid=(t // tm,),
        in_specs=in_specs,
        out_specs=pl.BlockSpec((tm, D_MODEL), row),
        out_shape=jax.ShapeDtypeStruct((t, D_MODEL), F32),
        compiler_params=_params(1),
        name=name,
    )(x, mod, ya, u_conv, u_conv, u_conv, u_conv, u_conv, yc, yd, conv_w, w_out, final_g)


def _w_in_prep_kernel(w_ref, o_ref):
    w = w_ref[...]
    kpe_end = W_KPE + MLA_ROPE
    o_ref[:, :W_KPE] = w[:, :W_KPE].astype(BF16)
    pad_l = jnp.zeros((w.shape[0], KPE_LANE), F32)
    pad_r = jnp.zeros((w.shape[0], LANES - KPE_LANE - MLA_ROPE), F32)
    o_ref[:, W_KPE:W_CZ] = jnp.concatenate([pad_l, w[:, W_KPE:kpe_end], pad_r], axis=1).astype(BF16)
    o_ref[:, W_CZ:] = w[:, kpe_end:].astype(BF16)


def _w_in_prep(w_in):
    d_in = w_in.shape[-1]
    rows = 256
    return pl.pallas_call(
        _w_in_prep_kernel,
        grid=(DEPTH, D_MODEL // rows),
        in_specs=[pl.BlockSpec((None, rows, d_in), lambda l, r: (l, r, 0))],
        out_specs=pl.BlockSpec((None, rows, D_IN_P), lambda l, r: (l, r, 0)),
        out_shape=jax.ShapeDtypeStruct((DEPTH, D_MODEL, D_IN_P), BF16),
        compiler_params=_params(2),
        name="w_in_prep",
    )(w_in)


def _rope_tables(seq):
    t = jnp.arange(seq)
    rows = (t // GRID_W).astype(F32)
    cols = (t % GRID_W).astype(F32)
    half = DA_SUB // 2
    inv = 1.0 / (ROPE_BASE ** (jnp.arange(0, half, 2, dtype=F32) / half))
    ar = rows[:, None] * inv
    ac = cols[:, None] * inv
    ang = jnp.concatenate([ar, ar, ac, ac], axis=-1)
    cos, sin = jnp.cos(ang), jnp.sin(ang)
    first = (np.arange(DA_SUB) % 16 < 8)[None, :]
    sin_neg = jnp.where(first, -sin, 0.0)
    sin_pos = jnp.where(first, 0.0, sin)
    tile = lambda a: jnp.tile(a, (1, LANES // DA_SUB))

    def pad(a, fill):
        return jnp.concatenate([jnp.full((seq, KPE_LANE), fill, F32), a,
                                jnp.full((seq, LANES - KPE_LANE - MLA_ROPE), fill, F32)], axis=1)

    return ([tile(cos), tile(sin_neg), tile(sin_pos)],
            [pad(cos, 1.0), pad(sin_neg, 0.0), pad(sin_pos, 0.0)])


def _pad_heads(w, width, take):
    d, k, _ = w.shape
    w = w.reshape(d, k, N_HEADS_GRP, width)[..., :take]
    return jnp.pad(w, ((0, 0), (0, 0), (0, 0), (0, LANES - take))).reshape(d, k, N_HEADS_GRP * LANES)


def kernel(x_prompt, x_sample, cache_a_k, cache_a_v, cache_c_kv, cache_c_kpe, cache_d_k, cache_d_v,
           c, c_ctx, ada_w, ada_b, norm_g, w_in, da_lambda, da_subln_g, conv_w,
           mla_q_norm_g, mla_w_uq, mla_kv_norm_g, mla_w_ukv, na_rpb, w_out, final_norm_g):
    batch, seq, _ = x_prompt.shape
    dec_batch, dec_seq, _ = x_sample.shape
    past = cache_a_k.shape[2]
    assert dec_seq % TQ == 0 and dec_seq // TQ >= LOCAL_CHUNKS and seq % LANES == 0

    w_in_p = _w_in_prep(w_in)
    w_out_b = w_out.astype(BF16)
    wuq = _pad_heads(mla_w_uq, MLA_NOPE + MLA_ROPE, MLA_NOPE + MLA_ROPE).astype(BF16)
    wuk = _pad_heads(mla_w_ukv, MLA_NOPE + MLA_V, MLA_NOPE).astype(BF16)
    wuv = mla_w_ukv.reshape(DEPTH, MLA_KV_RANK, N_HEADS_GRP, MLA_NOPE + MLA_V)[..., MLA_NOPE:]
    wuv = wuv.reshape(DEPTH, MLA_KV_RANK, GROUP_W).astype(BF16)
    subln_g2 = jnp.tile(da_subln_g, (1, LANES // HEAD_DIM)).reshape(DEPTH, 1, LANES)
    ng = norm_g.reshape(DEPTH, 1, D_MODEL)
    qng = mla_q_norm_g.reshape(DEPTH, 1, MLA_Q_RANK)
    kvng = mla_kv_norm_g.reshape(DEPTH, 1, MLA_KV_RANK)
    fg = final_norm_g.reshape(1, D_MODEL)
    tables_a, tables_c = _rope_tables(dec_seq)
    tables = tables_a + tables_c

    cvec8 = jnp.concatenate([c_ctx[None], c, jnp.zeros((8 - 1 - dec_batch, D_MODEL), F32)], axis=0)
    mod = _ada_call(cvec8, ada_w, ada_b).reshape(DEPTH, 8, 3, D_MODEL)

    flat = lambda a: a.reshape(dec_batch, DEPTH, past, GROUP_W)
    kpe_pad = jnp.pad(cache_c_kpe, ((0, 0), (0, 0), (0, 0), (KPE_LANE, LANES - KPE_LANE - MLA_ROPE)))
    cache = _cache_call(flat(cache_a_k), flat(cache_a_v), cache_c_kv, kpe_pad,
                        flat(cache_d_k), flat(cache_d_v), wuk, wuv)
    bias = _local_bias_tables(na_rpb, dec_seq // GRID_W)

    xp = x_prompt.reshape(batch * seq, D_MODEL)
    xs = x_sample.reshape(dec_batch * dec_seq, D_MODEL)
    states = []
    for l in range(DEPTH):
        lam_init = 0.8 - 0.6 * math.exp(-0.3 * l)
        final = l == DEPTH - 1

        tok, sq, loc, cv, *states = _in_call(xp, mod, ng, w_in_p, qng, wuq, kvng, wuk, wuv, None,
                                             layer=l, seq=seq, tm=seq, mod_row0=0, states=states,
                                             name=f"ctx_in_{l}")
        ya, yc, yd = _attn_call(tok, sq, loc, None, None, da_lambda, subln_g2, layer=l,
                                lam_init=lam_init, nb=batch, seq=seq, tq=seq, name=f"ctx_attn_{l}")
        xp = _out_call(xp, mod, cv, ya, yc, yd, conv_w, w_out_b, fg, layer=l, seq=seq, tm=seq,
                       mod_row0=0, final=final, name=f"ctx_out_{l}")

        tok, sq, loc, cv = _in_call(xs, mod, ng, w_in_p, qng, wuq, kvng, wuk, wuv, tables,
                                    layer=l, seq=dec_seq, tm=512, mod_row0=1, states=None,
                                    name=f"lat_in_{l}")
        ya, yc, yd = _attn_call(tok, sq, loc, cache, bias, da_lambda, subln_g2, layer=l,
                                lam_init=lam_init, nb=dec_batch, seq=dec_seq, tq=TQ,
                                name=f"lat_attn_{l}")
        xs = _out_call(xs, mod, cv, ya, yc, yd, conv_w, w_out_b, fg, layer=l, seq=dec_seq, tm=512,
                       mod_row0=1, final=final, name=f"lat_out_{l}")

    def heads(a):
        return a.reshape(batch, DEPTH, N_HEADS_GRP, HEAD_DIM, seq).transpose(0, 1, 4, 2, 3)

    s_ak, s_av, s_ckv, s_kpe, s_dk, s_dv = states
    return (xp.reshape(batch, seq, D_MODEL), xs.reshape(dec_batch, dec_seq, D_MODEL),
            heads(s_ak), heads(s_av), s_ckv, s_kpe.transpose(0, 1, 3, 2), heads(s_dk), heads(s_dv))
```

```python
import functools
import math

import jax
import jax.numpy as jnp
import numpy as np
from jax import lax
from jax.experimental import pallas as pl
from jax.experimental.pallas import tpu as pltpu

F32 = jnp.float32
BF16 = jnp.bfloat16

D_MODEL = 1024
DEPTH = 4
GRID_W = 64
HEAD_DIM = 64
GROUP_W = 256
N_HEADS_GRP = 4
DA_SUB = 32
MLA_Q_RANK = 256
MLA_KV_RANK = 128
MLA_NOPE = 64
MLA_ROPE = 32
MLA_V = 64
LOG2E = math.log2(math.e)
MLA_SCALE = (MLA_NOPE + MLA_ROPE) ** -0.5
DA_SCALE = DA_SUB ** -0.5
NA_SCALE = HEAD_DIM ** -0.5
NA_WIN_H = 8
NA_WIN_W = 16
ROPE_BASE = 10000.0
EPS = 1e-6
NEG_INF = -1e30

LANES = 128
N_PAIRS = GROUP_W // LANES
VMEM_LIMIT = 56 * 1024 * 1024

W_AQ, W_AK, W_AV, W_AZ = 0, 256, 512, 768
W_BB, W_BC, W_BH, W_BZ = 1024, 1280, 1536, 1792
W_CQ, W_CKV, W_KPE, W_CZ = 2048, 2304, 2432, 2560
W_DQ, W_DK, W_DV, W_DZ = 2816, 3072, 3328, 3584
D_IN_P = 3840
KPE_LANE = 64

T_QA, T_SZA, T_QC, T_SCZ, T_QD, T_SDZ, TOK_W = 0, 128, 256, 512, 640, 768, 896
S_KA, S_VA, S_KC, S_VC, SEQ_W = 0, 128, 256, 512, 640
L_KD, L_VD, LOC_W = 0, 128, 256
X_KA, X_VA, X_KC, X_VC, X_KD, X_VD, CACHE_W = 0, 128, 256, 512, 640, 768, 896
B_BB, B_G, B_SBZ, CONV_W = 0, 256, 512, 768

Q_ROWS = 4
TQ = Q_ROWS * GRID_W
LOCAL_CHUNKS = 3
LOCAL_ROWS = LOCAL_CHUNKS * Q_ROWS


def _dot(a, b):
    return jnp.dot(a, b, preferred_element_type=F32)


def _dot_nt(a, b):
    return lax.dot_general(a, b, (((1,), (1,)), ((), ())), preferred_element_type=F32)


def _silu(z):
    return z * (1.0 / (1.0 + jnp.exp(-z)))


def _rms(x, g):
    return x * lax.rsqrt(jnp.mean(x * x, axis=-1, keepdims=True) + EPS) * g


def _rope128(x, cos, sin_neg, sin_pos):
    return x * cos + pltpu.roll(x, LANES - 8, 1) * sin_neg + pltpu.roll(x, 8, 1) * sin_pos


def _scores(job):
    out = []
    for q, keys, biases in job[0]:
        blocks = []
        for k, b in zip(keys, biases):
            s = _dot_nt(q, k)
            blocks.append(s if b is None else s + b)
        out.append(blocks)
    return out


def _softmax_parts(blocks):
    m = functools.reduce(jnp.maximum, [jnp.max(s, axis=-1, keepdims=True) for s in blocks])
    ps = [jnp.exp2(s - m) for s in blocks]
    l = functools.reduce(jnp.add, [jnp.sum(p, axis=-1, keepdims=True) for p in ps])
    return ps, l


def _weighted_values(job, scores):
    _, values, lam, sum_lane = job
    if len(scores) == 1:
        blocks = scores[0]
        m = functools.reduce(jnp.maximum, [jnp.max(s, axis=-1, keepdims=True) for s in blocks])
        one_hot = _lane_mask(sum_lane, 1)
        o = functools.reduce(jnp.add, [
            _dot(jnp.exp2(s - m).astype(BF16), jnp.where(one_hot, jnp.ones_like(v), v))
            for s, v in zip(blocks, values)])
        l = jnp.sum(jnp.where(one_hot, o, 0.0), axis=-1, keepdims=True)
        return o / l
    (p1, l1), (p2, l2) = _softmax_parts(scores[0]), _softmax_parts(scores[1])
    r1, r2 = 1.0 / l1, lam / l2
    return functools.reduce(
        jnp.add, [_dot((a * r1 - b * r2).astype(BF16), v) for a, b, v in zip(p1, p2, values)])


LOOKAHEAD = 1


def _attend_all(jobs):
    outs = []
    pending = [_scores(job) for job in jobs[:LOOKAHEAD]]
    for i, job in enumerate(jobs):
        if i + LOOKAHEAD < len(jobs):
            pending.append(_scores(jobs[i + LOOKAHEAD]))
        outs.append(_weighted_values(job, pending.pop(0)))
    return outs


def _lane_mask(lo, width):
    lane = lax.broadcasted_iota(jnp.int32, (1, LANES), 1)
    return jnp.logical_and(lane >= lo, lane < lo + width)


def _params(n_axes):
    return pltpu.CompilerParams(dimension_semantics=("arbitrary",) * n_axes,
                                vmem_limit_bytes=VMEM_LIMIT)


def _mod_map(layer, mod_row0, nblk):
    if mod_row0 == 0:
        return lambda i: (layer, 0, 0, 0)
    return lambda i: (layer, mod_row0 + i // nblk, 0, 0)


def _layer_spec(shape, layer):
    zeros = (0,) * len(shape)
    return pl.BlockSpec((None,) + tuple(shape), lambda *_: (layer,) + zeros)


def _ada_kernel(c_ref, w_ref, b_ref, o_ref):
    s = _silu(c_ref[...]).astype(BF16)
    o_ref[...] = _dot(s, w_ref[...].astype(BF16)) + b_ref[...]


def _ada_call(cvec8, ada_w, ada_b):
    nj = 3
    return pl.pallas_call(
        _ada_kernel,
        grid=(DEPTH, nj),
        in_specs=[pl.BlockSpec((8, D_MODEL), lambda l, j: (0, 0)),
                  pl.BlockSpec((None, D_MODEL, D_MODEL), lambda l, j: (l, 0, j)),
                  pl.BlockSpec((None, None, 1, D_MODEL), lambda l, j: (l, j, 0, 0))],
        out_specs=pl.BlockSpec((None, 8, D_MODEL), lambda l, j: (l, 0, j)),
        out_shape=jax.ShapeDtypeStruct((DEPTH, 8, 3 * D_MODEL), F32),
        compiler_params=_params(2),
        name="adaln",
    )(cvec8, ada_w, ada_b.reshape(DEPTH, nj, 1, D_MODEL))


def _cache_kernel(ak_ref, av_ref, ckv_ref, kpe_ref, dk_ref, dv_ref, wuk_ref, wuv_ref, o_ref):
    ckv = ckv_ref[...].astype(BF16)
    kn = _dot(ckv, wuk_ref[...])
    vc = _dot(ckv, wuv_ref[...])
    past = ckv.shape[0]
    kpe = jnp.concatenate([jnp.zeros((KPE_LANE, past), F32), kpe_ref[...],
                           jnp.zeros((LANES - KPE_LANE - MLA_ROPE, past), F32)], axis=0).T
    for hp in range(N_PAIRS):
        half = slice(LANES * hp, LANES * (hp + 1))
        o_ref[hp, :, X_KA:X_KA + LANES] = ak_ref[half, :].T.astype(BF16)
        o_ref[hp, :, X_VA:X_VA + LANES] = av_ref[half, :].T.astype(BF16)
        for e in range(2):
            sl = slice(LANES * (2 * hp + e), LANES * (2 * hp + e + 1))
            o_ref[hp, :, X_KC + LANES * e:X_KC + LANES * (e + 1)] = (kn[:, sl] + kpe).astype(BF16)
        o_ref[hp, :, X_VC:X_VC + LANES] = vc[:, half].astype(BF16)
        o_ref[hp, :, X_KD:X_KD + LANES] = dk_ref[half, :].T.astype(BF16)
        o_ref[hp, :, X_VD:X_VD + LANES] = dv_ref[half, :].T.astype(BF16)


def _cache_call(ca_k, ca_v, c_kv, c_kpe, cd_k, cd_v, wuk, wuv):
    nb, _, past, _ = c_kv.shape
    cache = lambda w: pl.BlockSpec((None, None, past, w), lambda l, b: (b, l, 0, 0))
    feat = lambda rows: pl.BlockSpec((None, None, rows, past), lambda l, b: (b, l, 0, 0))
    return pl.pallas_call(
        _cache_kernel,
        grid=(DEPTH, nb),
        in_specs=[feat(GROUP_W), feat(GROUP_W), cache(MLA_KV_RANK), feat(MLA_ROPE),
                  feat(GROUP_W), feat(GROUP_W),
                  pl.BlockSpec((None, MLA_KV_RANK, 4 * LANES), lambda l, b: (l, 0, 0)),
                  pl.BlockSpec((None, MLA_KV_RANK, GROUP_W), lambda l, b: (l, 0, 0))],
        out_specs=pl.BlockSpec((None, None, N_PAIRS, past, CACHE_W), lambda l, b: (b, l, 0, 0, 0)),
        out_shape=jax.ShapeDtypeStruct((nb, DEPTH, N_PAIRS, past, CACHE_W), BF16),
        compiler_params=_params(2),
        name="cache_prep",
    )(ca_k, ca_v, c_kv, c_kpe, cd_k, cd_v, wuk, wuv)


def _in_kernel(*refs, rope, states):
    it = iter(refs)
    x_ref, mod_ref, ng_ref, w_ref, qng_ref, wuq_ref, kvng_ref, wuk_ref, wuv_ref = (
        next(it) for _ in range(9))
    if rope:
        ta = [next(it)[...] for _ in range(3)]
        tc = [next(it)[...] for _ in range(3)]
    if states == "update":
        for _ in range(6):
            next(it)
    tok_ref, seq_ref, loc_ref, conv_ref = (next(it) for _ in range(4))
    if states:
        sak_ref, sav_ref, sckv_ref, skpe_ref, sdk_ref, sdv_ref = (next(it) for _ in range(6))

    x = x_ref[...]
    h = _rms(x, ng_ref[...]) * (1.0 + mod_ref[1:2, :]) + mod_ref[0:1, :]
    hb = h.astype(BF16)

    def seg(off, n=GROUP_W):
        return _dot(hb, w_ref[:, off:off + n])

    def put(ref, width, off, val):
        for hp in range(N_PAIRS):
            ref[:, width * hp + off:width * hp + off + LANES] = (
                val[:, LANES * hp:LANES * (hp + 1)].astype(BF16))

    cqn = _rms(seg(W_CQ), qng_ref[...]).astype(BF16)
    ckvn = _rms(seg(W_CKV, MLA_KV_RANK), kvng_ref[...])
    kpe = seg(W_KPE, LANES)
    if states:
        sckv_ref[...] = ckvn
        skpe_ref[...] = kpe.T[KPE_LANE:KPE_LANE + MLA_ROPE, :]
    ckvb = ckvn.astype(BF16)

    aq, ak, av = seg(W_AQ), seg(W_AK), seg(W_AV)
    if states:
        sak_ref[...] = ak.T
        sav_ref[...] = av.T
    if rope:
        aq = jnp.concatenate([_rope128(aq[:, :LANES], *ta), _rope128(aq[:, LANES:], *ta)], axis=1)
        ak = jnp.concatenate([_rope128(ak[:, :LANES], *ta), _rope128(ak[:, LANES:], *ta)], axis=1)
    put(tok_ref, TOK_W, T_QA, aq * (DA_SCALE * LOG2E))
    put(seq_ref, SEQ_W, S_KA, ak)
    put(seq_ref, SEQ_W, S_VA, av)
    put(tok_ref, TOK_W, T_SZA, _silu(seg(W_AZ)))

    conv_ref[:, B_BB:B_BB + GROUP_W] = seg(W_BB).astype(BF16)
    conv_ref[:, B_G:B_G + GROUP_W] = (seg(W_BC) * seg(W_BH)).astype(BF16)
    conv_ref[:, B_SBZ:B_SBZ + GROUP_W] = _silu(seg(W_BZ)).astype(BF16)

    dk, dv = seg(W_DK), seg(W_DV)
    if states:
        sdk_ref[...] = dk.T
        sdv_ref[...] = dv.T
    put(tok_ref, TOK_W, T_QD, seg(W_DQ) * (NA_SCALE * LOG2E))
    put(loc_ref, LOC_W, L_KD, dk)
    put(loc_ref, LOC_W, L_VD, dv)
    put(tok_ref, TOK_W, T_SDZ, _silu(seg(W_DZ)))
    put(tok_ref, TOK_W, T_SCZ, _silu(seg(W_CZ)))

    q = _dot(cqn, wuq_ref[...])
    kn = _dot(ckvb, wuk_ref[...])
    vc = _dot(ckvb, wuv_ref[...])
    kpe_r = _rope128(kpe, *tc) if rope else kpe
    for hd in range(N_HEADS_GRP):
        sl = slice(LANES * hd, LANES * (hd + 1))
        hp, e = divmod(hd, 2)
        qh = q[:, sl]
        if rope:
            qh = _rope128(qh, *tc)
        o = LANES * e
        tok_ref[:, TOK_W * hp + T_QC + o:TOK_W * hp + T_QC + o + LANES] = (
            qh * (MLA_SCALE * LOG2E)).astype(BF16)
        seq_ref[:, SEQ_W * hp + S_KC + o:SEQ_W * hp + S_KC + o + LANES] = (
            kn[:, sl] + kpe_r).astype(BF16)
    put(seq_ref, SEQ_W, S_VC, vc)


def _state_layout(batch, seq, tm):
    nblk = seq // tm
    feat = lambda rows: ((batch, DEPTH, rows, seq), (None, None, rows, tm),
                         lambda layer: lambda i: (i // nblk, layer, 0, i % nblk))
    tokm = ((batch, DEPTH, seq, MLA_KV_RANK), (None, None, tm, MLA_KV_RANK),
            lambda layer: lambda i: (i // nblk, layer, i % nblk, 0))
    return [feat(GROUP_W), feat(GROUP_W), tokm, feat(MLA_ROPE), feat(GROUP_W), feat(GROUP_W)]


def _in_call(x, mod, norm_g, w_in, q_norm_g, wuq, kv_norm_g, wuk, wuv, tables, *, layer, seq, tm,
             mod_row0, states, name):
    t = x.shape[0]
    nblk = seq // tm
    rope = tables is not None
    in_specs = [
        pl.BlockSpec((tm, D_MODEL), lambda i: (i, 0)),
        pl.BlockSpec((None, None, 3, D_MODEL), _mod_map(layer, mod_row0, nblk)),
        _layer_spec((1, D_MODEL), layer),
        _layer_spec((D_MODEL, D_IN_P), layer),
        _layer_spec((1, MLA_Q_RANK), layer),
        _layer_spec((MLA_Q_RANK, 4 * LANES), layer),
        _layer_spec((1, MLA_KV_RANK), layer),
        _layer_spec((MLA_KV_RANK, 4 * LANES), layer),
        _layer_spec((MLA_KV_RANK, GROUP_W), layer),
    ]
    args = [x, mod, norm_g, w_in, q_norm_g, wuq, kv_norm_g, wuk, wuv]
    if rope:
        in_specs += [pl.BlockSpec((tm, LANES), lambda i: (i % nblk, 0))] * 6
        args += list(tables)
    widths = [N_PAIRS * TOK_W, N_PAIRS * SEQ_W, N_PAIRS * LOC_W, CONV_W]
    out_specs = [pl.BlockSpec((tm, w), lambda i: (i, 0)) for w in widths]
    out_shape = [jax.ShapeDtypeStruct((t, w), BF16) for w in widths]
    aliases, mode = {}, False
    if states is not None:
        mode = "update" if states else "create"
        for k, (shape, block, index) in enumerate(_state_layout(t // seq, seq, tm)):
            out_specs.append(pl.BlockSpec(block, index(layer)))
            out_shape.append(jax.ShapeDtypeStruct(shape, F32))
            if states:
                aliases[len(args)] = len(widths) + k
                in_specs.append(pl.BlockSpec(memory_space=pl.ANY))
                args.append(states[k])
    return pl.pallas_call(
        functools.partial(_in_kernel, rope=rope, states=mode),
        grid=(t // tm,),
        in_specs=in_specs,
        out_specs=out_specs,
        out_shape=out_shape,
        input_output_aliases=aliases,
        compiler_params=_params(1),
        name=name,
    )(*args)


def _attn_kernel(*refs, cached, lam_init):
    it = iter(refs)
    tok_ref, seq_ref = next(it), next(it)
    loc_refs = [next(it) for _ in range(LOCAL_CHUNKS if cached else 1)]
    if cached:
        x_ref, bias_ref = next(it), next(it)
    lam_ref, g_ref = next(it), next(it)
    oa_ref, oc_ref, od_ref = next(it), next(it), next(it)

    lv = lam_ref[...]
    lam = (jnp.exp(jnp.sum(lv[0:1] * lv[1:2], keepdims=True))
           - jnp.exp(jnp.sum(lv[2:3] * lv[3:4], keepdims=True)) + lam_init)

    jobs = []
    for hp in range(N_PAIRS):
        def col(ref, width, off, w=LANES, hp=hp):
            return ref[:, width * hp + off:width * hp + off + w]

        xcol = lambda off, hp=hp: x_ref[hp, :, off:off + LANES]
        qa = col(tok_ref, TOK_W, T_QA)
        keys = [col(seq_ref, SEQ_W, S_KA)] + ([xcol(X_KA)] if cached else [])
        vals = [col(seq_ref, SEQ_W, S_VA)] + ([xcol(X_VA)] if cached else [])
        nob = [None] * len(keys)
        for e in range(2):
            maps = []
            for c in range(2):
                qm = jnp.where(_lane_mask(HEAD_DIM * e + DA_SUB * c, DA_SUB), qa, jnp.zeros_like(qa))
                maps.append((qm, keys, nob))
            jobs.append((maps, vals, lam, None))
        vals = [col(seq_ref, SEQ_W, S_VC)] + ([xcol(X_VC)] if cached else [])
        for e in range(2):
            keys = [col(seq_ref, SEQ_W, S_KC + LANES * e)] + (
                [xcol(X_KC + LANES * e)] if cached else [])
            jobs.append(([(col(tok_ref, TOK_W, T_QC + LANES * e), keys, nob)], vals, None,
                         HEAD_DIM * (1 - e)))
        qd = col(tok_ref, TOK_W, T_QD)
        keys = [col(r, LOC_W, L_KD) for r in loc_refs] + ([xcol(X_KD)] if cached else [])
        vals = [col(r, LOC_W, L_VD) for r in loc_refs] + ([xcol(X_VD)] if cached else [])
        for e in range(2):
            qm = jnp.where(_lane_mask(HEAD_DIM * e, HEAD_DIM), qd, jnp.zeros_like(qd))
            biases = [None] * len(keys)
            if cached:
                biases = [bias_ref[2 * hp + e, :, TQ * j:TQ * (j + 1)]
                          for j in range(len(loc_refs))] + [None]
            jobs.append(([(qm, keys, biases)], vals, None, HEAD_DIM * (1 - e)))

    outs = _attend_all(jobs)
    first = _lane_mask(0, HEAD_DIM)
    per_pair = len(jobs) // N_PAIRS
    for hp in range(N_PAIRS):
        o_a0, o_a1, o_c0, o_c1, o_d0, o_d1 = outs[per_pair * hp:per_pair * (hp + 1)]
        gate = lambda off, hp=hp: tok_ref[:, TOK_W * hp + off:TOK_W * hp + off + LANES].astype(F32)
        half = slice(LANES * hp, LANES * (hp + 1))
        o = jnp.where(first, o_a0, o_a1)
        o2 = o * o
        ss0 = jnp.sum(jnp.where(first, o2, 0.0), axis=-1, keepdims=True)
        ss1 = jnp.sum(jnp.where(first, 0.0, o2), axis=-1, keepdims=True)
        ms = jnp.where(first, ss0, ss1) * (1.0 / HEAD_DIM)
        y = o * lax.rsqrt(ms + EPS) * g_ref[...] * (1.0 - lam_init)
        oa_ref[:, half] = (y * gate(T_SZA)).astype(BF16)
        oc_ref[:, half] = (jnp.where(first, o_c0, o_c1) * gate(T_SCZ)).astype(BF16)
        od_ref[:, half] = (jnp.where(first, o_d0, o_d1) * gate(T_SDZ)).astype(BF16)


def _attn_call(u_tok, u_seq, u_loc, cache, bias, lam_vecs, subln_g2, *, layer, lam_init, nb, seq,
               tq, name):
    cached = cache is not None
    nq = seq // tq
    in_specs = [pl.BlockSpec((tq, N_PAIRS * TOK_W), lambda b, i: (b * nq + i, 0)),
                pl.BlockSpec((seq, N_PAIRS * SEQ_W), lambda b, i: (b, 0))]
    args = [u_tok, u_seq]
    if cached:
        past = cache.shape[3]
        base = lambda i: jnp.clip(i - 1, 0, nq - LOCAL_CHUNKS)
        for j in range(LOCAL_CHUNKS):
            in_specs.append(pl.BlockSpec(
                (tq, N_PAIRS * LOC_W), lambda b, i, j=j: (b * nq + base(i) + j, 0)))
            args.append(u_loc)

        def bias_map(b, i):
            pattern = jnp.where(i == 0, 0, jnp.where(i == nq - 1, 2, 1))
            return (layer, pattern, 0, 0, 0)

        in_specs += [pl.BlockSpec((None, None, N_PAIRS, past, CACHE_W),
                                  lambda b, i: (b, layer, 0, 0, 0)),
                     pl.BlockSpec((None, None, N_HEADS_GRP, tq, LOCAL_CHUNKS * tq), bias_map)]
        args += [cache, bias]
    else:
        in_specs.append(pl.BlockSpec((seq, N_PAIRS * LOC_W), lambda b, i: (b, 0)))
        args.append(u_loc)
    in_specs += [_layer_spec((4, DA_SUB), layer), _layer_spec((1, LANES), layer)]
    args += [lam_vecs, subln_g2]
    out_spec = pl.BlockSpec((tq, GROUP_W), lambda b, i: (b * nq + i, 0))
    out_shape = jax.ShapeDtypeStruct((nb * seq, GROUP_W), BF16)
    return pl.pallas_call(
        functools.partial(_attn_kernel, cached=cached, lam_init=lam_init),
        grid=(nb, nq),
        in_specs=in_specs,
        out_specs=[out_spec] * 3,
        out_shape=[out_shape] * 3,
        compiler_params=_params(2),
        name=name,
    )(*args)


def _local_bias_tables(rpb, rows):
    nq = rows // Q_ROWS
    n_dy, n_dx = 2 * NA_WIN_H - 1, 2 * NA_WIN_W - 1
    qc = np.arange(GRID_W)[:, None]
    kc = np.arange(GRID_W)[None, :]
    ws = np.clip(qc - NA_WIN_W // 2, 0, GRID_W - NA_WIN_W)
    col_ok = (kc >= ws) & (kc < ws + NA_WIN_W)
    oh_dx = ((kc - qc + NA_WIN_W - 1)[None] == np.arange(n_dx)[:, None, None]) & col_ok[None]
    tile_idx = []
    for i in (0, 1, nq - 1):
        row0 = Q_ROWS * int(np.clip(i - 1, 0, nq - LOCAL_CHUNKS))
        r = (Q_ROWS * i + np.arange(Q_ROWS))[:, None]
        kr = (row0 + np.arange(LOCAL_ROWS))[None, :]
        rs = np.clip(r - NA_WIN_H // 2, 0, rows - NA_WIN_H)
        ok = (kr >= rs) & (kr < rs + NA_WIN_H)
        tile_idx.append(np.where(ok, kr - r + NA_WIN_H - 1, n_dy))
    tile_idx = np.stack(tile_idx)
    hi = lax.Precision.HIGHEST
    cols = jnp.einsum("lhyd,dqc->lhyqc", rpb.astype(F32), jnp.asarray(oh_dx, F32), precision=hi)
    cols = jnp.where(col_ok, cols * LOG2E, NEG_INF)
    cols = jnp.concatenate(
        [cols, jnp.full((DEPTH, N_HEADS_GRP, 1, GRID_W, GRID_W), NEG_INF, F32)], axis=2)
    cols = jnp.concatenate([cols, cols], axis=-1)

    def build(cols_ref, o_ref):
        left = _lane_mask(0, GRID_W)
        for p in range(3):
            for j in range(Q_ROWS):
                for m in range(LOCAL_ROWS // 2):
                    a, b = int(tile_idx[p, j, 2 * m]), int(tile_idx[p, j, 2 * m + 1])
                    o_ref[p, GRID_W * j:GRID_W * (j + 1), LANES * m:LANES * (m + 1)] = jnp.where(
                        left, cols_ref[a], cols_ref[b])

    return pl.pallas_call(
        build,
        grid=(DEPTH, N_HEADS_GRP),
        in_specs=[pl.BlockSpec((None, None, n_dy + 1, GRID_W, LANES), lambda l, h: (l, h, 0, 0, 0))],
        out_specs=pl.BlockSpec((None, 3, None, TQ, LOCAL_ROWS * GRID_W), lambda l, h: (l, 0, h, 0, 0)),
        out_shape=jax.ShapeDtypeStruct((DEPTH, 3, N_HEADS_GRP, TQ, LOCAL_ROWS * GRID_W), F32),
        compiler_params=_params(2),
        name="local_bias",
    )(cols)


def _out_kernel(x_ref, mod_ref, ya_ref, bb_ref, g_ref, gp_ref, gn_ref, sbz_ref, yc_ref, yd_ref,
                cw_ref, w_ref, fg_ref, o_ref, *, nblk, final):
    tm = x_ref.shape[0]
    pos = pl.program_id(0) % nblk
    has_prev = jnp.where(pos != 0, 1.0, 0.0)
    has_next = jnp.where(pos != nblk - 1, 1.0, 0.0)
    g = g_ref[...].astype(F32)
    rows = lax.broadcasted_iota(jnp.int32, (tm, 1), 0)
    halo = gp_ref.shape[0]
    g_prev = jnp.where(rows == 0, gp_ref[halo - 1:halo, :].astype(F32) * has_prev,
                       pltpu.roll(g, 1, 0))
    g_next = jnp.where(rows == tm - 1, gn_ref[0:1, :].astype(F32) * has_next,
                       pltpu.roll(g, tm - 1, 0))
    cw = cw_ref[...]
    conv = g_prev * cw[0:1] + g * cw[1:2] + g_next * cw[2:3]
    yb = (bb_ref[...].astype(F32) * conv * sbz_ref[...].astype(F32)).astype(BF16)
    proj = (_dot(ya_ref[...], w_ref[0:GROUP_W, :]) + _dot(yc_ref[...], w_ref[2 * GROUP_W:3 * GROUP_W, :])
            + _dot(yd_ref[...], w_ref[3 * GROUP_W:, :]) + _dot(yb, w_ref[GROUP_W:2 * GROUP_W, :]))
    xn = x_ref[...] + mod_ref[2:3, :] * proj
    if final:
        xn = _rms(xn, fg_ref[...])
    o_ref[...] = xn


def _out_call(x, mod, u_conv, ya, yc, yd, conv_w, w_out, final_g, *, layer, seq, tm, mod_row0,
              final, name):
    t = x.shape[0]
    nblk = seq // tm
    halo = 16
    hb = tm // halo
    last = t // halo - 1
    cb = lambda off: off // GROUP_W
    row = lambda i: (i, 0)
    in_specs = [
        pl.BlockSpec((tm, D_MODEL), row),
        pl.BlockSpec((None, None, 3, D_MODEL), _mod_map(layer, mod_row0, nblk)),
        pl.BlockSpec((tm, GROUP_W), row),
        pl.BlockSpec((tm, GROUP_W), lambda i: (i, cb(B_BB))),
        pl.BlockSpec((tm, GROUP_W), lambda i: (i, cb(B_G))),
        pl.BlockSpec((halo, GROUP_W), lambda i: (jnp.maximum(i * hb - 1, 0), cb(B_G))),
        pl.BlockSpec((halo, GROUP_W), lambda i: (jnp.minimum((i + 1) * hb, last), cb(B_G))),
        pl.BlockSpec((tm, GROUP_W), lambda i: (i, cb(B_SBZ))),
        pl.BlockSpec((tm, GROUP_W), row),
        pl.BlockSpec((tm, GROUP_W), row),
        _layer_spec((3, GROUP_W), layer),
        _layer_spec((D_MODEL, D_MODEL), layer),
        pl.BlockSpec((1, D_MODEL), lambda i: (0, 0)),
    ]
    return pl.pallas_call(
        functools.partial(_out_kernel, nblk=nblk, final=final),
        grid=(t // tm,),
        in_specs=in_specs,
        out_specs=pl.BlockSpec((tm, D_MODEL), row),
        out_shape=jax.ShapeDtypeStruct((t, D_MODEL), F32),
        compiler_params=_params(1),
        name=name,
    )(x, mod, ya, u_conv, u_conv, u_conv, u_conv, u_conv, yc, yd, conv_w, w_out, final_g)


def _w_in_prep_kernel(w_ref, o_ref):
    rows = w_ref.shape[1]
    kpe_end = W_KPE + MLA_ROPE
    o_ref[:, :W_KPE] = w_ref[:W_KPE, :].T.astype(BF16)
    kpe = jnp.concatenate([jnp.zeros((KPE_LANE, rows), F32), w_ref[W_KPE:kpe_end, :],
                           jnp.zeros((LANES - KPE_LANE - MLA_ROPE, rows), F32)], axis=0)
    o_ref[:, W_KPE:W_CZ] = kpe.T.astype(BF16)
    o_ref[:, W_CZ:] = w_ref[kpe_end:, :].T.astype(BF16)


def _w_in_prep(w_in):
    d_in = w_in.shape[-1]
    rows = 256
    return pl.pallas_call(
        _w_in_prep_kernel,
        grid=(DEPTH, D_MODEL // rows),
        in_specs=[pl.BlockSpec((None, d_in, rows), lambda l, r: (l, 0, r))],
        out_specs=pl.BlockSpec((None, rows, D_IN_P), lambda l, r: (l, r, 0)),
        out_shape=jax.ShapeDtypeStruct((DEPTH, D_MODEL, D_IN_P), BF16),
        compiler_params=_params(2),
        name="w_in_prep",
    )(jnp.swapaxes(w_in, 1, 2))


def _rope_tables(seq):
    t = jnp.arange(seq)
    rows = (t // GRID_W).astype(F32)
    cols = (t % GRID_W).astype(F32)
    half = DA_SUB // 2
    inv = 1.0 / (ROPE_BASE ** (jnp.arange(0, half, 2, dtype=F32) / half))
    ar = rows[:, None] * inv
    ac = cols[:, None] * inv
    ang = jnp.concatenate([ar, ar, ac, ac], axis=-1)
    cos, sin = jnp.cos(ang), jnp.sin(ang)
    first = (np.arange(DA_SUB) % 16 < 8)[None, :]
    sin_neg = jnp.where(first, -sin, 0.0)
    sin_pos = jnp.where(first, 0.0, sin)
    tile = lambda a: jnp.tile(a, (1, LANES // DA_SUB))

    def pad(a, fill):
        return jnp.concatenate([jnp.full((seq, KPE_LANE), fill, F32), a,
                                jnp.full((seq, LANES - KPE_LANE - MLA_ROPE), fill, F32)], axis=1)

    return ([tile(cos), tile(sin_neg), tile(sin_pos)],
            [pad(cos, 1.0), pad(sin_neg, 0.0), pad(sin_pos, 0.0)])


def _pad_heads(w, width, take):
    d, k, _ = w.shape
    w = w.reshape(d, k, N_HEADS_GRP, width)[..., :take]
    return jnp.pad(w, ((0, 0), (0, 0), (0, 0), (0, LANES - take))).reshape(d, k, N_HEADS_GRP * LANES)


def kernel(x_prompt, x_sample, cache_a_k, cache_a_v, cache_c_kv, cache_c_kpe, cache_d_k, cache_d_v,
           c, c_ctx, ada_w, ada_b, norm_g, w_in, da_lambda, da_subln_g, conv_w,
           mla_q_norm_g, mla_w_uq, mla_kv_norm_g, mla_w_ukv, na_rpb, w_out, final_norm_g):
    batch, seq, _ = x_prompt.shape
    dec_batch, dec_seq, _ = x_sample.shape
    past = cache_a_k.shape[2]
    assert dec_seq % TQ == 0 and dec_seq // TQ >= LOCAL_CHUNKS and seq % LANES == 0

    w_in_p = _w_in_prep(w_in)
    w_out_b = w_out.astype(BF16)
    wuq = _pad_heads(mla_w_uq, MLA_NOPE + MLA_ROPE, MLA_NOPE + MLA_ROPE).astype(BF16)
    wuk = _pad_heads(mla_w_ukv, MLA_NOPE + MLA_V, MLA_NOPE).astype(BF16)
    wuv = mla_w_ukv.reshape(DEPTH, MLA_KV_RANK, N_HEADS_GRP, MLA_NOPE + MLA_V)[..., MLA_NOPE:]
    wuv = wuv.reshape(DEPTH, MLA_KV_RANK, GROUP_W).astype(BF16)
    subln_g2 = jnp.tile(da_subln_g, (1, LANES // HEAD_DIM)).reshape(DEPTH, 1, LANES)
    ng = norm_g.reshape(DEPTH, 1, D_MODEL)
    qng = mla_q_norm_g.reshape(DEPTH, 1, MLA_Q_RANK)
    kvng = mla_kv_norm_g.reshape(DEPTH, 1, MLA_KV_RANK)
    fg = final_norm_g.reshape(1, D_MODEL)
    tables_a, tables_c = _rope_tables(dec_seq)
    tables = tables_a + tables_c

    cvec8 = jnp.concatenate([c_ctx[None], c, jnp.zeros((8 - 1 - dec_batch, D_MODEL), F32)], axis=0)
    mod = _ada_call(cvec8, ada_w, ada_b).reshape(DEPTH, 8, 3, D_MODEL)

    feat = lambda a: a.transpose(0, 1, 3, 4, 2).reshape(dec_batch, DEPTH, GROUP_W, past)
    cache = _cache_call(feat(cache_a_k), feat(cache_a_v), cache_c_kv,
                        cache_c_kpe.transpose(0, 1, 3, 2), feat(cache_d_k), feat(cache_d_v), wuk, wuv)
    bias = _local_bias_tables(na_rpb, dec_seq // GRID_W)

    xp = x_prompt.reshape(batch * seq, D_MODEL)
    xs = x_sample.reshape(dec_batch * dec_seq, D_MODEL)
    states = []
    for l in range(DEPTH):
        lam_init = 0.8 - 0.6 * math.exp(-0.3 * l)
        final = l == DEPTH - 1

        tok, sq, loc, cv, *states = _in_call(xp, mod, ng, w_in_p, qng, wuq, kvng, wuk, wuv, None,
                                             layer=l, seq=seq, tm=seq, mod_row0=0, states=states,
                                             name=f"ctx_in_{l}")
        ya, yc, yd = _attn_call(tok, sq, loc, None, None, da_lambda, subln_g2, layer=l,
                                lam_init=lam_init, nb=batch, seq=seq, tq=seq, name=f"ctx_attn_{l}")
        xp = _out_call(xp, mod, cv, ya, yc, yd, conv_w, w_out_b, fg, layer=l, seq=seq, tm=seq,
                       mod_row0=0, final=final, name=f"ctx_out_{l}")

        tok, sq, loc, cv = _in_call(xs, mod, ng, w_in_p, qng, wuq, kvng, wuk, wuv, tables,
                                    layer=l, seq=dec_seq, tm=512, mod_row0=1, states=None,
                                    name=f"lat_in_{l}")
        ya, yc, yd = _attn_call(tok, sq, loc, cache, bias, da_lambda, subln_g2, layer=l,
                                lam_init=lam_init, nb=dec_batch, seq=dec_seq, tq=TQ,
                                name=f"lat_attn_{l}")
        xs = _out_call(xs, mod, cv, ya, yc, yd, conv_w, w_out_b, fg, layer=l, seq=dec_seq, tm=512,
                       mod_row0=1, final=final, name=f"lat_out_{l}")

    def heads(a):
        return a.reshape(batch, DEPTH, N_HEADS_GRP, HEAD_DIM, seq).transpose(0, 1, 4, 2, 3)

    s_ak, s_av, s_ckv, s_kpe, s_dk, s_dv = states
    return (xp.reshape(batch, seq, D_MODEL), xs.reshape(dec_batch, dec_seq, D_MODEL),
            heads(s_ak), heads(s_av), s_ckv, s_kpe.transpose(0, 1, 3, 2), heads(s_dk), heads(s_dv))
```

```python
import functools
import math

import jax
import jax.numpy as jnp
import numpy as np
from jax import lax
from jax.experimental import pallas as pl
from jax.experimental.pallas import tpu as pltpu

F32 = jnp.float32
BF16 = jnp.bfloat16

D_MODEL = 1024
DEPTH = 4
GRID_W = 64
HEAD_DIM = 64
GROUP_W = 256
N_HEADS_GRP = 4
DA_SUB = 32
MLA_Q_RANK = 256
MLA_KV_RANK = 128
MLA_NOPE = 64
MLA_ROPE = 32
MLA_V = 64
LOG2E = math.log2(math.e)
MLA_SCALE = (MLA_NOPE + MLA_ROPE) ** -0.5
DA_SCALE = DA_SUB ** -0.5
NA_SCALE = HEAD_DIM ** -0.5
NA_WIN_H = 8
NA_WIN_W = 16
ROPE_BASE = 10000.0
EPS = 1e-6
NEG_INF = -1e30

LANES = 128
N_PAIRS = GROUP_W // LANES
VMEM_LIMIT = 56 * 1024 * 1024

W_AQ, W_AK, W_AV, W_AZ = 0, 256, 512, 768
W_BB, W_BC, W_BH, W_BZ = 1024, 1280, 1536, 1792
W_CQ, W_CKV, W_KPE, W_CZ = 2048, 2304, 2432, 2560
W_DQ, W_DK, W_DV, W_DZ = 2816, 3072, 3328, 3584
D_IN_P = 3840
KPE_LANE = 64

T_QA, T_SZA, T_QC, T_SCZ, T_QD, T_SDZ, TOK_W = 0, 128, 256, 512, 640, 768, 896
S_KA, S_VA, S_KC, S_VC, SEQ_W = 0, 128, 256, 512, 640
L_KD, L_VD, LOC_W = 0, 128, 256
X_KA, X_VA, X_KC, X_VC, X_KD, X_VD, CACHE_W = 0, 128, 256, 512, 640, 768, 896
B_BB, B_G, B_SBZ, CONV_W = 0, 256, 512, 768

Q_ROWS = 4
TQ = Q_ROWS * GRID_W
LOCAL_CHUNKS = 3
LOCAL_ROWS = LOCAL_CHUNKS * Q_ROWS


def _dot(a, b):
    return jnp.dot(a, b, preferred_element_type=F32)


def _dot_nt(a, b):
    return lax.dot_general(a, b, (((1,), (1,)), ((), ())), preferred_element_type=F32)


def _silu(z):
    return z * (1.0 / (1.0 + jnp.exp(-z)))


def _rms(x, g):
    return x * lax.rsqrt(jnp.mean(x * x, axis=-1, keepdims=True) + EPS) * g


def _rope128(x, cos, sin_neg, sin_pos):
    return x * cos + pltpu.roll(x, LANES - 8, 1) * sin_neg + pltpu.roll(x, 8, 1) * sin_pos


def _scores(job):
    out = []
    for q, keys, biases in job[0]:
        blocks = []
        for k, b in zip(keys, biases):
            s = _dot_nt(q, k)
            blocks.append(s if b is None else s + b)
        out.append(blocks)
    return out


def _softmax_parts(blocks):
    m = functools.reduce(jnp.maximum, [jnp.max(s, axis=-1, keepdims=True) for s in blocks])
    ps = [jnp.exp2(s - m) for s in blocks]
    l = functools.reduce(jnp.add, [jnp.sum(p, axis=-1, keepdims=True) for p in ps])
    return ps, l


def _weighted_values(job, scores):
    _, values, lam, sum_lane = job
    if len(scores) == 1:
        blocks = scores[0]
        m = functools.reduce(jnp.maximum, [jnp.max(s, axis=-1, keepdims=True) for s in blocks])
        one_hot = _lane_mask(sum_lane, 1)
        o = functools.reduce(jnp.add, [
            _dot(jnp.exp2(s - m).astype(BF16), jnp.where(one_hot, jnp.ones_like(v), v))
            for s, v in zip(blocks, values)])
        l = jnp.sum(jnp.where(one_hot, o, 0.0), axis=-1, keepdims=True)
        return o / l
    (p1, l1), (p2, l2) = _softmax_parts(scores[0]), _softmax_parts(scores[1])
    r1, r2 = 1.0 / l1, lam / l2
    return functools.reduce(
        jnp.add, [_dot((a * r1 - b * r2).astype(BF16), v) for a, b, v in zip(p1, p2, values)])


LOOKAHEAD = 1


def _attend_all(jobs, after=None):
    outs = []
    pending = [_scores(job) for job in jobs[:LOOKAHEAD]]
    for i, job in enumerate(jobs):
        if i + LOOKAHEAD < len(jobs):
            pending.append(_scores(jobs[i + LOOKAHEAD]))
        outs.append(_weighted_values(job, pending.pop(0)))
        if after and i in after:
            after[i](outs)
    return outs


def _lane_mask(lo, width):
    lane = lax.broadcasted_iota(jnp.int32, (1, LANES), 1)
    return jnp.logical_and(lane >= lo, lane < lo + width)


def _params(n_axes):
    return pltpu.CompilerParams(dimension_semantics=("arbitrary",) * n_axes,
                                vmem_limit_bytes=VMEM_LIMIT)


def _mod_map(layer, mod_row0, nblk):
    if mod_row0 == 0:
        return lambda i: (layer, 0, 0, 0)
    return lambda i: (layer, mod_row0 + i // nblk, 0, 0)


def _layer_spec(shape, layer):
    zeros = (0,) * len(shape)
    return pl.BlockSpec((None,) + tuple(shape), lambda *_: (layer,) + zeros)


def _ada_kernel(c_ref, w_ref, b_ref, o_ref):
    s = _silu(c_ref[...]).astype(BF16)
    o_ref[...] = _dot(s, w_ref[...].astype(BF16)) + b_ref[...]


def _ada_call(cvec8, ada_w, ada_b):
    nj = 3
    return pl.pallas_call(
        _ada_kernel,
        grid=(DEPTH, nj),
        in_specs=[pl.BlockSpec((8, D_MODEL), lambda l, j: (0, 0)),
                  pl.BlockSpec((None, D_MODEL, D_MODEL), lambda l, j: (l, 0, j)),
                  pl.BlockSpec((None, None, 1, D_MODEL), lambda l, j: (l, j, 0, 0))],
        out_specs=pl.BlockSpec((None, 8, D_MODEL), lambda l, j: (l, 0, j)),
        out_shape=jax.ShapeDtypeStruct((DEPTH, 8, 3 * D_MODEL), F32),
        compiler_params=_params(2),
        name="adaln",
    )(cvec8, ada_w, ada_b.reshape(DEPTH, nj, 1, D_MODEL))


def _cache_kernel(ak_ref, av_ref, ckv_ref, kpe_ref, dk_ref, dv_ref, wuk_ref, wuv_ref, o_ref):
    ckv = ckv_ref[...].astype(BF16)
    kn = _dot(ckv, wuk_ref[...])
    vc = _dot(ckv, wuv_ref[...])
    past = ckv.shape[0]
    kpe = jnp.concatenate([jnp.zeros((KPE_LANE, past), F32), kpe_ref[...],
                           jnp.zeros((LANES - KPE_LANE - MLA_ROPE, past), F32)], axis=0).T
    for hp in range(N_PAIRS):
        half = slice(LANES * hp, LANES * (hp + 1))
        o_ref[hp, :, X_KA:X_KA + LANES] = ak_ref[half, :].T.astype(BF16)
        o_ref[hp, :, X_VA:X_VA + LANES] = av_ref[half, :].T.astype(BF16)
        for e in range(2):
            sl = slice(LANES * (2 * hp + e), LANES * (2 * hp + e + 1))
            o_ref[hp, :, X_KC + LANES * e:X_KC + LANES * (e + 1)] = (kn[:, sl] + kpe).astype(BF16)
        o_ref[hp, :, X_VC:X_VC + LANES] = vc[:, half].astype(BF16)
        o_ref[hp, :, X_KD:X_KD + LANES] = dk_ref[half, :].T.astype(BF16)
        o_ref[hp, :, X_VD:X_VD + LANES] = dv_ref[half, :].T.astype(BF16)


def _cache_call(ca_k, ca_v, c_kv, c_kpe, cd_k, cd_v, wuk, wuv):
    nb, _, past, _ = c_kv.shape
    cache = lambda w: pl.BlockSpec((None, None, past, w), lambda l, b: (b, l, 0, 0))
    feat = lambda rows: pl.BlockSpec((None, None, rows, past), lambda l, b: (b, l, 0, 0))
    return pl.pallas_call(
        _cache_kernel,
        grid=(DEPTH, nb),
        in_specs=[feat(GROUP_W), feat(GROUP_W), cache(MLA_KV_RANK), feat(MLA_ROPE),
                  feat(GROUP_W), feat(GROUP_W),
                  pl.BlockSpec((None, MLA_KV_RANK, 4 * LANES), lambda l, b: (l, 0, 0)),
                  pl.BlockSpec((None, MLA_KV_RANK, GROUP_W), lambda l, b: (l, 0, 0))],
        out_specs=pl.BlockSpec((None, None, N_PAIRS, past, CACHE_W), lambda l, b: (b, l, 0, 0, 0)),
        out_shape=jax.ShapeDtypeStruct((nb, DEPTH, N_PAIRS, past, CACHE_W), BF16),
        compiler_params=_params(2),
        name="cache_prep",
    )(ca_k, ca_v, c_kv, c_kpe, cd_k, cd_v, wuk, wuv)


def _in_kernel(*refs, rope, states):
    it = iter(refs)
    x_ref, mod_ref, ng_ref, w_ref, qng_ref, wuq_ref, kvng_ref, wuk_ref, wuv_ref = (
        next(it) for _ in range(9))
    if rope:
        ta = [next(it)[...] for _ in range(3)]
        tc = [next(it)[...] for _ in range(3)]
    if states == "update":
        for _ in range(6):
            next(it)
    tok_ref, seq_ref, loc_ref, conv_ref = (next(it) for _ in range(4))
    if states:
        sak_ref, sav_ref, sckv_ref, skpe_ref, sdk_ref, sdv_ref = (next(it) for _ in range(6))

    x = x_ref[...]
    h = _rms(x, ng_ref[...]) * (1.0 + mod_ref[1:2, :]) + mod_ref[0:1, :]
    hb = h.astype(BF16)

    def seg(off, n=GROUP_W):
        return _dot(hb, w_ref[:, off:off + n])

    def put_state(ref, val):
        if states == "create":
            ref[0] = val
            for later in range(1, DEPTH):
                ref[later] = jnp.zeros_like(val)
        else:
            ref[...] = val

    def put(ref, width, off, val):
        for hp in range(N_PAIRS):
            ref[:, width * hp + off:width * hp + off + LANES] = (
                val[:, LANES * hp:LANES * (hp + 1)].astype(BF16))

    cqn = _rms(seg(W_CQ), qng_ref[...]).astype(BF16)
    ckvn = _rms(seg(W_CKV, MLA_KV_RANK), kvng_ref[...])
    kpe = seg(W_KPE, LANES)
    if states:
        put_state(sckv_ref, ckvn)
        put_state(skpe_ref, kpe.T[KPE_LANE:KPE_LANE + MLA_ROPE, :])
    ckvb = ckvn.astype(BF16)

    aq, ak, av = seg(W_AQ), seg(W_AK), seg(W_AV)
    if states:
        put_state(sak_ref, ak.T)
        put_state(sav_ref, av.T)
    if rope:
        aq = jnp.concatenate([_rope128(aq[:, :LANES], *ta), _rope128(aq[:, LANES:], *ta)], axis=1)
        ak = jnp.concatenate([_rope128(ak[:, :LANES], *ta), _rope128(ak[:, LANES:], *ta)], axis=1)
    put(tok_ref, TOK_W, T_QA, aq * (DA_SCALE * LOG2E))
    put(seq_ref, SEQ_W, S_KA, ak)
    put(seq_ref, SEQ_W, S_VA, av)
    put(tok_ref, TOK_W, T_SZA, _silu(seg(W_AZ)))

    conv_ref[:, B_BB:B_BB + GROUP_W] = seg(W_BB).astype(BF16)
    conv_ref[:, B_G:B_G + GROUP_W] = (seg(W_BC) * seg(W_BH)).astype(BF16)
    conv_ref[:, B_SBZ:B_SBZ + GROUP_W] = _silu(seg(W_BZ)).astype(BF16)

    dk, dv = seg(W_DK), seg(W_DV)
    if states:
        put_state(sdk_ref, dk.T)
        put_state(sdv_ref, dv.T)
    put(tok_ref, TOK_W, T_QD, seg(W_DQ) * (NA_SCALE * LOG2E))
    put(loc_ref, LOC_W, L_KD, dk)
    put(loc_ref, LOC_W, L_VD, dv)
    put(tok_ref, TOK_W, T_SDZ, _silu(seg(W_DZ)))
    put(tok_ref, TOK_W, T_SCZ, _silu(seg(W_CZ)))

    q = _dot(cqn, wuq_ref[...])
    kn = _dot(ckvb, wuk_ref[...])
    vc = _dot(ckvb, wuv_ref[...])
    kpe_r = _rope128(kpe, *tc) if rope else kpe
    for hd in range(N_HEADS_GRP):
        sl = slice(LANES * hd, LANES * (hd + 1))
        hp, e = divmod(hd, 2)
        qh = q[:, sl]
        if rope:
            qh = _rope128(qh, *tc)
        o = LANES * e
        tok_ref[:, TOK_W * hp + T_QC + o:TOK_W * hp + T_QC + o + LANES] = (
            qh * (MLA_SCALE * LOG2E)).astype(BF16)
        seq_ref[:, SEQ_W * hp + S_KC + o:SEQ_W * hp + S_KC + o + LANES] = (
            kn[:, sl] + kpe_r).astype(BF16)
    put(seq_ref, SEQ_W, S_VC, vc)


def _state_layout(batch, seq, tm):
    nblk = seq // tm
    lblk = lambda layer: DEPTH if layer is None else None
    lidx = lambda layer: 0 if layer is None else layer
    feat = lambda rows: ((batch, DEPTH, rows, seq),
                         lambda layer: (None, lblk(layer), rows, tm),
                         lambda layer: lambda i: (i // nblk, lidx(layer), 0, i % nblk))
    tokm = ((batch, DEPTH, seq, MLA_KV_RANK),
            lambda layer: (None, lblk(layer), tm, MLA_KV_RANK),
            lambda layer: lambda i: (i // nblk, lidx(layer), i % nblk, 0))
    return [feat(GROUP_W), feat(GROUP_W), tokm, feat(MLA_ROPE), feat(GROUP_W), feat(GROUP_W)]


def _in_call(x, mod, norm_g, w_in, q_norm_g, wuq, kv_norm_g, wuk, wuv, tables, *, layer, seq, tm,
             mod_row0, states, name):
    t = x.shape[0]
    nblk = seq // tm
    rope = tables is not None
    in_specs = [
        pl.BlockSpec((tm, D_MODEL), lambda i: (i, 0)),
        pl.BlockSpec((None, None, 3, D_MODEL), _mod_map(layer, mod_row0, nblk)),
        _layer_spec((1, D_MODEL), layer),
        _layer_spec((D_MODEL, D_IN_P), layer),
        _layer_spec((1, MLA_Q_RANK), layer),
        _layer_spec((MLA_Q_RANK, 4 * LANES), layer),
        _layer_spec((1, MLA_KV_RANK), layer),
        _layer_spec((MLA_KV_RANK, 4 * LANES), layer),
        _layer_spec((MLA_KV_RANK, GROUP_W), layer),
    ]
    args = [x, mod, norm_g, w_in, q_norm_g, wuq, kv_norm_g, wuk, wuv]
    if rope:
        in_specs += [pl.BlockSpec((tm, LANES), lambda i: (i % nblk, 0))] * 6
        args += list(tables)
    widths = [N_PAIRS * TOK_W, N_PAIRS * SEQ_W, N_PAIRS * LOC_W, CONV_W]
    out_specs = [pl.BlockSpec((tm, w), lambda i: (i, 0)) for w in widths]
    out_shape = [jax.ShapeDtypeStruct((t, w), BF16) for w in widths]
    aliases, mode = {}, False
    if states is not None:
        mode = "update" if states else "create"
        assert states or layer == 0
        at = layer if states else None
        for k, (shape, block, index) in enumerate(_state_layout(t // seq, seq, tm)):
            out_specs.append(pl.BlockSpec(block(at), index(at)))
            out_shape.append(jax.ShapeDtypeStruct(shape, F32))
            if states:
                aliases[len(args)] = len(widths) + k
                in_specs.append(pl.BlockSpec(memory_space=pl.ANY))
                args.append(states[k])
    return pl.pallas_call(
        functools.partial(_in_kernel, rope=rope, states=mode),
        grid=(t // tm,),
        in_specs=in_specs,
        out_specs=out_specs,
        out_shape=out_shape,
        input_output_aliases=aliases,
        compiler_params=_params(1),
        name=name,
    )(*args)


def _attn_kernel(*refs, cached, lam_init, final):
    it = iter(refs)
    tok_ref, seq_ref = next(it), next(it)
    loc_refs = [next(it) for _ in range(LOCAL_CHUNKS if cached else 1)]
    if cached:
        x_ref, bias_ref = next(it), next(it)
    lam_ref, g_ref = next(it), next(it)
    res_ref, mod_ref, bb_ref, gc_ref, gp_ref, gn_ref, sbz_ref, cw_ref, w_ref, fg_ref = (
        next(it) for _ in range(10))
    o_ref = next(it)

    lv = lam_ref[...]
    lam = (jnp.exp(jnp.sum(lv[0:1] * lv[1:2], keepdims=True))
           - jnp.exp(jnp.sum(lv[2:3] * lv[3:4], keepdims=True)) + lam_init)

    tq = res_ref.shape[0]
    i, nq = pl.program_id(1), pl.num_programs(1)
    has_prev = jnp.where(i != 0, 1.0, 0.0)
    has_next = jnp.where(i != nq - 1, 1.0, 0.0)
    gc = gc_ref[...].astype(F32)
    rows = lax.broadcasted_iota(jnp.int32, (tq, 1), 0)
    halo = gp_ref.shape[0]
    g_prev = jnp.where(rows == 0, gp_ref[halo - 1:halo, :].astype(F32) * has_prev,
                       pltpu.roll(gc, 1, 0))
    g_next = jnp.where(rows == tq - 1, gn_ref[0:1, :].astype(F32) * has_next,
                       pltpu.roll(gc, tq - 1, 0))
    cw = cw_ref[...]
    conv = g_prev * cw[0:1] + gc * cw[1:2] + g_next * cw[2:3]
    yb = (bb_ref[...].astype(F32) * conv * sbz_ref[...].astype(F32)).astype(BF16)

    first = _lane_mask(0, HEAD_DIM)
    proj = []

    def gate(hp, off):
        return tok_ref[:, TOK_W * hp + off:TOK_W * hp + off + LANES].astype(F32)

    def project(group, halves):
        y = jnp.concatenate(halves, axis=1).astype(BF16)
        proj.append(_dot(y, w_ref[GROUP_W * group:GROUP_W * (group + 1), :]))

    def after_a(outs):
        proj.append(_dot(yb, w_ref[GROUP_W:2 * GROUP_W, :]))
        halves = []
        for hp in range(N_PAIRS):
            o = jnp.where(first, outs[2 * hp], outs[2 * hp + 1])
            o2 = o * o
            ss0 = jnp.sum(jnp.where(first, o2, 0.0), axis=-1, keepdims=True)
            ss1 = jnp.sum(jnp.where(first, 0.0, o2), axis=-1, keepdims=True)
            ms = jnp.where(first, ss0, ss1) * (1.0 / HEAD_DIM)
            y = o * lax.rsqrt(ms + EPS) * g_ref[...] * (1.0 - lam_init)
            halves.append(y * gate(hp, T_SZA))
        project(0, halves)

    def after_group(group, base, off):
        def run(outs):
            project(group, [jnp.where(first, outs[base + 2 * hp], outs[base + 2 * hp + 1])
                            * gate(hp, off) for hp in range(N_PAIRS)])
        return run

    jobs_a, jobs_c, jobs_d = [], [], []
    for hp in range(N_PAIRS):
        def col(ref, width, off, w=LANES, hp=hp):
            return ref[:, width * hp + off:width * hp + off + w]

        xcol = lambda off, hp=hp: x_ref[hp, :, off:off + LANES]
        qa = col(tok_ref, TOK_W, T_QA)
        keys = [col(seq_ref, SEQ_W, S_KA)] + ([xcol(X_KA)] if cached else [])
        vals = [col(seq_ref, SEQ_W, S_VA)] + ([xcol(X_VA)] if cached else [])
        nob = [None] * len(keys)
        for e in range(2):
            maps = []
            for c in range(2):
                qm = jnp.where(_lane_mask(HEAD_DIM * e + DA_SUB * c, DA_SUB), qa, jnp.zeros_like(qa))
                maps.append((qm, keys, nob))
            jobs_a.append((maps, vals, lam, None))
        vals = [col(seq_ref, SEQ_W, S_VC)] + ([xcol(X_VC)] if cached else [])
        for e in range(2):
            keys = [col(seq_ref, SEQ_W, S_KC + LANES * e)] + (
                [xcol(X_KC + LANES * e)] if cached else [])
            jobs_c.append(([(col(tok_ref, TOK_W, T_QC + LANES * e), keys, nob)], vals, None,
                           HEAD_DIM * (1 - e)))
        qd = col(tok_ref, TOK_W, T_QD)
        keys = [col(r, LOC_W, L_KD) for r in loc_refs] + ([xcol(X_KD)] if cached else [])
        vals = [col(r, LOC_W, L_VD) for r in loc_refs] + ([xcol(X_VD)] if cached else [])
        for e in range(2):
            qm = jnp.where(_lane_mask(HEAD_DIM * e, HEAD_DIM), qd, jnp.zeros_like(qd))
            biases = [None] * len(keys)
            if cached:
                biases = [bias_ref[2 * hp + e, :, TQ * j:TQ * (j + 1)]
                          for j in range(len(loc_refs))] + [None]
            jobs_d.append(([(qm, keys, biases)], vals, None, HEAD_DIM * (1 - e)))

    n = 2 * N_PAIRS
    after = {n - 1: after_a, 2 * n - 1: after_group(2, n, T_SCZ),
             3 * n - 1: after_group(3, 2 * n, T_SDZ)}
    _attend_all(jobs_a + jobs_c + jobs_d, after)
    xn = res_ref[...] + mod_ref[2:3, :] * functools.reduce(jnp.add, proj)
    if final:
        xn = _rms(xn, fg_ref[...])
    o_ref[...] = xn


def _attn_call(x, mod, u_tok, u_seq, u_loc, u_conv, cache, bias, lam_vecs, subln_g2, conv_w, w_out,
               final_g, *, layer, lam_init, nb, seq, tq, mod_row0, final, name):
    cached = cache is not None
    nq = seq // tq
    in_specs = [pl.BlockSpec((tq, N_PAIRS * TOK_W), lambda b, i: (b * nq + i, 0)),
                pl.BlockSpec((seq, N_PAIRS * SEQ_W), lambda b, i: (b, 0))]
    args = [u_tok, u_seq]
    if cached:
        past = cache.shape[3]
        base = lambda i: jnp.clip(i - 1, 0, nq - LOCAL_CHUNKS)
        for j in range(LOCAL_CHUNKS):
            in_specs.append(pl.BlockSpec(
                (tq, N_PAIRS * LOC_W), lambda b, i, j=j: (b * nq + base(i) + j, 0)))
            args.append(u_loc)

        def bias_map(b, i):
            pattern = jnp.where(i == 0, 0, jnp.where(i == nq - 1, 2, 1))
            return (layer, pattern, 0, 0, 0)

        in_specs += [pl.BlockSpec((None, None, N_PAIRS, past, CACHE_W),
                                  lambda b, i: (b, layer, 0, 0, 0)),
                     pl.BlockSpec((None, None, N_HEADS_GRP, tq, LOCAL_CHUNKS * tq), bias_map)]
        args += [cache, bias]
    else:
        in_specs.append(pl.BlockSpec((seq, N_PAIRS * LOC_W), lambda b, i: (b, 0)))
        args.append(u_loc)
    in_specs += [_layer_spec((4, DA_SUB), layer), _layer_spec((1, LANES), layer)]
    args += [lam_vecs, subln_g2]

    halo = 16
    hb = tq // halo
    last = nb * seq // halo - 1
    blk = lambda b, i: b * nq + i
    conv = lambda off: pl.BlockSpec((tq, GROUP_W), lambda b, i: (blk(b, i), off // GROUP_W))
    mod_row = (lambda b: 0) if mod_row0 == 0 else (lambda b: mod_row0 + b)
    in_specs += [
        pl.BlockSpec((tq, D_MODEL), lambda b, i: (blk(b, i), 0)),
        pl.BlockSpec((None, None, 3, D_MODEL), lambda b, i: (layer, mod_row(b), 0, 0)),
        conv(B_BB), conv(B_G),
        pl.BlockSpec((halo, GROUP_W),
                     lambda b, i: (jnp.maximum(blk(b, i) * hb - 1, 0), B_G // GROUP_W)),
        pl.BlockSpec((halo, GROUP_W),
                     lambda b, i: (jnp.minimum((blk(b, i) + 1) * hb, last), B_G // GROUP_W)),
        conv(B_SBZ),
        _layer_spec((3, GROUP_W), layer),
        _layer_spec((D_MODEL, D_MODEL), layer),
        pl.BlockSpec((1, D_MODEL), lambda b, i: (0, 0)),
    ]
    args += [x, mod, u_conv, u_conv, u_conv, u_conv, u_conv, conv_w, w_out, final_g]
    return pl.pallas_call(
        functools.partial(_attn_kernel, cached=cached, lam_init=lam_init, final=final),
        grid=(nb, nq),
        in_specs=in_specs,
        out_specs=pl.BlockSpec((tq, D_MODEL), lambda b, i: (blk(b, i), 0)),
        out_shape=jax.ShapeDtypeStruct((nb * seq, D_MODEL), F32),
        compiler_params=_params(2),
        name=name,
    )(*args)


def _local_bias_tables(rpb, rows):
    nq = rows // Q_ROWS
    n_dy, n_dx = 2 * NA_WIN_H - 1, 2 * NA_WIN_W - 1
    qc = np.arange(GRID_W)[:, None]
    kc = np.arange(GRID_W)[None, :]
    ws = np.clip(qc - NA_WIN_W // 2, 0, GRID_W - NA_WIN_W)
    col_ok = (kc >= ws) & (kc < ws + NA_WIN_W)
    oh_dx = ((kc - qc + NA_WIN_W - 1)[None] == np.arange(n_dx)[:, None, None]) & col_ok[None]
    tile_idx = []
    for i in (0, 1, nq - 1):
        row0 = Q_ROWS * int(np.clip(i - 1, 0, nq - LOCAL_CHUNKS))
        r = (Q_ROWS * i + np.arange(Q_ROWS))[:, None]
        kr = (row0 + np.arange(LOCAL_ROWS))[None, :]
        rs = np.clip(r - NA_WIN_H // 2, 0, rows - NA_WIN_H)
        ok = (kr >= rs) & (kr < rs + NA_WIN_H)
        tile_idx.append(np.where(ok, kr - r + NA_WIN_H - 1, n_dy))
    tile_idx = np.stack(tile_idx)
    hi = lax.Precision.HIGHEST
    cols = jnp.einsum("lhyd,dqc->lhyqc", rpb.astype(F32), jnp.asarray(oh_dx, F32), precision=hi)
    cols = jnp.where(col_ok, cols * LOG2E, NEG_INF)
    cols = jnp.concatenate(
        [cols, jnp.full((DEPTH, N_HEADS_GRP, 1, GRID_W, GRID_W), NEG_INF, F32)], axis=2)
    cols = jnp.concatenate([cols, cols], axis=-1)

    def build(cols_ref, o_ref):
        left = _lane_mask(0, GRID_W)
        for p in range(3):
            for j in range(Q_ROWS):
                for m in range(LOCAL_ROWS // 2):
                    a, b = int(tile_idx[p, j, 2 * m]), int(tile_idx[p, j, 2 * m + 1])
                    o_ref[p, GRID_W * j:GRID_W * (j + 1), LANES * m:LANES * (m + 1)] = jnp.where(
                        left, cols_ref[a], cols_ref[b])

    return pl.pallas_call(
        build,
        grid=(DEPTH, N_HEADS_GRP),
        in_specs=[pl.BlockSpec((None, None, n_dy + 1, GRID_W, LANES), lambda l, h: (l, h, 0, 0, 0))],
        out_specs=pl.BlockSpec((None, 3, None, TQ, LOCAL_ROWS * GRID_W), lambda l, h: (l, 0, h, 0, 0)),
        out_shape=jax.ShapeDtypeStruct((DEPTH, 3, N_HEADS_GRP, TQ, LOCAL_ROWS * GRID_W), F32),
        compiler_params=_params(2),
        name="local_bias",
    )(cols)


def _w_in_prep_kernel(w_ref, o_ref):
    rows = w_ref.shape[1]
    kpe_end = W_KPE + MLA_ROPE
    o_ref[:, :W_KPE] = w_ref[:W_KPE, :].T.astype(BF16)
    kpe = jnp.concatenate([jnp.zeros((KPE_LANE, rows), F32), w_ref[W_KPE:kpe_end, :],
                           jnp.zeros((LANES - KPE_LANE - MLA_ROPE, rows), F32)], axis=0)
    o_ref[:, W_KPE:W_CZ] = kpe.T.astype(BF16)
    o_ref[:, W_CZ:] = w_ref[kpe_end:, :].T.astype(BF16)


def _w_in_prep(w_in):
    d_in = w_in.shape[-1]
    rows = 256
    return pl.pallas_call(
        _w_in_prep_kernel,
        grid=(DEPTH, D_MODEL // rows),
        in_specs=[pl.BlockSpec((None, d_in, rows), lambda l, r: (l, 0, r))],
        out_specs=pl.BlockSpec((None, rows, D_IN_P), lambda l, r: (l, r, 0)),
        out_shape=jax.ShapeDtypeStruct((DEPTH, D_MODEL, D_IN_P), BF16),
        compiler_params=_params(2),
        name="w_in_prep",
    )(jnp.swapaxes(w_in, 1, 2))


def _rope_tables(seq):
    t = jnp.arange(seq)
    rows = (t // GRID_W).astype(F32)
    cols = (t % GRID_W).astype(F32)
    half = DA_SUB // 2
    inv = 1.0 / (ROPE_BASE ** (jnp.arange(0, half, 2, dtype=F32) / half))
    ar = rows[:, None] * inv
    ac = cols[:, None] * inv
    ang = jnp.concatenate([ar, ar, ac, ac], axis=-1)
    cos, sin = jnp.cos(ang), jnp.sin(ang)
    first = (np.arange(DA_SUB) % 16 < 8)[None, :]
    sin_neg = jnp.where(first, -sin, 0.0)
    sin_pos = jnp.where(first, 0.0, sin)
    tile = lambda a: jnp.tile(a, (1, LANES // DA_SUB))

    def pad(a, fill):
        return jnp.concatenate([jnp.full((seq, KPE_LANE), fill, F32), a,
                                jnp.full((seq, LANES - KPE_LANE - MLA_ROPE), fill, F32)], axis=1)

    return ([tile(cos), tile(sin_neg), tile(sin_pos)],
            [pad(cos, 1.0), pad(sin_neg, 0.0), pad(sin_pos, 0.0)])


def _pad_heads(w, width, take):
    d, k, _ = w.shape
    w = w.reshape(d, k, N_HEADS_GRP, width)[..., :take]
    return jnp.pad(w, ((0, 0), (0, 0), (0, 0), (0, LANES - take))).reshape(d, k, N_HEADS_GRP * LANES)


def kernel(x_prompt, x_sample, cache_a_k, cache_a_v, cache_c_kv, cache_c_kpe, cache_d_k, cache_d_v,
           c, c_ctx, ada_w, ada_b, norm_g, w_in, da_lambda, da_subln_g, conv_w,
           mla_q_norm_g, mla_w_uq, mla_kv_norm_g, mla_w_ukv, na_rpb, w_out, final_norm_g):
    batch, seq, _ = x_prompt.shape
    dec_batch, dec_seq, _ = x_sample.shape
    past = cache_a_k.shape[2]
    assert dec_seq % TQ == 0 and dec_seq // TQ >= LOCAL_CHUNKS and seq % LANES == 0

    w_in_p = _w_in_prep(w_in)
    w_out_b = w_out.astype(BF16)
    wuq = _pad_heads(mla_w_uq, MLA_NOPE + MLA_ROPE, MLA_NOPE + MLA_ROPE).astype(BF16)
    wuk = _pad_heads(mla_w_ukv, MLA_NOPE + MLA_V, MLA_NOPE).astype(BF16)
    wuv = mla_w_ukv.reshape(DEPTH, MLA_KV_RANK, N_HEADS_GRP, MLA_NOPE + MLA_V)[..., MLA_NOPE:]
    wuv = wuv.reshape(DEPTH, MLA_KV_RANK, GROUP_W).astype(BF16)
    subln_g2 = jnp.tile(da_subln_g, (1, LANES // HEAD_DIM)).reshape(DEPTH, 1, LANES)
    ng = norm_g.reshape(DEPTH, 1, D_MODEL)
    qng = mla_q_norm_g.reshape(DEPTH, 1, MLA_Q_RANK)
    kvng = mla_kv_norm_g.reshape(DEPTH, 1, MLA_KV_RANK)
    fg = final_norm_g.reshape(1, D_MODEL)
    tables_a, tables_c = _rope_tables(dec_seq)
    tables = tables_a + tables_c

    cvec8 = jnp.concatenate([c_ctx[None], c, jnp.zeros((8 - 1 - dec_batch, D_MODEL), F32)], axis=0)
    mod = _ada_call(cvec8, ada_w, ada_b).reshape(DEPTH, 8, 3, D_MODEL)

    feat = lambda a: a.transpose(0, 1, 3, 4, 2).reshape(dec_batch, DEPTH, GROUP_W, past)
    cache = _cache_call(feat(cache_a_k), feat(cache_a_v), cache_c_kv,
                        cache_c_kpe.transpose(0, 1, 3, 2), feat(cache_d_k), feat(cache_d_v), wuk, wuv)
    bias = _local_bias_tables(na_rpb, dec_seq // GRID_W)

    xp = x_prompt.reshape(batch * seq, D_MODEL)
    xs = x_sample.reshape(dec_batch * dec_seq, D_MODEL)
    states = []
    for l in range(DEPTH):
        lam_init = 0.8 - 0.6 * math.exp(-0.3 * l)
        final = l == DEPTH - 1

        tok, sq, loc, cv, *states = _in_call(xp, mod, ng, w_in_p, qng, wuq, kvng, wuk, wuv, None,
                                             layer=l, seq=seq, tm=seq, mod_row0=0, states=states,
                                             name=f"ctx_in_{l}")
        xp = _attn_call(xp, mod, tok, sq, loc, cv, None, None, da_lambda, subln_g2, conv_w, w_out_b,
                        fg, layer=l, lam_init=lam_init, nb=batch, seq=seq, tq=seq, mod_row0=0,
                        final=final, name=f"ctx_attn_{l}")

        tok, sq, loc, cv = _in_call(xs, mod, ng, w_in_p, qng, wuq, kvng, wuk, wuv, tables,
                                    layer=l, seq=dec_seq, tm=512, mod_row0=1, states=None,
                                    name=f"lat_in_{l}")
        xs = _attn_call(xs, mod, tok, sq, loc, cv, cache, bias, da_lambda, subln_g2, conv_w, w_out_b,
                        fg, layer=l, lam_init=lam_init, nb=dec_batch, seq=dec_seq, tq=TQ, mod_row0=1,
                        final=final, name=f"lat_attn_{l}")

    def heads(a):
        return a.reshape(batch, DEPTH, N_HEADS_GRP, HEAD_DIM, seq).transpose(0, 1, 4, 2, 3)

    s_ak, s_av, s_ckv, s_kpe, s_dk, s_dv = states
    return (xp.reshape(batch, seq, D_MODEL), xs.reshape(dec_batch, dec_seq, D_MODEL),
            heads(s_ak), heads(s_av), s_ckv, s_kpe.transpose(0, 1, 3, 2), heads(s_dk), heads(s_dv))
```

```python
import functools
import math

import jax
import jax.numpy as jnp
import numpy as np
from jax import lax
from jax.experimental import pallas as pl
from jax.experimental.pallas import tpu as pltpu

F32 = jnp.float32
BF16 = jnp.bfloat16

D_MODEL = 1024
DEPTH = 4
GRID_W = 64
HEAD_DIM = 64
GROUP_W = 256
N_HEADS_GRP = 4
DA_SUB = 32
MLA_Q_RANK = 256
MLA_KV_RANK = 128
MLA_NOPE = 64
MLA_ROPE = 32
MLA_V = 64
LOG2E = math.log2(math.e)
MLA_SCALE = (MLA_NOPE + MLA_ROPE) ** -0.5
DA_SCALE = DA_SUB ** -0.5
NA_SCALE = HEAD_DIM ** -0.5
NA_WIN_H = 8
NA_WIN_W = 16
ROPE_BASE = 10000.0
EPS = 1e-6
NEG_INF = -1e30

LANES = 128
N_PAIRS = GROUP_W // LANES
VMEM_LIMIT = 56 * 1024 * 1024

W_AQ, W_AK, W_AV, W_AZ = 0, 256, 512, 768
W_BB, W_BC, W_BH, W_BZ = 1024, 1280, 1536, 1792
W_CQ, W_CKV, W_KPE, W_CZ = 2048, 2304, 2432, 2560
W_DQ, W_DK, W_DV, W_DZ = 2816, 3072, 3328, 3584
D_IN_P = 3840
KPE_LANE = 64

T_QA, T_SZA, T_QC, T_SCZ, T_QD, T_SDZ, TOK_W = 0, 128, 256, 512, 640, 768, 896
S_KA, S_VA, S_KC, S_VC, SEQ_W = 0, 128, 256, 512, 640
L_KD, L_VD, LOC_W = 0, 128, 256
X_KA, X_VA, X_KC, X_VC, X_KD, X_VD, CACHE_W = 0, 128, 256, 512, 640, 768, 896
B_BB, B_G, B_SBZ, CONV_W = 0, 256, 512, 768

Q_ROWS = 4
TQ = Q_ROWS * GRID_W
LOCAL_CHUNKS = 3
LOCAL_ROWS = LOCAL_CHUNKS * Q_ROWS


def _dot(a, b):
    return jnp.dot(a, b, preferred_element_type=F32)


def _dot_nt(a, b):
    return lax.dot_general(a, b, (((1,), (1,)), ((), ())), preferred_element_type=F32)


def _silu(z):
    return z * (1.0 / (1.0 + jnp.exp(-z)))


def _rms(x, g):
    return x * lax.rsqrt(jnp.mean(x * x, axis=-1, keepdims=True) + EPS) * g


def _rope128(x, cos, sin_neg, sin_pos):
    return x * cos + pltpu.roll(x, LANES - 8, 1) * sin_neg + pltpu.roll(x, 8, 1) * sin_pos


def _scores(job):
    q, keys, biases, _, _ = job
    blocks = []
    for k, b in zip(keys, biases):
        s = _dot_nt(q, k)
        blocks.append(s if b is None else s + b)
    return blocks


def _weighted_values(job, blocks):
    _, _, _, values, sum_lane = job
    m = functools.reduce(jnp.maximum, [jnp.max(s, axis=-1, keepdims=True) for s in blocks])
    one_hot = _lane_mask(sum_lane, 1)
    o = functools.reduce(jnp.add, [
        _dot(jnp.exp2(s - m).astype(BF16), jnp.where(one_hot, jnp.ones_like(v), v))
        for s, v in zip(blocks, values)])
    l = jnp.sum(jnp.where(one_hot, o, 0.0), axis=-1, keepdims=True)
    return o / l


def _attend_all(jobs, after):
    outs = []
    nxt = _scores(jobs[0])
    for k, job in enumerate(jobs):
        cur = nxt
        if k + 1 < len(jobs):
            nxt = _scores(jobs[k + 1])
        outs.append(_weighted_values(job, cur))
        if k in after:
            after[k](outs)
    return outs


def _lane_mask(lo, width):
    lane = lax.broadcasted_iota(jnp.int32, (1, LANES), 1)
    return jnp.logical_and(lane >= lo, lane < lo + width)


def _params(n_axes):
    return pltpu.CompilerParams(dimension_semantics=("arbitrary",) * n_axes,
                                vmem_limit_bytes=VMEM_LIMIT)


def _mod_map(layer, mod_row0, nblk):
    if mod_row0 == 0:
        return lambda i: (layer, 0, 0, 0)
    return lambda i: (layer, mod_row0 + i // nblk, 0, 0)


def _layer_spec(shape, layer):
    zeros = (0,) * len(shape)
    return pl.BlockSpec((None,) + tuple(shape), lambda *_: (layer,) + zeros)


def _ada_kernel(c_ref, w_ref, b_ref, o_ref):
    s = _silu(c_ref[...]).astype(BF16)
    o_ref[...] = _dot(s, w_ref[...].astype(BF16)) + b_ref[...]


def _ada_call(cvec8, ada_w, ada_b):
    nj = 3
    return pl.pallas_call(
        _ada_kernel,
        grid=(DEPTH, nj),
        in_specs=[pl.BlockSpec((8, D_MODEL), lambda l, j: (0, 0)),
                  pl.BlockSpec((None, D_MODEL, D_MODEL), lambda l, j: (l, 0, j)),
                  pl.BlockSpec((None, None, 1, D_MODEL), lambda l, j: (l, j, 0, 0))],
        out_specs=pl.BlockSpec((None, 8, D_MODEL), lambda l, j: (l, 0, j)),
        out_shape=jax.ShapeDtypeStruct((DEPTH, 8, 3 * D_MODEL), F32),
        compiler_params=_params(2),
        name="adaln",
    )(cvec8, ada_w, ada_b.reshape(DEPTH, nj, 1, D_MODEL))


def _cache_kernel(ak_ref, av_ref, ckv_ref, kpe_ref, dk_ref, dv_ref, wuk_ref, wuv_ref, o_ref):
    ckv = ckv_ref[...].astype(BF16)
    kn = _dot(ckv, wuk_ref[...])
    vc = _dot(ckv, wuv_ref[...])
    past = ckv.shape[0]
    kpe = jnp.concatenate([jnp.zeros((KPE_LANE, past), F32), kpe_ref[...],
                           jnp.zeros((LANES - KPE_LANE - MLA_ROPE, past), F32)], axis=0).T
    for hp in range(N_PAIRS):
        half = slice(LANES * hp, LANES * (hp + 1))
        o_ref[hp, :, X_KA:X_KA + LANES] = ak_ref[half, :].T.astype(BF16)
        o_ref[hp, :, X_VA:X_VA + LANES] = av_ref[half, :].T.astype(BF16)
        for e in range(2):
            sl = slice(LANES * (2 * hp + e), LANES * (2 * hp + e + 1))
            o_ref[hp, :, X_KC + LANES * e:X_KC + LANES * (e + 1)] = (kn[:, sl] + kpe).astype(BF16)
        o_ref[hp, :, X_VC:X_VC + LANES] = vc[:, half].astype(BF16)
        o_ref[hp, :, X_KD:X_KD + LANES] = dk_ref[half, :].T.astype(BF16)
        o_ref[hp, :, X_VD:X_VD + LANES] = dv_ref[half, :].T.astype(BF16)


def _cache_call(ca_k, ca_v, c_kv, c_kpe, cd_k, cd_v, wuk, wuv):
    nb, _, past, _ = c_kv.shape
    cache = lambda w: pl.BlockSpec((None, None, past, w), lambda l, b: (b, l, 0, 0))
    feat = lambda rows: pl.BlockSpec((None, None, rows, past), lambda l, b: (b, l, 0, 0))
    return pl.pallas_call(
        _cache_kernel,
        grid=(DEPTH, nb),
        in_specs=[feat(GROUP_W), feat(GROUP_W), cache(MLA_KV_RANK), feat(MLA_ROPE),
                  feat(GROUP_W), feat(GROUP_W),
                  pl.BlockSpec((None, MLA_KV_RANK, 4 * LANES), lambda l, b: (l, 0, 0)),
                  pl.BlockSpec((None, MLA_KV_RANK, GROUP_W), lambda l, b: (l, 0, 0))],
        out_specs=pl.BlockSpec((None, None, N_PAIRS, past, CACHE_W), lambda l, b: (b, l, 0, 0, 0)),
        out_shape=jax.ShapeDtypeStruct((nb, DEPTH, N_PAIRS, past, CACHE_W), BF16),
        compiler_params=_params(2),
        name="cache_prep",
    )(ca_k, ca_v, c_kv, c_kpe, cd_k, cd_v, wuk, wuv)


def _in_kernel(*refs, rope, states):
    it = iter(refs)
    x_ref, mod_ref, ng_ref, w_ref, qng_ref, wuq_ref, kvng_ref, wuk_ref, wuv_ref = (
        next(it) for _ in range(9))
    if rope:
        ta = [next(it)[...] for _ in range(3)]
        tc = [next(it)[...] for _ in range(3)]
    if states == "update":
        for _ in range(6):
            next(it)
    tok_ref, seq_ref, loc_ref, conv_ref = (next(it) for _ in range(4))
    if states:
        sak_ref, sav_ref, sckv_ref, skpe_ref, sdk_ref, sdv_ref = (next(it) for _ in range(6))

    x = x_ref[...]
    h = _rms(x, ng_ref[...]) * (1.0 + mod_ref[1:2, :]) + mod_ref[0:1, :]
    hb = h.astype(BF16)

    def seg(off, n=GROUP_W):
        return _dot(hb, w_ref[:, off:off + n])

    def put_state(ref, val):
        if states == "create":
            ref[0] = val
            for later in range(1, DEPTH):
                ref[later] = jnp.zeros_like(val)
        else:
            ref[...] = val

    def put(ref, width, off, val):
        for hp in range(N_PAIRS):
            ref[:, width * hp + off:width * hp + off + LANES] = (
                val[:, LANES * hp:LANES * (hp + 1)].astype(BF16))

    cqn = _rms(seg(W_CQ), qng_ref[...]).astype(BF16)
    ckvn = _rms(seg(W_CKV, MLA_KV_RANK), kvng_ref[...])
    kpe = seg(W_KPE, LANES)
    if states:
        put_state(sckv_ref, ckvn)
        put_state(skpe_ref, kpe.T[KPE_LANE:KPE_LANE + MLA_ROPE, :])
    ckvb = ckvn.astype(BF16)

    aq, ak, av = seg(W_AQ), seg(W_AK), seg(W_AV)
    if states:
        put_state(sak_ref, ak.T)
        put_state(sav_ref, av.T)
    if rope:
        aq = jnp.concatenate([_rope128(aq[:, :LANES], *ta), _rope128(aq[:, LANES:], *ta)], axis=1)
        ak = jnp.concatenate([_rope128(ak[:, :LANES], *ta), _rope128(ak[:, LANES:], *ta)], axis=1)
    put(tok_ref, TOK_W, T_QA, aq * (DA_SCALE * LOG2E))
    put(seq_ref, SEQ_W, S_KA, ak)
    put(seq_ref, SEQ_W, S_VA, av)
    put(tok_ref, TOK_W, T_SZA, _silu(seg(W_AZ)))

    conv_ref[:, B_BB:B_BB + GROUP_W] = seg(W_BB).astype(BF16)
    conv_ref[:, B_G:B_G + GROUP_W] = (seg(W_BC) * seg(W_BH)).astype(BF16)
    conv_ref[:, B_SBZ:B_SBZ + GROUP_W] = _silu(seg(W_BZ)).astype(BF16)

    dk, dv = seg(W_DK), seg(W_DV)
    if states:
        put_state(sdk_ref, dk.T)
        put_state(sdv_ref, dv.T)
    put(tok_ref, TOK_W, T_QD, seg(W_DQ) * (NA_SCALE * LOG2E))
    put(loc_ref, LOC_W, L_KD, dk)
    put(loc_ref, LOC_W, L_VD, dv)
    put(tok_ref, TOK_W, T_SDZ, _silu(seg(W_DZ)))
    put(tok_ref, TOK_W, T_SCZ, _silu(seg(W_CZ)))

    q = _dot(cqn, wuq_ref[...])
    kn = _dot(ckvb, wuk_ref[...])
    vc = _dot(ckvb, wuv_ref[...])
    kpe_r = _rope128(kpe, *tc) if rope else kpe
    for hd in range(N_HEADS_GRP):
        sl = slice(LANES * hd, LANES * (hd + 1))
        hp, e = divmod(hd, 2)
        qh = q[:, sl]
        if rope:
            qh = _rope128(qh, *tc)
        o = LANES * e
        tok_ref[:, TOK_W * hp + T_QC + o:TOK_W * hp + T_QC + o + LANES] = (
            qh * (MLA_SCALE * LOG2E)).astype(BF16)
        seq_ref[:, SEQ_W * hp + S_KC + o:SEQ_W * hp + S_KC + o + LANES] = (
            kn[:, sl] + kpe_r).astype(BF16)
    put(seq_ref, SEQ_W, S_VC, vc)


def _state_layout(batch, seq, tm):
    nblk = seq // tm
    lblk = lambda layer: DEPTH if layer is None else None
    lidx = lambda layer: 0 if layer is None else layer
    feat = lambda rows: ((batch, DEPTH, rows, seq),
                         lambda layer: (None, lblk(layer), rows, tm),
                         lambda layer: lambda i: (i // nblk, lidx(layer), 0, i % nblk))
    tokm = ((batch, DEPTH, seq, MLA_KV_RANK),
            lambda layer: (None, lblk(layer), tm, MLA_KV_RANK),
            lambda layer: lambda i: (i // nblk, lidx(layer), i % nblk, 0))
    return [feat(GROUP_W), feat(GROUP_W), tokm, feat(MLA_ROPE), feat(GROUP_W), feat(GROUP_W)]


def _in_call(x, mod, norm_g, w_in, q_norm_g, wuq, kv_norm_g, wuk, wuv, tables, *, layer, seq, tm,
             mod_row0, states, name):
    t = x.shape[0]
    nblk = seq // tm
    rope = tables is not None
    in_specs = [
        pl.BlockSpec((tm, D_MODEL), lambda i: (i, 0)),
        pl.BlockSpec((None, None, 3, D_MODEL), _mod_map(layer, mod_row0, nblk)),
        _layer_spec((1, D_MODEL), layer),
        _layer_spec((D_MODEL, D_IN_P), layer),
        _layer_spec((1, MLA_Q_RANK), layer),
        _layer_spec((MLA_Q_RANK, 4 * LANES), layer),
        _layer_spec((1, MLA_KV_RANK), layer),
        _layer_spec((MLA_KV_RANK, 4 * LANES), layer),
        _layer_spec((MLA_KV_RANK, GROUP_W), layer),
    ]
    args = [x, mod, norm_g, w_in, q_norm_g, wuq, kv_norm_g, wuk, wuv]
    if rope:
        in_specs += [pl.BlockSpec((tm, LANES), lambda i: (i % nblk, 0))] * 6
        args += list(tables)
    widths = [N_PAIRS * TOK_W, N_PAIRS * SEQ_W, N_PAIRS * LOC_W, CONV_W]
    out_specs = [pl.BlockSpec((tm, w), lambda i: (i, 0)) for w in widths]
    out_shape = [jax.ShapeDtypeStruct((t, w), BF16) for w in widths]
    aliases, mode = {}, False
    if states is not None:
        mode = "update" if states else "create"
        assert states or layer == 0
        at = layer if states else None
        for k, (shape, block, index) in enumerate(_state_layout(t // seq, seq, tm)):
            out_specs.append(pl.BlockSpec(block(at), index(at)))
            out_shape.append(jax.ShapeDtypeStruct(shape, F32))
            if states:
                aliases[len(args)] = len(widths) + k
                in_specs.append(pl.BlockSpec(memory_space=pl.ANY))
                args.append(states[k])
    return pl.pallas_call(
        functools.partial(_in_kernel, rope=rope, states=mode),
        grid=(t // tm,),
        in_specs=in_specs,
        out_specs=out_specs,
        out_shape=out_shape,
        input_output_aliases=aliases,
        compiler_params=_params(1),
        name=name,
    )(*args)


def _attn_kernel(*refs, cached, lam_init, final):
    it = iter(refs)
    tok_ref, seq_ref = next(it), next(it)
    loc_refs = [next(it) for _ in range(LOCAL_CHUNKS if cached else 1)]
    if cached:
        x_ref, bias_ref = next(it), next(it)
    lam_ref, g_ref = next(it), next(it)
    res_ref, mod_ref, bb_ref, gc_ref, gp_ref, gn_ref, sbz_ref, cw_ref, w_ref, fg_ref = (
        next(it) for _ in range(10))
    o_ref = next(it)

    lv = lam_ref[...]
    lam = (jnp.exp(jnp.sum(lv[0:1] * lv[1:2], keepdims=True))
           - jnp.exp(jnp.sum(lv[2:3] * lv[3:4], keepdims=True)) + lam_init)

    tq = res_ref.shape[0]
    i, nq = pl.program_id(1), pl.num_programs(1)
    has_prev = jnp.where(i != 0, 1.0, 0.0)
    has_next = jnp.where(i != nq - 1, 1.0, 0.0)
    gc = gc_ref[...].astype(F32)
    rows = lax.broadcasted_iota(jnp.int32, (tq, 1), 0)
    halo = gp_ref.shape[0]
    g_prev = jnp.where(rows == 0, gp_ref[halo - 1:halo, :].astype(F32) * has_prev,
                       pltpu.roll(gc, 1, 0))
    g_next = jnp.where(rows == tq - 1, gn_ref[0:1, :].astype(F32) * has_next,
                       pltpu.roll(gc, tq - 1, 0))
    cw = cw_ref[...]
    conv = g_prev * cw[0:1] + gc * cw[1:2] + g_next * cw[2:3]
    yb = (bb_ref[...].astype(F32) * conv * sbz_ref[...].astype(F32)).astype(BF16)

    first = _lane_mask(0, HEAD_DIM)
    proj = []

    def gate(hp, off):
        return tok_ref[:, TOK_W * hp + off:TOK_W * hp + off + LANES].astype(F32)

    def project(group, halves):
        y = jnp.concatenate(halves, axis=1).astype(BF16)
        proj.append(_dot(y, w_ref[GROUP_W * group:GROUP_W * (group + 1), :]))

    def after_a(outs):
        proj.append(_dot(yb, w_ref[GROUP_W:2 * GROUP_W, :]))
        halves = []
        for hp in range(N_PAIRS):
            m00, m01, m10, m11 = outs[4 * hp:4 * hp + 4]
            o = jnp.where(first, m00 - lam * m01, m10 - lam * m11)
            o2 = o * o
            ss0 = jnp.sum(jnp.where(first, o2, 0.0), axis=-1, keepdims=True)
            ss1 = jnp.sum(jnp.where(first, 0.0, o2), axis=-1, keepdims=True)
            ms = jnp.where(first, ss0, ss1) * (1.0 / HEAD_DIM)
            y = o * lax.rsqrt(ms + EPS) * g_ref[...] * (1.0 - lam_init)
            halves.append(y * gate(hp, T_SZA))
        project(0, halves)

    def after_group(group, base, off):
        def run(outs):
            project(group, [jnp.where(first, outs[base + 2 * hp], outs[base + 2 * hp + 1])
                            * gate(hp, off) for hp in range(N_PAIRS)])
        return run

    jobs_a, jobs_c, jobs_d = [], [], []
    for hp in range(N_PAIRS):
        def col(ref, width, off, w=LANES, hp=hp):
            return ref[:, width * hp + off:width * hp + off + w]

        xcol = lambda off, hp=hp: x_ref[hp, :, off:off + LANES]
        qa = col(tok_ref, TOK_W, T_QA)
        keys = [col(seq_ref, SEQ_W, S_KA)] + ([xcol(X_KA)] if cached else [])
        vals = [col(seq_ref, SEQ_W, S_VA)] + ([xcol(X_VA)] if cached else [])
        nob = [None] * len(keys)
        for e in range(2):
            for c in range(2):
                qm = jnp.where(_lane_mask(HEAD_DIM * e + DA_SUB * c, DA_SUB), qa, jnp.zeros_like(qa))
                jobs_a.append((qm, keys, nob, vals, HEAD_DIM * (1 - e)))
        vals = [col(seq_ref, SEQ_W, S_VC)] + ([xcol(X_VC)] if cached else [])
        for e in range(2):
            keys = [col(seq_ref, SEQ_W, S_KC + LANES * e)] + (
                [xcol(X_KC + LANES * e)] if cached else [])
            jobs_c.append((col(tok_ref, TOK_W, T_QC + LANES * e), keys, nob, vals,
                           HEAD_DIM * (1 - e)))
        qd = col(tok_ref, TOK_W, T_QD)
        keys = [col(r, LOC_W, L_KD) for r in loc_refs] + ([xcol(X_KD)] if cached else [])
        vals = [col(r, LOC_W, L_VD) for r in loc_refs] + ([xcol(X_VD)] if cached else [])
        for e in range(2):
            qm = jnp.where(_lane_mask(HEAD_DIM * e, HEAD_DIM), qd, jnp.zeros_like(qd))
            biases = [None] * len(keys)
            if cached:
                biases = [bias_ref[2 * hp + e, :, TQ * j:TQ * (j + 1)]
                          for j in range(len(loc_refs))] + [None]
            jobs_d.append((qm, keys, biases, vals, HEAD_DIM * (1 - e)))

    n_a, n = len(jobs_a), 2 * N_PAIRS
    after = {n_a - 1: after_a, n_a + n - 1: after_group(2, n_a, T_SCZ),
             n_a + 2 * n - 1: after_group(3, n_a + n, T_SDZ)}
    _attend_all(jobs_a + jobs_c + jobs_d, after)
    xn = res_ref[...] + mod_ref[2:3, :] * functools.reduce(jnp.add, proj)
    if final:
        xn = _rms(xn, fg_ref[...])
    o_ref[...] = xn


def _attn_call(x, mod, u_tok, u_seq, u_loc, u_conv, cache, bias, lam_vecs, subln_g2, conv_w, w_out,
               final_g, *, layer, lam_init, nb, seq, tq, mod_row0, final, name):
    cached = cache is not None
    nq = seq // tq
    in_specs = [pl.BlockSpec((tq, N_PAIRS * TOK_W), lambda b, i: (b * nq + i, 0)),
                pl.BlockSpec((seq, N_PAIRS * SEQ_W), lambda b, i: (b, 0))]
    args = [u_tok, u_seq]
    if cached:
        past = cache.shape[3]
        base = lambda i: jnp.clip(i - 1, 0, nq - LOCAL_CHUNKS)
        for j in range(LOCAL_CHUNKS):
            in_specs.append(pl.BlockSpec(
                (tq, N_PAIRS * LOC_W), lambda b, i, j=j: (b * nq + base(i) + j, 0)))
            args.append(u_loc)

        def bias_map(b, i):
            pattern = jnp.where(i == 0, 0, jnp.where(i == nq - 1, 2, 1))
            return (layer, pattern, 0, 0, 0)

        in_specs += [pl.BlockSpec((None, None, N_PAIRS, past, CACHE_W),
                                  lambda b, i: (b, layer, 0, 0, 0)),
                     pl.BlockSpec((None, None, N_HEADS_GRP, tq, LOCAL_CHUNKS * tq), bias_map)]
        args += [cache, bias]
    else:
        in_specs.append(pl.BlockSpec((seq, N_PAIRS * LOC_W), lambda b, i: (b, 0)))
        args.append(u_loc)
    in_specs += [_layer_spec((4, DA_SUB), layer), _layer_spec((1, LANES), layer)]
    args += [lam_vecs, subln_g2]

    halo = 16
    hb = tq // halo
    last = nb * seq // halo - 1
    blk = lambda b, i: b * nq + i
    conv = lambda off: pl.BlockSpec((tq, GROUP_W), lambda b, i: (blk(b, i), off // GROUP_W))
    mod_row = (lambda b: 0) if mod_row0 == 0 else (lambda b: mod_row0 + b)
    in_specs += [
        pl.BlockSpec((tq, D_MODEL), lambda b, i: (blk(b, i), 0)),
        pl.BlockSpec((None, None, 3, D_MODEL), lambda b, i: (layer, mod_row(b), 0, 0)),
        conv(B_BB), conv(B_G),
        pl.BlockSpec((halo, GROUP_W),
                     lambda b, i: (jnp.maximum(blk(b, i) * hb - 1, 0), B_G // GROUP_W)),
        pl.BlockSpec((halo, GROUP_W),
                     lambda b, i: (jnp.minimum((blk(b, i) + 1) * hb, last), B_G // GROUP_W)),
        conv(B_SBZ),
        _layer_spec((3, GROUP_W), layer),
        _layer_spec((D_MODEL, D_MODEL), layer),
        pl.BlockSpec((1, D_MODEL), lambda b, i: (0, 0)),
    ]
    args += [x, mod, u_conv, u_conv, u_conv, u_conv, u_conv, conv_w, w_out, final_g]
    return pl.pallas_call(
        functools.partial(_attn_kernel, cached=cached, lam_init=lam_init, final=final),
        grid=(nb, nq),
        in_specs=in_specs,
        out_specs=pl.BlockSpec((tq, D_MODEL), lambda b, i: (blk(b, i), 0)),
        out_shape=jax.ShapeDtypeStruct((nb * seq, D_MODEL), F32),
        compiler_params=_params(2),
        name=name,
    )(*args)


def _local_bias_tables(rpb, rows):
    nq = rows // Q_ROWS
    n_dy, n_dx = 2 * NA_WIN_H - 1, 2 * NA_WIN_W - 1
    qc = np.arange(GRID_W)[:, None]
    kc = np.arange(GRID_W)[None, :]
    ws = np.clip(qc - NA_WIN_W // 2, 0, GRID_W - NA_WIN_W)
    col_ok = (kc >= ws) & (kc < ws + NA_WIN_W)
    oh_dx = ((kc - qc + NA_WIN_W - 1)[None] == np.arange(n_dx)[:, None, None]) & col_ok[None]
    tile_idx = []
    for i in (0, 1, nq - 1):
        row0 = Q_ROWS * int(np.clip(i - 1, 0, nq - LOCAL_CHUNKS))
        r = (Q_ROWS * i + np.arange(Q_ROWS))[:, None]
        kr = (row0 + np.arange(LOCAL_ROWS))[None, :]
        rs = np.clip(r - NA_WIN_H // 2, 0, rows - NA_WIN_H)
        ok = (kr >= rs) & (kr < rs + NA_WIN_H)
        tile_idx.append(np.where(ok, kr - r + NA_WIN_H - 1, n_dy))
    tile_idx = np.stack(tile_idx)
    hi = lax.Precision.HIGHEST
    cols = jnp.einsum("lhyd,dqc->lhyqc", rpb.astype(F32), jnp.asarray(oh_dx, F32), precision=hi)
    cols = jnp.where(col_ok, cols * LOG2E, NEG_INF)
    cols = jnp.concatenate(
        [cols, jnp.full((DEPTH, N_HEADS_GRP, 1, GRID_W, GRID_W), NEG_INF, F32)], axis=2)
    cols = jnp.concatenate([cols, cols], axis=-1)

    def build(cols_ref, o_ref):
        left = _lane_mask(0, GRID_W)
        for p in range(3):
            for j in range(Q_ROWS):
                for m in range(LOCAL_ROWS // 2):
                    a, b = int(tile_idx[p, j, 2 * m]), int(tile_idx[p, j, 2 * m + 1])
                    o_ref[p, GRID_W * j:GRID_W * (j + 1), LANES * m:LANES * (m + 1)] = jnp.where(
                        left, cols_ref[a], cols_ref[b])

    return pl.pallas_call(
        build,
        grid=(DEPTH, N_HEADS_GRP),
        in_specs=[pl.BlockSpec((None, None, n_dy + 1, GRID_W, LANES), lambda l, h: (l, h, 0, 0, 0))],
        out_specs=pl.BlockSpec((None, 3, None, TQ, LOCAL_ROWS * GRID_W), lambda l, h: (l, 0, h, 0, 0)),
        out_shape=jax.ShapeDtypeStruct((DEPTH, 3, N_HEADS_GRP, TQ, LOCAL_ROWS * GRID_W), F32),
        compiler_params=_params(2),
        name="local_bias",
    )(cols)


def _w_in_prep_kernel(w_ref, o_ref):
    rows = w_ref.shape[1]
    kpe_end = W_KPE + MLA_ROPE
    o_ref[:, :W_KPE] = w_ref[:W_KPE, :].T.astype(BF16)
    kpe = jnp.concatenate([jnp.zeros((KPE_LANE, rows), F32), w_ref[W_KPE:kpe_end, :],
                           jnp.zeros((LANES - KPE_LANE - MLA_ROPE, rows), F32)], axis=0)
    o_ref[:, W_KPE:W_CZ] = kpe.T.astype(BF16)
    o_ref[:, W_CZ:] = w_ref[kpe_end:, :].T.astype(BF16)


def _w_in_prep(w_in):
    d_in = w_in.shape[-1]
    rows = 256
    return pl.pallas_call(
        _w_in_prep_kernel,
        grid=(DEPTH, D_MODEL // rows),
        in_specs=[pl.BlockSpec((None, d_in, rows), lambda l, r: (l, 0, r))],
        out_specs=pl.BlockSpec((None, rows, D_IN_P), lambda l, r: (l, r, 0)),
        out_shape=jax.ShapeDtypeStruct((DEPTH, D_MODEL, D_IN_P), BF16),
        compiler_params=_params(2),
        name="w_in_prep",
    )(jnp.swapaxes(w_in, 1, 2))


def _rope_tables(seq):
    t = jnp.arange(seq)
    rows = (t // GRID_W).astype(F32)
    cols = (t % GRID_W).astype(F32)
    half = DA_SUB // 2
    inv = 1.0 / (ROPE_BASE ** (jnp.arange(0, half, 2, dtype=F32) / half))
    ar = rows[:, None] * inv
    ac = cols[:, None] * inv
    ang = jnp.concatenate([ar, ar, ac, ac], axis=-1)
    cos, sin = jnp.cos(ang), jnp.sin(ang)
    first = (np.arange(DA_SUB) % 16 < 8)[None, :]
    sin_neg = jnp.where(first, -sin, 0.0)
    sin_pos = jnp.where(first, 0.0, sin)
    tile = lambda a: jnp.tile(a, (1, LANES // DA_SUB))

    def pad(a, fill):
        return jnp.concatenate([jnp.full((seq, KPE_LANE), fill, F32), a,
                                jnp.full((seq, LANES - KPE_LANE - MLA_ROPE), fill, F32)], axis=1)

    return ([tile(cos), tile(sin_neg), tile(sin_pos)],
            [pad(cos, 1.0), pad(sin_neg, 0.0), pad(sin_pos, 0.0)])


def _pad_heads(w, width, take):
    d, k, _ = w.shape
    w = w.reshape(d, k, N_HEADS_GRP, width)[..., :take]
    return jnp.pad(w, ((0, 0), (0, 0), (0, 0), (0, LANES - take))).reshape(d, k, N_HEADS_GRP * LANES)


def kernel(x_prompt, x_sample, cache_a_k, cache_a_v, cache_c_kv, cache_c_kpe, cache_d_k, cache_d_v,
           c, c_ctx, ada_w, ada_b, norm_g, w_in, da_lambda, da_subln_g, conv_w,
           mla_q_norm_g, mla_w_uq, mla_kv_norm_g, mla_w_ukv, na_rpb, w_out, final_norm_g):
    batch, seq, _ = x_prompt.shape
    dec_batch, dec_seq, _ = x_sample.shape
    past = cache_a_k.shape[2]
    assert dec_seq % TQ == 0 and dec_seq // TQ >= LOCAL_CHUNKS and seq % LANES == 0

    w_in_p = _w_in_prep(w_in)
    w_out_b = w_out.astype(BF16)
    wuq = _pad_heads(mla_w_uq, MLA_NOPE + MLA_ROPE, MLA_NOPE + MLA_ROPE).astype(BF16)
    wuk = _pad_heads(mla_w_ukv, MLA_NOPE + MLA_V, MLA_NOPE).astype(BF16)
    wuv = mla_w_ukv.reshape(DEPTH, MLA_KV_RANK, N_HEADS_GRP, MLA_NOPE + MLA_V)[..., MLA_NOPE:]
    wuv = wuv.reshape(DEPTH, MLA_KV_RANK, GROUP_W).astype(BF16)
    subln_g2 = jnp.tile(da_subln_g, (1, LANES // HEAD_DIM)).reshape(DEPTH, 1, LANES)
    ng = norm_g.reshape(DEPTH, 1, D_MODEL)
    qng = mla_q_norm_g.reshape(DEPTH, 1, MLA_Q_RANK)
    kvng = mla_kv_norm_g.reshape(DEPTH, 1, MLA_KV_RANK)
    fg = final_norm_g.reshape(1, D_MODEL)
    tables_a, tables_c = _rope_tables(dec_seq)
    tables = tables_a + tables_c

    cvec8 = jnp.concatenate([c_ctx[None], c, jnp.zeros((8 - 1 - dec_batch, D_MODEL), F32)], axis=0)
    mod = _ada_call(cvec8, ada_w, ada_b).reshape(DEPTH, 8, 3, D_MODEL)

    feat = lambda a: a.transpose(0, 1, 3, 4, 2).reshape(dec_batch, DEPTH, GROUP_W, past)
    cache = _cache_call(feat(cache_a_k), feat(cache_a_v), cache_c_kv,
                        cache_c_kpe.transpose(0, 1, 3, 2), feat(cache_d_k), feat(cache_d_v), wuk, wuv)
    bias = _local_bias_tables(na_rpb, dec_seq // GRID_W)

    xp = x_prompt.reshape(batch * seq, D_MODEL)
    xs = x_sample.reshape(dec_batch * dec_seq, D_MODEL)
    states = []
    for l in range(DEPTH):
        lam_init = 0.8 - 0.6 * math.exp(-0.3 * l)
        final = l == DEPTH - 1

        tok, sq, loc, cv, *states = _in_call(xp, mod, ng, w_in_p, qng, wuq, kvng, wuk, wuv, None,
                                             layer=l, seq=seq, tm=seq, mod_row0=0, states=states,
                                             name=f"ctx_in_{l}")
        xp = _attn_call(xp, mod, tok, sq, loc, cv, None, None, da_lambda, subln_g2, conv_w, w_out_b,
                        fg, layer=l, lam_init=lam_init, nb=batch, seq=seq, tq=seq, mod_row0=0,
                        final=final, name=f"ctx_attn_{l}")

        tok, sq, loc, cv = _in_call(xs, mod, ng, w_in_p, qng, wuq, kvng, wuk, wuv, tables,
                                    layer=l, seq=dec_seq, tm=512, mod_row0=1, states=None,
                                    name=f"lat_in_{l}")
        xs = _attn_call(xs, mod, tok, sq, loc, cv, cache, bias, da_lambda, subln_g2, conv_w, w_out_b,
                        fg, layer=l, lam_init=lam_init, nb=dec_batch, seq=dec_seq, tq=TQ, mod_row0=1,
                        final=final, name=f"lat_attn_{l}")

    def heads(a):
        return a.reshape(batch, DEPTH, N_HEADS_GRP, HEAD_DIM, seq).transpose(0, 1, 4, 2, 3)

    s_ak, s_av, s_ckv, s_kpe, s_dk, s_dv = states
    return (xp.reshape(batch, seq, D_MODEL), xs.reshape(dec_batch, dec_seq, D_MODEL),
            heads(s_ak), heads(s_av), s_ckv, s_kpe.transpose(0, 1, 3, 2), heads(s_dk), heads(s_dv))
```

```python
import functools
import math

import jax
import jax.numpy as jnp
import numpy as np
from jax import lax
from jax.experimental import pallas as pl
from jax.experimental.pallas import tpu as pltpu

F32 = jnp.float32
BF16 = jnp.bfloat16

D_MODEL = 1024
DEPTH = 4
GRID_W = 64
HEAD_DIM = 64
GROUP_W = 256
N_HEADS_GRP = 4
DA_SUB = 32
MLA_Q_RANK = 256
MLA_KV_RANK = 128
MLA_NOPE = 64
MLA_ROPE = 32
MLA_V = 64
LOG2E = math.log2(math.e)
MLA_SCALE = (MLA_NOPE + MLA_ROPE) ** -0.5
DA_SCALE = DA_SUB ** -0.5
NA_SCALE = HEAD_DIM ** -0.5
NA_WIN_H = 8
NA_WIN_W = 16
ROPE_BASE = 10000.0
EPS = 1e-6
NEG_INF = -1e30

LANES = 128
N_PAIRS = GROUP_W // LANES
VMEM_LIMIT = 56 * 1024 * 1024

W_AQ, W_AK, W_AV, W_AZ = 0, 256, 512, 768
W_BB, W_BC, W_BH, W_BZ = 1024, 1280, 1536, 1792
W_CQ, W_CKV, W_KPE, W_CZ = 2048, 2304, 2432, 2560
W_DQ, W_DK, W_DV, W_DZ = 2816, 3072, 3328, 3584
D_IN_P = 3840
KPE_LANE = 64

T_QA, T_SZA, T_QC, T_SCZ, T_QD, T_SDZ, TOK_W = 0, 128, 256, 512, 640, 768, 896
S_KA, S_VA, S_KC, S_VC, SEQ_W = 0, 128, 256, 512, 640
L_KD, L_VD, LOC_W = 0, 128, 256
X_KA, X_VA, X_KC, X_VC, X_KD, X_VD, CACHE_W = 0, 128, 256, 512, 640, 768, 896
B_BB, B_G, B_SBZ, CONV_W = 0, 256, 512, 768

Q_ROWS = 4
TQ = Q_ROWS * GRID_W
LOCAL_CHUNKS = 3
LOCAL_ROWS = LOCAL_CHUNKS * Q_ROWS


def _dot(a, b):
    return jnp.dot(a, b, preferred_element_type=F32)


def _dot_nt(a, b):
    return lax.dot_general(a, b, (((1,), (1,)), ((), ())), preferred_element_type=F32)


def _silu(z):
    return z * (1.0 / (1.0 + jnp.exp(-z)))


def _rms(x, g):
    return x * lax.rsqrt(jnp.mean(x * x, axis=-1, keepdims=True) + EPS) * g


def _rope128(x, cos, sin_neg, sin_pos):
    return x * cos + pltpu.roll(x, LANES - 8, 1) * sin_neg + pltpu.roll(x, 8, 1) * sin_pos


def _scores(job):
    q, keys, biases, _, _ = job
    blocks = []
    for k, b in zip(keys, biases):
        s = _dot_nt(q, k)
        blocks.append(s if b is None else s + b)
    return blocks


def _weighted_values(job, blocks):
    _, _, _, values, sum_lane = job
    m = functools.reduce(jnp.maximum, [jnp.max(s, axis=-1, keepdims=True) for s in blocks])
    one_hot = _lane_mask(sum_lane, 1)
    o = functools.reduce(jnp.add, [
        _dot(jnp.exp2(s - m).astype(BF16), jnp.where(one_hot, jnp.ones_like(v), v))
        for s, v in zip(blocks, values)])
    l = jnp.sum(jnp.where(one_hot, o, 0.0), axis=-1, keepdims=True)
    return o / l


SCORES_IN_FLIGHT_BYTES = 4 * 1024 * 1024


def _attend_all(jobs, after):
    q, keys = jobs[0][0], jobs[0][1]
    score_bytes = 4 * q.shape[0] * sum(k.shape[0] for k in keys)
    ahead = max(1, min(len(jobs), SCORES_IN_FLIGHT_BYTES // score_bytes))
    outs = []
    pending = [_scores(job) for job in jobs[:ahead]]
    for k, job in enumerate(jobs):
        if k + ahead < len(jobs):
            pending.append(_scores(jobs[k + ahead]))
        outs.append(_weighted_values(job, pending.pop(0)))
        if k in after:
            after[k](outs)
    return outs


def _lane_mask(lo, width):
    lane = lax.broadcasted_iota(jnp.int32, (1, LANES), 1)
    return jnp.logical_and(lane >= lo, lane < lo + width)


def _params(n_axes):
    return pltpu.CompilerParams(dimension_semantics=("arbitrary",) * n_axes,
                                vmem_limit_bytes=VMEM_LIMIT)


def _mod_map(layer, mod_row0, nblk):
    if mod_row0 == 0:
        return lambda i: (layer, 0, 0, 0)
    return lambda i: (layer, mod_row0 + i // nblk, 0, 0)


def _layer_spec(shape, layer):
    zeros = (0,) * len(shape)
    return pl.BlockSpec((None,) + tuple(shape), lambda *_: (layer,) + zeros)


def _ada_kernel(c_ref, w_ref, b_ref, o_ref):
    s = _silu(c_ref[...]).astype(BF16)
    o_ref[...] = _dot(s, w_ref[...].astype(BF16)) + b_ref[...]


def _ada_call(cvec8, ada_w, ada_b):
    nj = 3
    return pl.pallas_call(
        _ada_kernel,
        grid=(DEPTH, nj),
        in_specs=[pl.BlockSpec((8, D_MODEL), lambda l, j: (0, 0)),
                  pl.BlockSpec((None, D_MODEL, D_MODEL), lambda l, j: (l, 0, j)),
                  pl.BlockSpec((None, None, 1, D_MODEL), lambda l, j: (l, j, 0, 0))],
        out_specs=pl.BlockSpec((None, 8, D_MODEL), lambda l, j: (l, 0, j)),
        out_shape=jax.ShapeDtypeStruct((DEPTH, 8, 3 * D_MODEL), F32),
        compiler_params=_params(2),
        name="adaln",
    )(cvec8, ada_w, ada_b.reshape(DEPTH, nj, 1, D_MODEL))


def _cache_kernel(ak_ref, av_ref, ckv_ref, kpe_ref, dk_ref, dv_ref, wuk_ref, wuv_ref, o_ref):
    ckv = ckv_ref[...].astype(BF16)
    kn = _dot(ckv, wuk_ref[...])
    vc = _dot(ckv, wuv_ref[...])
    past = ckv.shape[0]
    kpe = jnp.concatenate([jnp.zeros((KPE_LANE, past), F32), kpe_ref[...],
                           jnp.zeros((LANES - KPE_LANE - MLA_ROPE, past), F32)], axis=0).T
    for hp in range(N_PAIRS):
        half = slice(LANES * hp, LANES * (hp + 1))
        o_ref[hp, :, X_KA:X_KA + LANES] = ak_ref[half, :].T.astype(BF16)
        o_ref[hp, :, X_VA:X_VA + LANES] = av_ref[half, :].T.astype(BF16)
        for e in range(2):
            sl = slice(LANES * (2 * hp + e), LANES * (2 * hp + e + 1))
            o_ref[hp, :, X_KC + LANES * e:X_KC + LANES * (e + 1)] = (kn[:, sl] + kpe).astype(BF16)
        o_ref[hp, :, X_VC:X_VC + LANES] = vc[:, half].astype(BF16)
        o_ref[hp, :, X_KD:X_KD + LANES] = dk_ref[half, :].T.astype(BF16)
        o_ref[hp, :, X_VD:X_VD + LANES] = dv_ref[half, :].T.astype(BF16)


def _cache_call(ca_k, ca_v, c_kv, c_kpe, cd_k, cd_v, wuk, wuv):
    nb, _, past, _ = c_kv.shape
    cache = lambda w: pl.BlockSpec((None, None, past, w), lambda l, b: (b, l, 0, 0))
    feat = lambda rows: pl.BlockSpec((None, None, rows, past), lambda l, b: (b, l, 0, 0))
    return pl.pallas_call(
        _cache_kernel,
        grid=(DEPTH, nb),
        in_specs=[feat(GROUP_W), feat(GROUP_W), cache(MLA_KV_RANK), feat(MLA_ROPE),
                  feat(GROUP_W), feat(GROUP_W),
                  pl.BlockSpec((None, MLA_KV_RANK, 4 * LANES), lambda l, b: (l, 0, 0)),
                  pl.BlockSpec((None, MLA_KV_RANK, GROUP_W), lambda l, b: (l, 0, 0))],
        out_specs=pl.BlockSpec((None, None, N_PAIRS, past, CACHE_W), lambda l, b: (b, l, 0, 0, 0)),
        out_shape=jax.ShapeDtypeStruct((nb, DEPTH, N_PAIRS, past, CACHE_W), BF16),
        compiler_params=_params(2),
        name="cache_prep",
    )(ca_k, ca_v, c_kv, c_kpe, cd_k, cd_v, wuk, wuv)


def _in_kernel(*refs, rope, states):
    it = iter(refs)
    x_ref, mod_ref, ng_ref, w_ref, qng_ref, wuq_ref, kvng_ref, wuk_ref, wuv_ref = (
        next(it) for _ in range(9))
    if rope:
        ta = [next(it)[...] for _ in range(3)]
        tc = [next(it)[...] for _ in range(3)]
    if states == "update":
        for _ in range(6):
            next(it)
    tok_ref, seq_ref, loc_ref, conv_ref = (next(it) for _ in range(4))
    if states:
        sak_ref, sav_ref, sckv_ref, skpe_ref, sdk_ref, sdv_ref = (next(it) for _ in range(6))

    x = x_ref[...]
    h = _rms(x, ng_ref[...]) * (1.0 + mod_ref[1:2, :]) + mod_ref[0:1, :]
    hb = h.astype(BF16)

    def seg(off, n=GROUP_W):
        return _dot(hb, w_ref[:, off:off + n])

    def put_state(ref, val):
        if states == "create":
            ref[0] = val
            for later in range(1, DEPTH):
                ref[later] = jnp.zeros_like(val)
        else:
            ref[...] = val

    def put(ref, width, off, val):
        for hp in range(N_PAIRS):
            ref[:, width * hp + off:width * hp + off + LANES] = (
                val[:, LANES * hp:LANES * (hp + 1)].astype(BF16))

    cqn = _rms(seg(W_CQ), qng_ref[...]).astype(BF16)
    ckvn = _rms(seg(W_CKV, MLA_KV_RANK), kvng_ref[...])
    kpe = seg(W_KPE, LANES)
    if states:
        put_state(sckv_ref, ckvn)
        put_state(skpe_ref, kpe.T[KPE_LANE:KPE_LANE + MLA_ROPE, :])
    ckvb = ckvn.astype(BF16)

    aq, ak, av = seg(W_AQ), seg(W_AK), seg(W_AV)
    if states:
        put_state(sak_ref, ak.T)
        put_state(sav_ref, av.T)
    if rope:
        aq = jnp.concatenate([_rope128(aq[:, :LANES], *ta), _rope128(aq[:, LANES:], *ta)], axis=1)
        ak = jnp.concatenate([_rope128(ak[:, :LANES], *ta), _rope128(ak[:, LANES:], *ta)], axis=1)
    put(tok_ref, TOK_W, T_QA, aq * (DA_SCALE * LOG2E))
    put(seq_ref, SEQ_W, S_KA, ak)
    put(seq_ref, SEQ_W, S_VA, av)
    put(tok_ref, TOK_W, T_SZA, _silu(seg(W_AZ)))

    conv_ref[:, B_BB:B_BB + GROUP_W] = seg(W_BB).astype(BF16)
    conv_ref[:, B_G:B_G + GROUP_W] = (seg(W_BC) * seg(W_BH)).astype(BF16)
    conv_ref[:, B_SBZ:B_SBZ + GROUP_W] = _silu(seg(W_BZ)).astype(BF16)

    dk, dv = seg(W_DK), seg(W_DV)
    if states:
        put_state(sdk_ref, dk.T)
        put_state(sdv_ref, dv.T)
    put(tok_ref, TOK_W, T_QD, seg(W_DQ) * (NA_SCALE * LOG2E))
    put(loc_ref, LOC_W, L_KD, dk)
    put(loc_ref, LOC_W, L_VD, dv)
    put(tok_ref, TOK_W, T_SDZ, _silu(seg(W_DZ)))
    put(tok_ref, TOK_W, T_SCZ, _silu(seg(W_CZ)))

    q = _dot(cqn, wuq_ref[...])
    kn = _dot(ckvb, wuk_ref[...])
    vc = _dot(ckvb, wuv_ref[...])
    kpe_r = _rope128(kpe, *tc) if rope else kpe
    for hd in range(N_HEADS_GRP):
        sl = slice(LANES * hd, LANES * (hd + 1))
        hp, e = divmod(hd, 2)
        qh = q[:, sl]
        if rope:
            qh = _rope128(qh, *tc)
        o = LANES * e
        tok_ref[:, TOK_W * hp + T_QC + o:TOK_W * hp + T_QC + o + LANES] = (
            qh * (MLA_SCALE * LOG2E)).astype(BF16)
        seq_ref[:, SEQ_W * hp + S_KC + o:SEQ_W * hp + S_KC + o + LANES] = (
            kn[:, sl] + kpe_r).astype(BF16)
    put(seq_ref, SEQ_W, S_VC, vc)


def _state_layout(batch, seq, tm):
    nblk = seq // tm
    lblk = lambda layer: DEPTH if layer is None else None
    lidx = lambda layer: 0 if layer is None else layer
    feat = lambda rows: ((batch, DEPTH, rows, seq),
                         lambda layer: (None, lblk(layer), rows, tm),
                         lambda layer: lambda i: (i // nblk, lidx(layer), 0, i % nblk))
    tokm = ((batch, DEPTH, seq, MLA_KV_RANK),
            lambda layer: (None, lblk(layer), tm, MLA_KV_RANK),
            lambda layer: lambda i: (i // nblk, lidx(layer), i % nblk, 0))
    return [feat(GROUP_W), feat(GROUP_W), tokm, feat(MLA_ROPE), feat(GROUP_W), feat(GROUP_W)]


def _in_call(x, mod, norm_g, w_in, q_norm_g, wuq, kv_norm_g, wuk, wuv, tables, *, layer, seq, tm,
             mod_row0, states, name):
    t = x.shape[0]
    nblk = seq // tm
    rope = tables is not None
    in_specs = [
        pl.BlockSpec((tm, D_MODEL), lambda i: (i, 0)),
        pl.BlockSpec((None, None, 3, D_MODEL), _mod_map(layer, mod_row0, nblk)),
        _layer_spec((1, D_MODEL), layer),
        _layer_spec((D_MODEL, D_IN_P), layer),
        _layer_spec((1, MLA_Q_RANK), layer),
        _layer_spec((MLA_Q_RANK, 4 * LANES), layer),
        _layer_spec((1, MLA_KV_RANK), layer),
        _layer_spec((MLA_KV_RANK, 4 * LANES), layer),
        _layer_spec((MLA_KV_RANK, GROUP_W), layer),
    ]
    args = [x, mod, norm_g, w_in, q_norm_g, wuq, kv_norm_g, wuk, wuv]
    if rope:
        in_specs += [pl.BlockSpec((tm, LANES), lambda i: (i % nblk, 0))] * 6
        args += list(tables)
    widths = [N_PAIRS * TOK_W, N_PAIRS * SEQ_W, N_PAIRS * LOC_W, CONV_W]
    out_specs = [pl.BlockSpec((tm, w), lambda i: (i, 0)) for w in widths]
    out_shape = [jax.ShapeDtypeStruct((t, w), BF16) for w in widths]
    aliases, mode = {}, False
    if states is not None:
        mode = "update" if states else "create"
        assert states or layer == 0
        at = layer if states else None
        for k, (shape, block, index) in enumerate(_state_layout(t // seq, seq, tm)):
            out_specs.append(pl.BlockSpec(block(at), index(at)))
            out_shape.append(jax.ShapeDtypeStruct(shape, F32))
            if states:
                aliases[len(args)] = len(widths) + k
                in_specs.append(pl.BlockSpec(memory_space=pl.ANY))
                args.append(states[k])
    return pl.pallas_call(
        functools.partial(_in_kernel, rope=rope, states=mode),
        grid=(t // tm,),
        in_specs=in_specs,
        out_specs=out_specs,
        out_shape=out_shape,
        input_output_aliases=aliases,
        compiler_params=_params(1),
        name=name,
    )(*args)


def _attn_kernel(*refs, cached, lam_init, final):
    it = iter(refs)
    tok_ref, seq_ref = next(it), next(it)
    loc_refs = [next(it) for _ in range(LOCAL_CHUNKS if cached else 1)]
    if cached:
        x_ref, bias_ref = next(it), next(it)
    lam_ref, g_ref = next(it), next(it)
    res_ref, mod_ref, bb_ref, gc_ref, gp_ref, gn_ref, sbz_ref, cw_ref, w_ref, fg_ref = (
        next(it) for _ in range(10))
    o_ref = next(it)

    lv = lam_ref[...]
    lam = (jnp.exp(jnp.sum(lv[0:1] * lv[1:2], keepdims=True))
           - jnp.exp(jnp.sum(lv[2:3] * lv[3:4], keepdims=True)) + lam_init)

    tq = res_ref.shape[0]
    i, nq = pl.program_id(1), pl.num_programs(1)
    has_prev = jnp.where(i != 0, 1.0, 0.0)
    has_next = jnp.where(i != nq - 1, 1.0, 0.0)
    gc = gc_ref[...].astype(F32)
    rows = lax.broadcasted_iota(jnp.int32, (tq, 1), 0)
    halo = gp_ref.shape[0]
    g_prev = jnp.where(rows == 0, gp_ref[halo - 1:halo, :].astype(F32) * has_prev,
                       pltpu.roll(gc, 1, 0))
    g_next = jnp.where(rows == tq - 1, gn_ref[0:1, :].astype(F32) * has_next,
                       pltpu.roll(gc, tq - 1, 0))
    cw = cw_ref[...]
    conv = g_prev * cw[0:1] + gc * cw[1:2] + g_next * cw[2:3]
    yb = (bb_ref[...].astype(F32) * conv * sbz_ref[...].astype(F32)).astype(BF16)

    first = _lane_mask(0, HEAD_DIM)
    proj = []

    def gate(hp, off):
        return tok_ref[:, TOK_W * hp + off:TOK_W * hp + off + LANES].astype(F32)

    def project(group, halves):
        y = jnp.concatenate(halves, axis=1).astype(BF16)
        proj.append(_dot(y, w_ref[GROUP_W * group:GROUP_W * (group + 1), :]))

    def after_a(outs):
        proj.append(_dot(yb, w_ref[GROUP_W:2 * GROUP_W, :]))
        halves = []
        for hp in range(N_PAIRS):
            m00, m01, m10, m11 = outs[4 * hp:4 * hp + 4]
            o = jnp.where(first, m00 - lam * m01, m10 - lam * m11)
            o2 = o * o
            ss0 = jnp.sum(jnp.where(first, o2, 0.0), axis=-1, keepdims=True)
            ss1 = jnp.sum(jnp.where(first, 0.0, o2), axis=-1, keepdims=True)
            ms = jnp.where(first, ss0, ss1) * (1.0 / HEAD_DIM)
            y = o * lax.rsqrt(ms + EPS) * g_ref[...] * (1.0 - lam_init)
            halves.append(y * gate(hp, T_SZA))
        project(0, halves)

    def after_group(group, base, off):
        def run(outs):
            project(group, [jnp.where(first, outs[base + 2 * hp], outs[base + 2 * hp + 1])
                            * gate(hp, off) for hp in range(N_PAIRS)])
        return run

    jobs_a, jobs_c, jobs_d = [], [], []
    for hp in range(N_PAIRS):
        def col(ref, width, off, w=LANES, hp=hp):
            return ref[:, width * hp + off:width * hp + off + w]

        xcol = lambda off, hp=hp: x_ref[hp, :, off:off + LANES]
        qa = col(tok_ref, TOK_W, T_QA)
        keys = [col(seq_ref, SEQ_W, S_KA)] + ([xcol(X_KA)] if cached else [])
        vals = [col(seq_ref, SEQ_W, S_VA)] + ([xcol(X_VA)] if cached else [])
        nob = [None] * len(keys)
        for e in range(2):
            for c in range(2):
                qm = jnp.where(_lane_mask(HEAD_DIM * e + DA_SUB * c, DA_SUB), qa, jnp.zeros_like(qa))
                jobs_a.append((qm, keys, nob, vals, HEAD_DIM * (1 - e)))
        vals = [col(seq_ref, SEQ_W, S_VC)] + ([xcol(X_VC)] if cached else [])
        for e in range(2):
            keys = [col(seq_ref, SEQ_W, S_KC + LANES * e)] + (
                [xcol(X_KC + LANES * e)] if cached else [])
            jobs_c.append((col(tok_ref, TOK_W, T_QC + LANES * e), keys, nob, vals,
                           HEAD_DIM * (1 - e)))
        qd = col(tok_ref, TOK_W, T_QD)
        keys = [col(r, LOC_W, L_KD) for r in loc_refs] + ([xcol(X_KD)] if cached else [])
        vals = [col(r, LOC_W, L_VD) for r in loc_refs] + ([xcol(X_VD)] if cached else [])
        for e in range(2):
            qm = jnp.where(_lane_mask(HEAD_DIM * e, HEAD_DIM), qd, jnp.zeros_like(qd))
            biases = [None] * len(keys)
            if cached:
                biases = [bias_ref[2 * hp + e, :, TQ * j:TQ * (j + 1)]
                          for j in range(len(loc_refs))] + [None]
            jobs_d.append((qm, keys, biases, vals, HEAD_DIM * (1 - e)))

    n_a, n = len(jobs_a), 2 * N_PAIRS
    after = {n_a - 1: after_a, n_a + n - 1: after_group(2, n_a, T_SCZ),
             n_a + 2 * n - 1: after_group(3, n_a + n, T_SDZ)}
    _attend_all(jobs_a + jobs_c + jobs_d, after)
    xn = res_ref[...] + mod_ref[2:3, :] * functools.reduce(jnp.add, proj)
    if final:
        xn = _rms(xn, fg_ref[...])
    o_ref[...] = xn


def _attn_call(x, mod, u_tok, u_seq, u_loc, u_conv, cache, bias, lam_vecs, subln_g2, conv_w, w_out,
               final_g, *, layer, lam_init, nb, seq, tq, mod_row0, final, name):
    cached = cache is not None
    nq = seq // tq
    in_specs = [pl.BlockSpec((tq, N_PAIRS * TOK_W), lambda b, i: (b * nq + i, 0)),
                pl.BlockSpec((seq, N_PAIRS * SEQ_W), lambda b, i: (b, 0))]
    args = [u_tok, u_seq]
    if cached:
        past = cache.shape[3]
        base = lambda i: jnp.clip(i - 1, 0, nq - LOCAL_CHUNKS)
        for j in range(LOCAL_CHUNKS):
            in_specs.append(pl.BlockSpec(
                (tq, N_PAIRS * LOC_W), lambda b, i, j=j: (b * nq + base(i) + j, 0)))
            args.append(u_loc)

        def bias_map(b, i):
            pattern = jnp.where(i == 0, 0, jnp.where(i == nq - 1, 2, 1))
            return (layer, pattern, 0, 0, 0)

        in_specs += [pl.BlockSpec((None, None, N_PAIRS, past, CACHE_W),
                                  lambda b, i: (b, layer, 0, 0, 0)),
                     pl.BlockSpec((None, None, N_HEADS_GRP, tq, LOCAL_CHUNKS * tq), bias_map)]
        args += [cache, bias]
    else:
        in_specs.append(pl.BlockSpec((seq, N_PAIRS * LOC_W), lambda b, i: (b, 0)))
        args.append(u_loc)
    in_specs += [_layer_spec((4, DA_SUB), layer), _layer_spec((1, LANES), layer)]
    args += [lam_vecs, subln_g2]

    halo = 16
    hb = tq // halo
    last = nb * seq // halo - 1
    blk = lambda b, i: b * nq + i
    conv = lambda off: pl.BlockSpec((tq, GROUP_W), lambda b, i: (blk(b, i), off // GROUP_W))
    mod_row = (lambda b: 0) if mod_row0 == 0 else (lambda b: mod_row0 + b)
    in_specs += [
        pl.BlockSpec((tq, D_MODEL), lambda b, i: (blk(b, i), 0)),
        pl.BlockSpec((None, None, 3, D_MODEL), lambda b, i: (layer, mod_row(b), 0, 0)),
        conv(B_BB), conv(B_G),
        pl.BlockSpec((halo, GROUP_W),
                     lambda b, i: (jnp.maximum(blk(b, i) * hb - 1, 0), B_G // GROUP_W)),
        pl.BlockSpec((halo, GROUP_W),
                     lambda b, i: (jnp.minimum((blk(b, i) + 1) * hb, last), B_G // GROUP_W)),
        conv(B_SBZ),
        _layer_spec((3, GROUP_W), layer),
        _layer_spec((D_MODEL, D_MODEL), layer),
        pl.BlockSpec((1, D_MODEL), lambda b, i: (0, 0)),
    ]
    args += [x, mod, u_conv, u_conv, u_conv, u_conv, u_conv, conv_w, w_out, final_g]
    return pl.pallas_call(
        functools.partial(_attn_kernel, cached=cached, lam_init=lam_init, final=final),
        grid=(nb, nq),
        in_specs=in_specs,
        out_specs=pl.BlockSpec((tq, D_MODEL), lambda b, i: (blk(b, i), 0)),
        out_shape=jax.ShapeDtypeStruct((nb * seq, D_MODEL), F32),
        compiler_params=_params(2),
        name=name,
    )(*args)


def _local_bias_tables(rpb, rows):
    nq = rows // Q_ROWS
    n_dy, n_dx = 2 * NA_WIN_H - 1, 2 * NA_WIN_W - 1
    qc = np.arange(GRID_W)[:, None]
    kc = np.arange(GRID_W)[None, :]
    ws = np.clip(qc - NA_WIN_W // 2, 0, GRID_W - NA_WIN_W)
    col_ok = (kc >= ws) & (kc < ws + NA_WIN_W)
    oh_dx = ((kc - qc + NA_WIN_W - 1)[None] == np.arange(n_dx)[:, None, None]) & col_ok[None]
    tile_idx = []
    for i in (0, 1, nq - 1):
        row0 = Q_ROWS * int(np.clip(i - 1, 0, nq - LOCAL_CHUNKS))
        r = (Q_ROWS * i + np.arange(Q_ROWS))[:, None]
        kr = (row0 + np.arange(LOCAL_ROWS))[None, :]
        rs = np.clip(r - NA_WIN_H // 2, 0, rows - NA_WIN_H)
        ok = (kr >= rs) & (kr < rs + NA_WIN_H)
        tile_idx.append(np.where(ok, kr - r + NA_WIN_H - 1, n_dy))
    tile_idx = np.stack(tile_idx)
    hi = lax.Precision.HIGHEST
    cols = jnp.einsum("lhyd,dqc->lhyqc", rpb.astype(F32), jnp.asarray(oh_dx, F32), precision=hi)
    cols = jnp.where(col_ok, cols * LOG2E, NEG_INF)
    cols = jnp.concatenate(
        [cols, jnp.full((DEPTH, N_HEADS_GRP, 1, GRID_W, GRID_W), NEG_INF, F32)], axis=2)
    cols = jnp.concatenate([cols, cols], axis=-1)

    def build(cols_ref, o_ref):
        left = _lane_mask(0, GRID_W)
        for p in range(3):
            for j in range(Q_ROWS):
                for m in range(LOCAL_ROWS // 2):
                    a, b = int(tile_idx[p, j, 2 * m]), int(tile_idx[p, j, 2 * m + 1])
                    o_ref[p, GRID_W * j:GRID_W * (j + 1), LANES * m:LANES * (m + 1)] = jnp.where(
                        left, cols_ref[a], cols_ref[b])

    return pl.pallas_call(
        build,
        grid=(DEPTH, N_HEADS_GRP),
        in_specs=[pl.BlockSpec((None, None, n_dy + 1, GRID_W, LANES), lambda l, h: (l, h, 0, 0, 0))],
        out_specs=pl.BlockSpec((None, 3, None, TQ, LOCAL_ROWS * GRID_W), lambda l, h: (l, 0, h, 0, 0)),
        out_shape=jax.ShapeDtypeStruct((DEPTH, 3, N_HEADS_GRP, TQ, LOCAL_ROWS * GRID_W), F32),
        compiler_params=_params(2),
        name="local_bias",
    )(cols)


def _w_in_prep_kernel(w_ref, o_ref):
    rows = w_ref.shape[1]
    kpe_end = W_KPE + MLA_ROPE
    o_ref[:, :W_KPE] = w_ref[:W_KPE, :].T.astype(BF16)
    kpe = jnp.concatenate([jnp.zeros((KPE_LANE, rows), F32), w_ref[W_KPE:kpe_end, :],
                           jnp.zeros((LANES - KPE_LANE - MLA_ROPE, rows), F32)], axis=0)
    o_ref[:, W_KPE:W_CZ] = kpe.T.astype(BF16)
    o_ref[:, W_CZ:] = w_ref[kpe_end:, :].T.astype(BF16)


def _w_in_prep(w_in):
    d_in = w_in.shape[-1]
    rows = 256
    return pl.pallas_call(
        _w_in_prep_kernel,
        grid=(DEPTH, D_MODEL // rows),
        in_specs=[pl.BlockSpec((None, d_in, rows), lambda l, r: (l, 0, r))],
        out_specs=pl.BlockSpec((None, rows, D_IN_P), lambda l, r: (l, r, 0)),
        out_shape=jax.ShapeDtypeStruct((DEPTH, D_MODEL, D_IN_P), BF16),
        compiler_params=_params(2),
        name="w_in_prep",
    )(jnp.swapaxes(w_in, 1, 2))


def _rope_tables(seq):
    t = jnp.arange(seq)
    rows = (t // GRID_W).astype(F32)
    cols = (t % GRID_W).astype(F32)
    half = DA_SUB // 2
    inv = 1.0 / (ROPE_BASE ** (jnp.arange(0, half, 2, dtype=F32) / half))
    ar = rows[:, None] * inv
    ac = cols[:, None] * inv
    ang = jnp.concatenate([ar, ar, ac, ac], axis=-1)
    cos, sin = jnp.cos(ang), jnp.sin(ang)
    first = (np.arange(DA_SUB) % 16 < 8)[None, :]
    sin_neg = jnp.where(first, -sin, 0.0)
    sin_pos = jnp.where(first, 0.0, sin)
    tile = lambda a: jnp.tile(a, (1, LANES // DA_SUB))

    def pad(a, fill):
        return jnp.concatenate([jnp.full((seq, KPE_LANE), fill, F32), a,
                                jnp.full((seq, LANES - KPE_LANE - MLA_ROPE), fill, F32)], axis=1)

    return ([tile(cos), tile(sin_neg), tile(sin_pos)],
            [pad(cos, 1.0), pad(sin_neg, 0.0), pad(sin_pos, 0.0)])


def _pad_heads(w, width, take):
    d, k, _ = w.shape
    w = w.reshape(d, k, N_HEADS_GRP, width)[..., :take]
    return jnp.pad(w, ((0, 0), (0, 0), (0, 0), (0, LANES - take))).reshape(d, k, N_HEADS_GRP * LANES)


def kernel(x_prompt, x_sample, cache_a_k, cache_a_v, cache_c_kv, cache_c_kpe, cache_d_k, cache_d_v,
           c, c_ctx, ada_w, ada_b, norm_g, w_in, da_lambda, da_subln_g, conv_w,
           mla_q_norm_g, mla_w_uq, mla_kv_norm_g, mla_w_ukv, na_rpb, w_out, final_norm_g):
    batch, seq, _ = x_prompt.shape
    dec_batch, dec_seq, _ = x_sample.shape
    past = cache_a_k.shape[2]
    assert dec_seq % TQ == 0 and dec_seq // TQ >= LOCAL_CHUNKS and seq % LANES == 0

    w_in_p = _w_in_prep(w_in)
    w_out_b = w_out.astype(BF16)
    wuq = _pad_heads(mla_w_uq, MLA_NOPE + MLA_ROPE, MLA_NOPE + MLA_ROPE).astype(BF16)
    wuk = _pad_heads(mla_w_ukv, MLA_NOPE + MLA_V, MLA_NOPE).astype(BF16)
    wuv = mla_w_ukv.reshape(DEPTH, MLA_KV_RANK, N_HEADS_GRP, MLA_NOPE + MLA_V)[..., MLA_NOPE:]
    wuv = wuv.reshape(DEPTH, MLA_KV_RANK, GROUP_W).astype(BF16)
    subln_g2 = jnp.tile(da_subln_g, (1, LANES // HEAD_DIM)).reshape(DEPTH, 1, LANES)
    ng = norm_g.reshape(DEPTH, 1, D_MODEL)
    qng = mla_q_norm_g.reshape(DEPTH, 1, MLA_Q_RANK)
    kvng = mla_kv_norm_g.reshape(DEPTH, 1, MLA_KV_RANK)
    fg = final_norm_g.reshape(1, D_MODEL)
    tables_a, tables_c = _rope_tables(dec_seq)
    tables = tables_a + tables_c

    cvec8 = jnp.concatenate([c_ctx[None], c, jnp.zeros((8 - 1 - dec_batch, D_MODEL), F32)], axis=0)
    mod = _ada_call(cvec8, ada_w, ada_b).reshape(DEPTH, 8, 3, D_MODEL)

    feat = lambda a: a.transpose(0, 1, 3, 4, 2).reshape(dec_batch, DEPTH, GROUP_W, past)
    cache = _cache_call(feat(cache_a_k), feat(cache_a_v), cache_c_kv,
                        cache_c_kpe.transpose(0, 1, 3, 2), feat(cache_d_k), feat(cache_d_v), wuk, wuv)
    bias = _local_bias_tables(na_rpb, dec_seq // GRID_W)

    xp = x_prompt.reshape(batch * seq, D_MODEL)
    xs = x_sample.reshape(dec_batch * dec_seq, D_MODEL)
    states = []
    for l in range(DEPTH):
        lam_init = 0.8 - 0.6 * math.exp(-0.3 * l)
        final = l == DEPTH - 1

        tok, sq, loc, cv, *states = _in_call(xp, mod, ng, w_in_p, qng, wuq, kvng, wuk, wuv, None,
                                             layer=l, seq=seq, tm=seq, mod_row0=0, states=states,
                                             name=f"ctx_in_{l}")
        xp = _attn_call(xp, mod, tok, sq, loc, cv, None, None, da_lambda, subln_g2, conv_w, w_out_b,
                        fg, layer=l, lam_init=lam_init, nb=batch, seq=seq, tq=seq, mod_row0=0,
                        final=final, name=f"ctx_attn_{l}")

        tok, sq, loc, cv = _in_call(xs, mod, ng, w_in_p, qng, wuq, kvng, wuk, wuv, tables,
                                    layer=l, seq=dec_seq, tm=512, mod_row0=1, states=None,
                                    name=f"lat_in_{l}")
        xs = _attn_call(xs, mod, tok, sq, loc, cv, cache, bias, da_lambda, subln_g2, conv_w, w_out_b,
                        fg, layer=l, lam_init=lam_init, nb=dec_batch, seq=dec_seq, tq=TQ, mod_row0=1,
                        final=final, name=f"lat_attn_{l}")

    def heads(a):
        return a.reshape(batch, DEPTH, N_HEADS_GRP, HEAD_DIM, seq).transpose(0, 1, 4, 2, 3)

    s_ak, s_av, s_ckv, s_kpe, s_dk, s_dv = states
    return (xp.reshape(batch, seq, D_MODEL), xs.reshape(dec_batch, dec_seq, D_MODEL),
            heads(s_ak), heads(s_av), s_ckv, s_kpe.transpose(0, 1, 3, 2), heads(s_dk), heads(s_dv))
```

```python
import functools
import math

import jax
import jax.numpy as jnp
import numpy as np
from jax import lax
from jax.experimental import pallas as pl
from jax.experimental.pallas import tpu as pltpu

F32 = jnp.float32
BF16 = jnp.bfloat16

D_MODEL = 1024
DEPTH = 4
GRID_W = 64
HEAD_DIM = 64
GROUP_W = 256
N_HEADS_GRP = 4
DA_SUB = 32
MLA_Q_RANK = 256
MLA_KV_RANK = 128
MLA_NOPE = 64
MLA_ROPE = 32
MLA_V = 64
LOG2E = math.log2(math.e)
MLA_SCALE = (MLA_NOPE + MLA_ROPE) ** -0.5
DA_SCALE = DA_SUB ** -0.5
NA_SCALE = HEAD_DIM ** -0.5
NA_WIN_H = 8
NA_WIN_W = 16
ROPE_BASE = 10000.0
EPS = 1e-6
NEG_INF = -1e30

LANES = 128
BF16_SUBLANES = 16
N_PAIRS = GROUP_W // LANES
VMEM_LIMIT = 56 * 1024 * 1024
IN_PROJ_ROWS = 512
W_PREP_ROWS = 256

W_AQ, W_AK, W_AV, W_AZ = 0, 256, 512, 768
W_BB, W_BC, W_BH, W_BZ = 1024, 1280, 1536, 1792
W_CQ, W_CKV, W_KPE, W_CZ = 2048, 2304, 2432, 2560
W_DQ, W_DK, W_DV, W_DZ = 2816, 3072, 3328, 3584
D_IN_P = 3840
KPE_LANE = 64

T_QA, T_SZA, T_QC, T_SCZ, T_QD, T_SDZ, TOK_W = 0, 128, 256, 512, 640, 768, 896
S_KA, S_VA, S_KC, S_VC, SEQ_W = 0, 128, 256, 512, 640
L_KD, L_VD, LOC_W = 0, 128, 256
X_KA, X_VA, X_KC, X_VC, X_KD, X_VD, CACHE_W = 0, 128, 256, 512, 640, 768, 896
B_BB, B_G, B_SBZ, CONV_W = 0, 256, 512, 768

Q_ROWS = 4
TQ = Q_ROWS * GRID_W
LOCAL_CHUNKS = 3
LOCAL_ROWS = LOCAL_CHUNKS * Q_ROWS


def _dot(a, b):
    return jnp.dot(a, b, preferred_element_type=F32)


def _dot_nt(a, b):
    return lax.dot_general(a, b, (((1,), (1,)), ((), ())), preferred_element_type=F32)


def _silu(z):
    return z * (1.0 / (1.0 + jnp.exp(-z)))


def _rms(x, g):
    return x * lax.rsqrt(jnp.mean(x * x, axis=-1, keepdims=True) + EPS) * g


def _rope128(x, cos, sin_neg, sin_pos):
    return x * cos + pltpu.roll(x, LANES - 8, 1) * sin_neg + pltpu.roll(x, 8, 1) * sin_pos


def _scores(job):
    q, keys, biases, _, _ = job
    blocks = []
    for k, b in zip(keys, biases):
        s = _dot_nt(q, k)
        blocks.append(s if b is None else s + b)
    return blocks


def _weighted_values(job, blocks):
    _, _, _, values, sum_lane = job
    m = functools.reduce(jnp.maximum, [jnp.max(s, axis=-1, keepdims=True) for s in blocks])
    one_hot = _lane_mask(sum_lane, 1)
    o = functools.reduce(jnp.add, [
        _dot(jnp.exp2(s - m).astype(BF16), jnp.where(one_hot, jnp.ones_like(v), v))
        for s, v in zip(blocks, values)])
    l = jnp.sum(jnp.where(one_hot, o, 0.0), axis=-1, keepdims=True)
    return o / l


SCORES_IN_FLIGHT_BYTES = 6 * 1024 * 1024


def _attend_all(jobs, after):
    q, keys = jobs[0][0], jobs[0][1]
    score_bytes = 4 * q.shape[0] * sum(k.shape[0] for k in keys)
    ahead = max(1, min(len(jobs), SCORES_IN_FLIGHT_BYTES // score_bytes))
    outs = []
    pending = [_scores(job) for job in jobs[:ahead]]
    for k, job in enumerate(jobs):
        if k + ahead < len(jobs):
            pending.append(_scores(jobs[k + ahead]))
        outs.append(_weighted_values(job, pending.pop(0)))
        if k in after:
            after[k](outs)
    return outs


def _lane_mask(lo, width):
    lane = lax.broadcasted_iota(jnp.int32, (1, LANES), 1)
    return jnp.logical_and(lane >= lo, lane < lo + width)


def _params(n_axes):
    return pltpu.CompilerParams(dimension_semantics=("arbitrary",) * n_axes,
                                vmem_limit_bytes=VMEM_LIMIT)


def _mod_map(layer, mod_row0, nblk):
    if mod_row0 == 0:
        return lambda i: (layer, 0, 0, 0)
    return lambda i: (layer, mod_row0 + i // nblk, 0, 0)


def _layer_spec(shape, layer):
    zeros = (0,) * len(shape)
    return pl.BlockSpec((None,) + tuple(shape), lambda *_: (layer,) + zeros)


def _ada_kernel(c_ref, w_ref, b_ref, o_ref):
    s = _silu(c_ref[...]).astype(BF16)
    o_ref[...] = _dot(s, w_ref[...].astype(BF16)) + b_ref[...]


def _ada_call(cvec8, ada_w, ada_b):
    nj = 3
    return pl.pallas_call(
        _ada_kernel,
        grid=(DEPTH, nj),
        in_specs=[pl.BlockSpec((8, D_MODEL), lambda l, j: (0, 0)),
                  pl.BlockSpec((None, D_MODEL, D_MODEL), lambda l, j: (l, 0, j)),
                  pl.BlockSpec((None, None, 1, D_MODEL), lambda l, j: (l, j, 0, 0))],
        out_specs=pl.BlockSpec((None, 8, D_MODEL), lambda l, j: (l, 0, j)),
        out_shape=jax.ShapeDtypeStruct((DEPTH, 8, 3 * D_MODEL), F32),
        compiler_params=_params(2),
        name="adaln",
    )(cvec8, ada_w, ada_b.reshape(DEPTH, nj, 1, D_MODEL))


def _cache_kernel(ak_ref, av_ref, ckv_ref, kpe_ref, dk_ref, dv_ref, wuk_ref, wuv_ref, o_ref):
    ckv = ckv_ref[...].astype(BF16)
    kn = _dot(ckv, wuk_ref[...])
    vc = _dot(ckv, wuv_ref[...])
    past = ckv.shape[0]
    kpe = jnp.concatenate([jnp.zeros((KPE_LANE, past), F32), kpe_ref[...],
                           jnp.zeros((LANES - KPE_LANE - MLA_ROPE, past), F32)], axis=0).T
    for hp in range(N_PAIRS):
        half = slice(LANES * hp, LANES * (hp + 1))
        o_ref[hp, :, X_KA:X_KA + LANES] = ak_ref[half, :].T.astype(BF16)
        o_ref[hp, :, X_VA:X_VA + LANES] = av_ref[half, :].T.astype(BF16)
        for e in range(2):
            sl = slice(LANES * (2 * hp + e), LANES * (2 * hp + e + 1))
            o_ref[hp, :, X_KC + LANES * e:X_KC + LANES * (e + 1)] = (kn[:, sl] + kpe).astype(BF16)
        o_ref[hp, :, X_VC:X_VC + LANES] = vc[:, half].astype(BF16)
        o_ref[hp, :, X_KD:X_KD + LANES] = dk_ref[half, :].T.astype(BF16)
        o_ref[hp, :, X_VD:X_VD + LANES] = dv_ref[half, :].T.astype(BF16)


def _cache_call(ca_k, ca_v, c_kv, c_kpe, cd_k, cd_v, wuk, wuv):
    nb, _, past, _ = c_kv.shape
    cache = lambda w: pl.BlockSpec((None, None, past, w), lambda l, b: (b, l, 0, 0))
    feat = lambda rows: pl.BlockSpec((None, None, rows, past), lambda l, b: (b, l, 0, 0))
    return pl.pallas_call(
        _cache_kernel,
        grid=(DEPTH, nb),
        in_specs=[feat(GROUP_W), feat(GROUP_W), cache(MLA_KV_RANK), feat(MLA_ROPE),
                  feat(GROUP_W), feat(GROUP_W),
                  pl.BlockSpec((None, MLA_KV_RANK, 4 * LANES), lambda l, b: (l, 0, 0)),
                  pl.BlockSpec((None, MLA_KV_RANK, GROUP_W), lambda l, b: (l, 0, 0))],
        out_specs=pl.BlockSpec((None, None, N_PAIRS, past, CACHE_W), lambda l, b: (b, l, 0, 0, 0)),
        out_shape=jax.ShapeDtypeStruct((nb, DEPTH, N_PAIRS, past, CACHE_W), BF16),
        compiler_params=_params(2),
        name="cache_prep",
    )(ca_k, ca_v, c_kv, c_kpe, cd_k, cd_v, wuk, wuv)


def _in_kernel(*refs, rope, states):
    it = iter(refs)
    x_ref, mod_ref, ng_ref, w_ref, qng_ref, wuq_ref, kvng_ref, wuk_ref, wuv_ref = (
        next(it) for _ in range(9))
    if rope:
        ta = [next(it)[...] for _ in range(3)]
        tc = [next(it)[...] for _ in range(3)]
    if states == "update":
        for _ in range(6):
            next(it)
    tok_ref, seq_ref, loc_ref, conv_ref = (next(it) for _ in range(4))
    if states:
        sak_ref, sav_ref, sckv_ref, skpe_ref, sdk_ref, sdv_ref = (next(it) for _ in range(6))

    x = x_ref[...]
    h = _rms(x, ng_ref[...]) * (1.0 + mod_ref[1:2, :]) + mod_ref[0:1, :]
    hb = h.astype(BF16)

    def seg(off, n=GROUP_W):
        return _dot(hb, w_ref[:, off:off + n])

    def put_state(ref, val):
        if states == "create":
            ref[0] = val
            for later in range(1, DEPTH):
                ref[later] = jnp.zeros_like(val)
        else:
            ref[...] = val

    def put(ref, width, off, val):
        for hp in range(N_PAIRS):
            ref[:, width * hp + off:width * hp + off + LANES] = (
                val[:, LANES * hp:LANES * (hp + 1)].astype(BF16))

    cqn = _rms(seg(W_CQ), qng_ref[...]).astype(BF16)
    ckvn = _rms(seg(W_CKV, MLA_KV_RANK), kvng_ref[...])
    kpe = seg(W_KPE, LANES)
    if states:
        put_state(sckv_ref, ckvn)
        put_state(skpe_ref, kpe.T[KPE_LANE:KPE_LANE + MLA_ROPE, :])
    ckvb = ckvn.astype(BF16)

    aq, ak, av = seg(W_AQ), seg(W_AK), seg(W_AV)
    if states:
        put_state(sak_ref, ak.T)
        put_state(sav_ref, av.T)
    if rope:
        aq = jnp.concatenate([_rope128(aq[:, :LANES], *ta), _rope128(aq[:, LANES:], *ta)], axis=1)
        ak = jnp.concatenate([_rope128(ak[:, :LANES], *ta), _rope128(ak[:, LANES:], *ta)], axis=1)
    put(tok_ref, TOK_W, T_QA, aq * (DA_SCALE * LOG2E))
    put(seq_ref, SEQ_W, S_KA, ak)
    put(seq_ref, SEQ_W, S_VA, av)
    put(tok_ref, TOK_W, T_SZA, _silu(seg(W_AZ)))

    conv_ref[:, B_BB:B_BB + GROUP_W] = seg(W_BB).astype(BF16)
    conv_ref[:, B_G:B_G + GROUP_W] = (seg(W_BC) * seg(W_BH)).astype(BF16)
    conv_ref[:, B_SBZ:B_SBZ + GROUP_W] = _silu(seg(W_BZ)).astype(BF16)

    dk, dv = seg(W_DK), seg(W_DV)
    if states:
        put_state(sdk_ref, dk.T)
        put_state(sdv_ref, dv.T)
    put(tok_ref, TOK_W, T_QD, seg(W_DQ) * (NA_SCALE * LOG2E))
    put(loc_ref, LOC_W, L_KD, dk)
    put(loc_ref, LOC_W, L_VD, dv)
    put(tok_ref, TOK_W, T_SDZ, _silu(seg(W_DZ)))
    put(tok_ref, TOK_W, T_SCZ, _silu(seg(W_CZ)))

    q = _dot(cqn, wuq_ref[...])
    kn = _dot(ckvb, wuk_ref[...])
    vc = _dot(ckvb, wuv_ref[...])
    kpe_r = _rope128(kpe, *tc) if rope else kpe
    for hd in range(N_HEADS_GRP):
        sl = slice(LANES * hd, LANES * (hd + 1))
        hp, e = divmod(hd, 2)
        qh = q[:, sl]
        if rope:
            qh = _rope128(qh, *tc)
        o = LANES * e
        tok_ref[:, TOK_W * hp + T_QC + o:TOK_W * hp + T_QC + o + LANES] = (
            qh * (MLA_SCALE * LOG2E)).astype(BF16)
        seq_ref[:, SEQ_W * hp + S_KC + o:SEQ_W * hp + S_KC + o + LANES] = (
            kn[:, sl] + kpe_r).astype(BF16)
    put(seq_ref, SEQ_W, S_VC, vc)


def _state_layout(batch, seq, tm):
    nblk = seq // tm
    lblk = lambda layer: DEPTH if layer is None else None
    lidx = lambda layer: 0 if layer is None else layer
    feat = lambda rows: ((batch, DEPTH, rows, seq),
                         lambda layer: (None, lblk(layer), rows, tm),
                         lambda layer: lambda i: (i // nblk, lidx(layer), 0, i % nblk))
    tokm = ((batch, DEPTH, seq, MLA_KV_RANK),
            lambda layer: (None, lblk(layer), tm, MLA_KV_RANK),
            lambda layer: lambda i: (i // nblk, lidx(layer), i % nblk, 0))
    return [feat(GROUP_W), feat(GROUP_W), tokm, feat(MLA_ROPE), feat(GROUP_W), feat(GROUP_W)]


def _in_call(x, mod, norm_g, w_in, q_norm_g, wuq, kv_norm_g, wuk, wuv, tables, *, layer, seq, tm,
             mod_row0, states, name):
    t = x.shape[0]
    nblk = seq // tm
    rope = tables is not None
    in_specs = [
        pl.BlockSpec((tm, D_MODEL), lambda i: (i, 0)),
        pl.BlockSpec((None, None, 3, D_MODEL), _mod_map(layer, mod_row0, nblk)),
        _layer_spec((1, D_MODEL), layer),
        _layer_spec((D_MODEL, D_IN_P), layer),
        _layer_spec((1, MLA_Q_RANK), layer),
        _layer_spec((MLA_Q_RANK, 4 * LANES), layer),
        _layer_spec((1, MLA_KV_RANK), layer),
        _layer_spec((MLA_KV_RANK, 4 * LANES), layer),
        _layer_spec((MLA_KV_RANK, GROUP_W), layer),
    ]
    args = [x, mod, norm_g, w_in, q_norm_g, wuq, kv_norm_g, wuk, wuv]
    if rope:
        in_specs += [pl.BlockSpec((tm, LANES), lambda i: (i % nblk, 0))] * 6
        args += list(tables)
    widths = [N_PAIRS * TOK_W, N_PAIRS * SEQ_W, N_PAIRS * LOC_W, CONV_W]
    out_specs = [pl.BlockSpec((tm, w), lambda i: (i, 0)) for w in widths]
    out_shape = [jax.ShapeDtypeStruct((t, w), BF16) for w in widths]
    aliases, mode = {}, False
    if states is not None:
        mode = "update" if states else "create"
        assert states or layer == 0
        at = layer if states else None
        for k, (shape, block, index) in enumerate(_state_layout(t // seq, seq, tm)):
            out_specs.append(pl.BlockSpec(block(at), index(at)))
            out_shape.append(jax.ShapeDtypeStruct(shape, F32))
            if states:
                aliases[len(args)] = len(widths) + k
                in_specs.append(pl.BlockSpec(memory_space=pl.ANY))
                args.append(states[k])
    return pl.pallas_call(
        functools.partial(_in_kernel, rope=rope, states=mode),
        grid=(t // tm,),
        in_specs=in_specs,
        out_specs=out_specs,
        out_shape=out_shape,
        input_output_aliases=aliases,
        compiler_params=_params(1),
        name=name,
    )(*args)


def _attn_kernel(*refs, cached, lam_init, final):
    it = iter(refs)
    tok_ref, seq_ref = next(it), next(it)
    loc_refs = [next(it) for _ in range(LOCAL_CHUNKS if cached else 1)]
    if cached:
        x_ref, bias_ref = next(it), next(it)
    lam_ref, g_ref = next(it), next(it)
    res_ref, mod_ref, bb_ref, gc_ref, gp_ref, gn_ref, sbz_ref, cw_ref, w_ref, fg_ref = (
        next(it) for _ in range(10))
    o_ref = next(it)

    lv = lam_ref[...]
    lam = (jnp.exp(jnp.sum(lv[0:1] * lv[1:2], keepdims=True))
           - jnp.exp(jnp.sum(lv[2:3] * lv[3:4], keepdims=True)) + lam_init)

    tq = res_ref.shape[0]
    i, nq = pl.program_id(1), pl.num_programs(1)
    has_prev = jnp.where(i != 0, 1.0, 0.0)
    has_next = jnp.where(i != nq - 1, 1.0, 0.0)
    gc = gc_ref[...].astype(F32)
    rows = lax.broadcasted_iota(jnp.int32, (tq, 1), 0)
    halo = gp_ref.shape[0]
    g_prev = jnp.where(rows == 0, gp_ref[halo - 1:halo, :].astype(F32) * has_prev,
                       pltpu.roll(gc, 1, 0))
    g_next = jnp.where(rows == tq - 1, gn_ref[0:1, :].astype(F32) * has_next,
                       pltpu.roll(gc, tq - 1, 0))
    cw = cw_ref[...]
    conv = g_prev * cw[0:1] + gc * cw[1:2] + g_next * cw[2:3]
    yb = (bb_ref[...].astype(F32) * conv * sbz_ref[...].astype(F32)).astype(BF16)

    first = _lane_mask(0, HEAD_DIM)
    proj = []

    def gate(hp, off):
        return tok_ref[:, TOK_W * hp + off:TOK_W * hp + off + LANES].astype(F32)

    def project(group, halves):
        y = jnp.concatenate(halves, axis=1).astype(BF16)
        proj.append(_dot(y, w_ref[GROUP_W * group:GROUP_W * (group + 1), :]))

    def after_a(outs):
        proj.append(_dot(yb, w_ref[GROUP_W:2 * GROUP_W, :]))
        halves = []
        for hp in range(N_PAIRS):
            m00, m01, m10, m11 = outs[4 * hp:4 * hp + 4]
            o = jnp.where(first, m00 - lam * m01, m10 - lam * m11)
            o2 = o * o
            ss0 = jnp.sum(jnp.where(first, o2, 0.0), axis=-1, keepdims=True)
            ss1 = jnp.sum(jnp.where(first, 0.0, o2), axis=-1, keepdims=True)
            ms = jnp.where(first, ss0, ss1) * (1.0 / HEAD_DIM)
            y = o * lax.rsqrt(ms + EPS) * g_ref[...] * (1.0 - lam_init)
            halves.append(y * gate(hp, T_SZA))
        project(0, halves)

    def after_group(group, base, off):
        def run(outs):
            project(group, [jnp.where(first, outs[base + 2 * hp], outs[base + 2 * hp + 1])
                            * gate(hp, off) for hp in range(N_PAIRS)])
        return run

    jobs_a, jobs_c, jobs_d = [], [], []
    for hp in range(N_PAIRS):
        def col(ref, width, off, w=LANES, hp=hp):
            return ref[:, width * hp + off:width * hp + off + w]

        xcol = lambda off, hp=hp: x_ref[hp, :, off:off + LANES]
        qa = col(tok_ref, TOK_W, T_QA)
        keys = [col(seq_ref, SEQ_W, S_KA)] + ([xcol(X_KA)] if cached else [])
        vals = [col(seq_ref, SEQ_W, S_VA)] + ([xcol(X_VA)] if cached else [])
        nob = [None] * len(keys)
        for e in range(2):
            for c in range(2):
                qm = jnp.where(_lane_mask(HEAD_DIM * e + DA_SUB * c, DA_SUB), qa, jnp.zeros_like(qa))
                jobs_a.append((qm, keys, nob, vals, HEAD_DIM * (1 - e)))
        vals = [col(seq_ref, SEQ_W, S_VC)] + ([xcol(X_VC)] if cached else [])
        for e in range(2):
            keys = [col(seq_ref, SEQ_W, S_KC + LANES * e)] + (
                [xcol(X_KC + LANES * e)] if cached else [])
            jobs_c.append((col(tok_ref, TOK_W, T_QC + LANES * e), keys, nob, vals,
                           HEAD_DIM * (1 - e)))
        qd = col(tok_ref, TOK_W, T_QD)
        keys = [col(r, LOC_W, L_KD) for r in loc_refs] + ([xcol(X_KD)] if cached else [])
        vals = [col(r, LOC_W, L_VD) for r in loc_refs] + ([xcol(X_VD)] if cached else [])
        for e in range(2):
            qm = jnp.where(_lane_mask(HEAD_DIM * e, HEAD_DIM), qd, jnp.zeros_like(qd))
            biases = [None] * len(keys)
            if cached:
                biases = [bias_ref[2 * hp + e, :, TQ * j:TQ * (j + 1)]
                          for j in range(len(loc_refs))] + [None]
            jobs_d.append((qm, keys, biases, vals, HEAD_DIM * (1 - e)))

    n_a, n = len(jobs_a), 2 * N_PAIRS
    after = {n_a - 1: after_a, n_a + n - 1: after_group(2, n_a, T_SCZ),
             n_a + 2 * n - 1: after_group(3, n_a + n, T_SDZ)}
    _attend_all(jobs_a + jobs_c + jobs_d, after)
    xn = res_ref[...] + mod_ref[2:3, :] * functools.reduce(jnp.add, proj)
    if final:
        xn = _rms(xn, fg_ref[...])
    o_ref[...] = xn


def _attn_call(x, mod, u_tok, u_seq, u_loc, u_conv, cache, bias, lam_vecs, subln_g2, conv_w, w_out,
               final_g, *, layer, lam_init, nb, seq, tq, mod_row0, final, name):
    cached = cache is not None
    nq = seq // tq
    in_specs = [pl.BlockSpec((tq, N_PAIRS * TOK_W), lambda b, i: (b * nq + i, 0)),
                pl.BlockSpec((seq, N_PAIRS * SEQ_W), lambda b, i: (b, 0))]
    args = [u_tok, u_seq]
    if cached:
        past = cache.shape[3]
        base = lambda i: jnp.clip(i - 1, 0, nq - LOCAL_CHUNKS)
        for j in range(LOCAL_CHUNKS):
            in_specs.append(pl.BlockSpec(
                (tq, N_PAIRS * LOC_W), lambda b, i, j=j: (b * nq + base(i) + j, 0)))
            args.append(u_loc)

        def bias_map(b, i):
            pattern = jnp.where(i == 0, 0, jnp.where(i == nq - 1, 2, 1))
            return (layer, pattern, 0, 0, 0)

        in_specs += [pl.BlockSpec((None, None, N_PAIRS, past, CACHE_W),
                                  lambda b, i: (b, layer, 0, 0, 0)),
                     pl.BlockSpec((None, None, N_HEADS_GRP, tq, LOCAL_CHUNKS * tq), bias_map)]
        args += [cache, bias]
    else:
        in_specs.append(pl.BlockSpec((seq, N_PAIRS * LOC_W), lambda b, i: (b, 0)))
        args.append(u_loc)
    in_specs += [_layer_spec((4, DA_SUB), layer), _layer_spec((1, LANES), layer)]
    args += [lam_vecs, subln_g2]

    halo = BF16_SUBLANES
    hb = tq // halo
    last = nb * seq // halo - 1
    blk = lambda b, i: b * nq + i
    conv = lambda off: pl.BlockSpec((tq, GROUP_W), lambda b, i: (blk(b, i), off // GROUP_W))
    mod_row = (lambda b: 0) if mod_row0 == 0 else (lambda b: mod_row0 + b)
    in_specs += [
        pl.BlockSpec((tq, D_MODEL), lambda b, i: (blk(b, i), 0)),
        pl.BlockSpec((None, None, 3, D_MODEL), lambda b, i: (layer, mod_row(b), 0, 0)),
        conv(B_BB), conv(B_G),
        pl.BlockSpec((halo, GROUP_W),
                     lambda b, i: (jnp.maximum(blk(b, i) * hb - 1, 0), B_G // GROUP_W)),
        pl.BlockSpec((halo, GROUP_W),
                     lambda b, i: (jnp.minimum((blk(b, i) + 1) * hb, last), B_G // GROUP_W)),
        conv(B_SBZ),
        _layer_spec((3, GROUP_W), layer),
        _layer_spec((D_MODEL, D_MODEL), layer),
        pl.BlockSpec((1, D_MODEL), lambda b, i: (0, 0)),
    ]
    args += [x, mod, u_conv, u_conv, u_conv, u_conv, u_conv, conv_w, w_out, final_g]
    return pl.pallas_call(
        functools.partial(_attn_kernel, cached=cached, lam_init=lam_init, final=final),
        grid=(nb, nq),
        in_specs=in_specs,
        out_specs=pl.BlockSpec((tq, D_MODEL), lambda b, i: (blk(b, i), 0)),
        out_shape=jax.ShapeDtypeStruct((nb * seq, D_MODEL), F32),
        compiler_params=_params(2),
        name=name,
    )(*args)


def _local_bias_tables(rpb, rows):
    nq = rows // Q_ROWS
    n_dy, n_dx = 2 * NA_WIN_H - 1, 2 * NA_WIN_W - 1
    qc = np.arange(GRID_W)[:, None]
    kc = np.arange(GRID_W)[None, :]
    ws = np.clip(qc - NA_WIN_W // 2, 0, GRID_W - NA_WIN_W)
    col_ok = (kc >= ws) & (kc < ws + NA_WIN_W)
    oh_dx = ((kc - qc + NA_WIN_W - 1)[None] == np.arange(n_dx)[:, None, None]) & col_ok[None]
    tile_idx = []
    for i in (0, 1, nq - 1):
        row0 = Q_ROWS * int(np.clip(i - 1, 0, nq - LOCAL_CHUNKS))
        r = (Q_ROWS * i + np.arange(Q_ROWS))[:, None]
        kr = (row0 + np.arange(LOCAL_ROWS))[None, :]
        rs = np.clip(r - NA_WIN_H // 2, 0, rows - NA_WIN_H)
        ok = (kr >= rs) & (kr < rs + NA_WIN_H)
        tile_idx.append(np.where(ok, kr - r + NA_WIN_H - 1, n_dy))
    tile_idx = np.stack(tile_idx)
    hi = lax.Precision.HIGHEST
    cols = jnp.einsum("lhyd,dqc->lhyqc", rpb.astype(F32), jnp.asarray(oh_dx, F32), precision=hi)
    cols = jnp.where(col_ok, cols * LOG2E, NEG_INF)
    cols = jnp.concatenate(
        [cols, jnp.full((DEPTH, N_HEADS_GRP, 1, GRID_W, GRID_W), NEG_INF, F32)], axis=2)
    cols = jnp.concatenate([cols, cols], axis=-1)

    def build(cols_ref, o_ref):
        left = _lane_mask(0, GRID_W)
        for p in range(3):
            for j in range(Q_ROWS):
                for m in range(LOCAL_ROWS // 2):
                    a, b = int(tile_idx[p, j, 2 * m]), int(tile_idx[p, j, 2 * m + 1])
                    o_ref[p, GRID_W * j:GRID_W * (j + 1), LANES * m:LANES * (m + 1)] = jnp.where(
                        left, cols_ref[a], cols_ref[b])

    return pl.pallas_call(
        build,
        grid=(DEPTH, N_HEADS_GRP),
        in_specs=[pl.BlockSpec((None, None, n_dy + 1, GRID_W, LANES), lambda l, h: (l, h, 0, 0, 0))],
        out_specs=pl.BlockSpec((None, 3, None, TQ, LOCAL_ROWS * GRID_W), lambda l, h: (l, 0, h, 0, 0)),
        out_shape=jax.ShapeDtypeStruct((DEPTH, 3, N_HEADS_GRP, TQ, LOCAL_ROWS * GRID_W), F32),
        compiler_params=_params(2),
        name="local_bias",
    )(cols)


def _w_in_prep_kernel(w_ref, o_ref):
    rows = w_ref.shape[1]
    kpe_end = W_KPE + MLA_ROPE
    o_ref[:, :W_KPE] = w_ref[:W_KPE, :].T.astype(BF16)
    kpe = jnp.concatenate([jnp.zeros((KPE_LANE, rows), F32), w_ref[W_KPE:kpe_end, :],
                           jnp.zeros((LANES - KPE_LANE - MLA_ROPE, rows), F32)], axis=0)
    o_ref[:, W_KPE:W_CZ] = kpe.T.astype(BF16)
    o_ref[:, W_CZ:] = w_ref[kpe_end:, :].T.astype(BF16)


def _w_in_prep(w_in):
    d_in = w_in.shape[-1]
    rows = W_PREP_ROWS
    return pl.pallas_call(
        _w_in_prep_kernel,
        grid=(DEPTH, D_MODEL // rows),
        in_specs=[pl.BlockSpec((None, d_in, rows), lambda l, r: (l, 0, r))],
        out_specs=pl.BlockSpec((None, rows, D_IN_P), lambda l, r: (l, r, 0)),
        out_shape=jax.ShapeDtypeStruct((DEPTH, D_MODEL, D_IN_P), BF16),
        compiler_params=_params(2),
        name="w_in_prep",
    )(jnp.swapaxes(w_in, 1, 2))


def _rope_tables(seq):
    t = jnp.arange(seq)
    rows = (t // GRID_W).astype(F32)
    cols = (t % GRID_W).astype(F32)
    half = DA_SUB // 2
    inv = 1.0 / (ROPE_BASE ** (jnp.arange(0, half, 2, dtype=F32) / half))
    ar = rows[:, None] * inv
    ac = cols[:, None] * inv
    ang = jnp.concatenate([ar, ar, ac, ac], axis=-1)
    cos, sin = jnp.cos(ang), jnp.sin(ang)
    first = (np.arange(DA_SUB) % 16 < 8)[None, :]
    sin_neg = jnp.where(first, -sin, 0.0)
    sin_pos = jnp.where(first, 0.0, sin)
    tile = lambda a: jnp.tile(a, (1, LANES // DA_SUB))

    def pad(a, fill):
        return jnp.concatenate([jnp.full((seq, KPE_LANE), fill, F32), a,
                                jnp.full((seq, LANES - KPE_LANE - MLA_ROPE), fill, F32)], axis=1)

    return ([tile(cos), tile(sin_neg), tile(sin_pos)],
            [pad(cos, 1.0), pad(sin_neg, 0.0), pad(sin_pos, 0.0)])


def _pad_heads(w, width, take):
    d, k, _ = w.shape
    w = w.reshape(d, k, N_HEADS_GRP, width)[..., :take]
    return jnp.pad(w, ((0, 0), (0, 0), (0, 0), (0, LANES - take))).reshape(d, k, N_HEADS_GRP * LANES)


def kernel(x_prompt, x_sample, cache_a_k, cache_a_v, cache_c_kv, cache_c_kpe, cache_d_k, cache_d_v,
           c, c_ctx, ada_w, ada_b, norm_g, w_in, da_lambda, da_subln_g, conv_w,
           mla_q_norm_g, mla_w_uq, mla_kv_norm_g, mla_w_ukv, na_rpb, w_out, final_norm_g):
    batch, seq, _ = x_prompt.shape
    dec_batch, dec_seq, _ = x_sample.shape
    past = cache_a_k.shape[2]
    assert dec_seq % TQ == 0 and dec_seq // TQ >= LOCAL_CHUNKS and seq % LANES == 0
    assert dec_seq % IN_PROJ_ROWS == 0 and D_MODEL % W_PREP_ROWS == 0

    w_in_p = _w_in_prep(w_in)
    w_out_b = w_out.astype(BF16)
    wuq = _pad_heads(mla_w_uq, MLA_NOPE + MLA_ROPE, MLA_NOPE + MLA_ROPE).astype(BF16)
    wuk = _pad_heads(mla_w_ukv, MLA_NOPE + MLA_V, MLA_NOPE).astype(BF16)
    wuv = mla_w_ukv.reshape(DEPTH, MLA_KV_RANK, N_HEADS_GRP, MLA_NOPE + MLA_V)[..., MLA_NOPE:]
    wuv = wuv.reshape(DEPTH, MLA_KV_RANK, GROUP_W).astype(BF16)
    subln_g2 = jnp.tile(da_subln_g, (1, LANES // HEAD_DIM)).reshape(DEPTH, 1, LANES)
    ng = norm_g.reshape(DEPTH, 1, D_MODEL)
    qng = mla_q_norm_g.reshape(DEPTH, 1, MLA_Q_RANK)
    kvng = mla_kv_norm_g.reshape(DEPTH, 1, MLA_KV_RANK)
    fg = final_norm_g.reshape(1, D_MODEL)
    tables_a, tables_c = _rope_tables(dec_seq)
    tables = tables_a + tables_c

    cvec8 = jnp.concatenate([c_ctx[None], c, jnp.zeros((8 - 1 - dec_batch, D_MODEL), F32)], axis=0)
    mod = _ada_call(cvec8, ada_w, ada_b).reshape(DEPTH, 8, 3, D_MODEL)

    feat = lambda a: a.transpose(0, 1, 3, 4, 2).reshape(dec_batch, DEPTH, GROUP_W, past)
    cache = _cache_call(feat(cache_a_k), feat(cache_a_v), cache_c_kv,
                        cache_c_kpe.transpose(0, 1, 3, 2), feat(cache_d_k), feat(cache_d_v), wuk, wuv)
    bias = _local_bias_tables(na_rpb, dec_seq // GRID_W)

    xp = x_prompt.reshape(batch * seq, D_MODEL)
    xs = x_sample.reshape(dec_batch * dec_seq, D_MODEL)
    states = []
    for l in range(DEPTH):
        lam_init = 0.8 - 0.6 * math.exp(-0.3 * l)
        final = l == DEPTH - 1

        tok, sq, loc, cv, *states = _in_call(xp, mod, ng, w_in_p, qng, wuq, kvng, wuk, wuv, None,
                                             layer=l, seq=seq, tm=seq, mod_row0=0, states=states,
                                             name=f"ctx_in_{l}")
        xp = _attn_call(xp, mod, tok, sq, loc, cv, None, None, da_lambda, subln_g2, conv_w, w_out_b,
                        fg, layer=l, lam_init=lam_init, nb=batch, seq=seq, tq=seq, mod_row0=0,
                        final=final, name=f"ctx_attn_{l}")

        tok, sq, loc, cv = _in_call(xs, mod, ng, w_in_p, qng, wuq, kvng, wuk, wuv, tables,
                                    layer=l, seq=dec_seq, tm=IN_PROJ_ROWS, mod_row0=1, states=None,
                                    name=f"lat_in_{l}")
        xs = _attn_call(xs, mod, tok, sq, loc, cv, cache, bias, da_lambda, subln_g2, conv_w, w_out_b,
                        fg, layer=l, lam_init=lam_init, nb=dec_batch, seq=dec_seq, tq=TQ, mod_row0=1,
                        final=final, name=f"lat_attn_{l}")

    def heads(a):
        return a.reshape(batch, DEPTH, N_HEADS_GRP, HEAD_DIM, seq).transpose(0, 1, 4, 2, 3)

    s_ak, s_av, s_ckv, s_kpe, s_dk, s_dv = states
    return (xp.reshape(batch, seq, D_MODEL), xs.reshape(dec_batch, dec_seq, D_MODEL),
            heads(s_ak), heads(s_av), s_ckv, s_kpe.transpose(0, 1, 3, 2), heads(s_dk), heads(s_dv))
```

```python
import functools
import math

import jax
import jax.numpy as jnp
import numpy as np
from jax import lax
from jax.experimental import pallas as pl
from jax.experimental.pallas import tpu as pltpu

F32 = jnp.float32
BF16 = jnp.bfloat16

D_MODEL = 1024
DEPTH = 4
GRID_W = 64
HEAD_DIM = 64
GROUP_W = 256
N_HEADS_GRP = 4
DA_SUB = 32
MLA_Q_RANK = 256
MLA_KV_RANK = 128
MLA_NOPE = 64
MLA_ROPE = 32
MLA_V = 64
LOG2E = math.log2(math.e)
MLA_SCALE = (MLA_NOPE + MLA_ROPE) ** -0.5
DA_SCALE = DA_SUB ** -0.5
NA_SCALE = HEAD_DIM ** -0.5
NA_WIN_H = 8
NA_WIN_W = 16
ROPE_BASE = 10000.0
EPS = 1e-6
NEG_INF = -1e30

LANES = 128
BF16_SUBLANES = 16
N_PAIRS = GROUP_W // LANES
VMEM_LIMIT = 56 * 1024 * 1024
IN_PROJ_ROWS = 512
W_PREP_ROWS = 256

W_AQ, W_AK, W_AV, W_AZ = 0, 256, 512, 768
W_BB, W_BC, W_BH, W_BZ = 1024, 1280, 1536, 1792
W_CQ, W_CKV, W_KPE, W_CZ = 2048, 2304, 2432, 2560
W_DQ, W_DK, W_DV, W_DZ = 2816, 3072, 3328, 3584
D_IN_P = 3840
KPE_LANE = 64

T_QA, T_SZA, T_QC, T_SCZ, T_QD, T_SDZ, TOK_W = 0, 128, 256, 512, 640, 768, 896
S_KA, S_VA, S_KC, S_VC, SEQ_W = 0, 128, 256, 512, 640
L_KD, L_VD, LOC_W = 0, 128, 256
X_KA, X_VA, X_KC, X_VC, X_KD, X_VD, CACHE_W = 0, 128, 256, 512, 640, 768, 896
B_BB, B_G, B_SBZ, CONV_W = 0, 256, 512, 768

Q_ROWS = 4
TQ = Q_ROWS * GRID_W
LOCAL_CHUNKS = 3
LOCAL_ROWS = LOCAL_CHUNKS * Q_ROWS


def _dot(a, b):
    return jnp.dot(a, b, preferred_element_type=F32)


def _dot_nt(a, b):
    return lax.dot_general(a, b, (((1,), (1,)), ((), ())), preferred_element_type=F32)


def _silu(z):
    return z * (1.0 / (1.0 + jnp.exp(-z)))


def _rms(x, g):
    return x * lax.rsqrt(jnp.mean(x * x, axis=-1, keepdims=True) + EPS) * g


def _rope128(x, cos, sin_neg, sin_pos):
    return x * cos + pltpu.roll(x, LANES - 8, 1) * sin_neg + pltpu.roll(x, 8, 1) * sin_pos


def _scores(job):
    q, keys, biases, _, _ = job
    blocks = []
    for k, b in zip(keys, biases):
        s = _dot_nt(q, k)
        blocks.append(s if b is None else s + b)
    return blocks


def _weighted_values(job, blocks):
    _, _, _, values, sum_lane = job
    m = functools.reduce(jnp.maximum, [jnp.max(s, axis=-1, keepdims=True) for s in blocks])
    one_hot = _lane_mask(sum_lane, 1)
    o = functools.reduce(jnp.add, [
        _dot(jnp.exp2(s - m).astype(BF16), jnp.where(one_hot, jnp.ones_like(v), v))
        for s, v in zip(blocks, values)])
    l = jnp.sum(jnp.where(one_hot, o, 0.0), axis=-1, keepdims=True)
    return o / l


SCORES_IN_FLIGHT_BYTES = 6 * 1024 * 1024


def _attend_all(jobs, after):
    q, keys = jobs[0][0], jobs[0][1]
    score_bytes = 4 * q.shape[0] * sum(k.shape[0] for k in keys)
    ahead = max(1, min(len(jobs), SCORES_IN_FLIGHT_BYTES // score_bytes))
    outs = []
    pending = [_scores(job) for job in jobs[:ahead]]
    for k, job in enumerate(jobs):
        if k + ahead < len(jobs):
            pending.append(_scores(jobs[k + ahead]))
        outs.append(_weighted_values(job, pending.pop(0)))
        if k in after:
            after[k](outs)
    return outs


def _lane_mask(lo, width):
    lane = lax.broadcasted_iota(jnp.int32, (1, LANES), 1)
    return jnp.logical_and(lane >= lo, lane < lo + width)


def _params(n_axes):
    return pltpu.CompilerParams(dimension_semantics=("arbitrary",) * n_axes,
                                vmem_limit_bytes=VMEM_LIMIT)


def _mod_map(layer, mod_row0, nblk):
    if mod_row0 == 0:
        return lambda i: (layer, 0, 0, 0)
    return lambda i: (layer, mod_row0 + i // nblk, 0, 0)


def _layer_spec(shape, layer):
    zeros = (0,) * len(shape)
    return pl.BlockSpec((None,) + tuple(shape), lambda *_: (layer,) + zeros)


def _ada_kernel(c_ref, w_ref, b_ref, o_ref):
    s = _silu(c_ref[...]).astype(BF16)
    o_ref[...] = _dot(s, w_ref[...].astype(BF16)) + b_ref[...]


def _ada_call(cvec8, ada_w, ada_b):
    nj = 3
    return pl.pallas_call(
        _ada_kernel,
        grid=(DEPTH, nj),
        in_specs=[pl.BlockSpec((8, D_MODEL), lambda l, j: (0, 0)),
                  pl.BlockSpec((None, D_MODEL, D_MODEL), lambda l, j: (l, 0, j)),
                  pl.BlockSpec((None, None, 1, D_MODEL), lambda l, j: (l, j, 0, 0))],
        out_specs=pl.BlockSpec((None, 8, D_MODEL), lambda l, j: (l, 0, j)),
        out_shape=jax.ShapeDtypeStruct((DEPTH, 8, 3 * D_MODEL), F32),
        compiler_params=_params(2),
        name="adaln",
    )(cvec8, ada_w, ada_b.reshape(DEPTH, nj, 1, D_MODEL))


def _cache_kernel(ak_ref, av_ref, ckv_ref, kpe_ref, dk_ref, dv_ref, wuk_ref, wuv_ref, o_ref):
    ckv = ckv_ref[...].astype(BF16)
    kn = _dot(ckv, wuk_ref[...])
    vc = _dot(ckv, wuv_ref[...])
    past = ckv.shape[0]
    kpe = jnp.concatenate([jnp.zeros((KPE_LANE, past), F32), kpe_ref[...],
                           jnp.zeros((LANES - KPE_LANE - MLA_ROPE, past), F32)], axis=0).T
    for hp in range(N_PAIRS):
        half = slice(LANES * hp, LANES * (hp + 1))
        o_ref[hp, :, X_KA:X_KA + LANES] = ak_ref[half, :].T.astype(BF16)
        o_ref[hp, :, X_VA:X_VA + LANES] = av_ref[half, :].T.astype(BF16)
        for e in range(2):
            sl = slice(LANES * (2 * hp + e), LANES * (2 * hp + e + 1))
            o_ref[hp, :, X_KC + LANES * e:X_KC + LANES * (e + 1)] = (kn[:, sl] + kpe).astype(BF16)
        o_ref[hp, :, X_VC:X_VC + LANES] = vc[:, half].astype(BF16)
        o_ref[hp, :, X_KD:X_KD + LANES] = dk_ref[half, :].T.astype(BF16)
        o_ref[hp, :, X_VD:X_VD + LANES] = dv_ref[half, :].T.astype(BF16)


def _cache_call(ca_k, ca_v, c_kv, c_kpe, cd_k, cd_v, wuk, wuv):
    nb, _, past, _ = c_kv.shape
    cache = lambda w: pl.BlockSpec((None, None, past, w), lambda l, b: (b, l, 0, 0))
    feat = lambda rows: pl.BlockSpec((None, None, rows, past), lambda l, b: (b, l, 0, 0))
    return pl.pallas_call(
        _cache_kernel,
        grid=(DEPTH, nb),
        in_specs=[feat(GROUP_W), feat(GROUP_W), cache(MLA_KV_RANK), feat(MLA_ROPE),
                  feat(GROUP_W), feat(GROUP_W),
                  pl.BlockSpec((None, MLA_KV_RANK, 4 * LANES), lambda l, b: (l, 0, 0)),
                  pl.BlockSpec((None, MLA_KV_RANK, GROUP_W), lambda l, b: (l, 0, 0))],
        out_specs=pl.BlockSpec((None, None, N_PAIRS, past, CACHE_W), lambda l, b: (b, l, 0, 0, 0)),
        out_shape=jax.ShapeDtypeStruct((nb, DEPTH, N_PAIRS, past, CACHE_W), BF16),
        compiler_params=_params(2),
        name="cache_prep",
    )(ca_k, ca_v, c_kv, c_kpe, cd_k, cd_v, wuk, wuv)


def _in_kernel(*refs, rope, states):
    it = iter(refs)
    x_ref, mod_ref, ng_ref, w_ref, qng_ref, wuq_ref, kvng_ref, wuk_ref, wuv_ref = (
        next(it) for _ in range(9))
    if rope:
        ta = [next(it)[...] for _ in range(3)]
        tc = [next(it)[...] for _ in range(3)]
    if states == "update":
        for _ in range(6):
            next(it)
    tok_ref, seq_ref, loc_ref, conv_ref = (next(it) for _ in range(4))
    if states:
        sak_ref, sav_ref, sckv_ref, skpe_ref, sdk_ref, sdv_ref = (next(it) for _ in range(6))

    x = x_ref[...]
    h = _rms(x, ng_ref[...]) * (1.0 + mod_ref[1:2, :]) + mod_ref[0:1, :]
    hb = h.astype(BF16)

    def seg(off, n=GROUP_W):
        return _dot(hb, w_ref[:, off:off + n])

    def put_state(ref, val):
        if states == "create":
            ref[0] = val
            for later in range(1, DEPTH):
                ref[later] = jnp.zeros_like(val)
        else:
            ref[...] = val

    def put(ref, width, off, val):
        for hp in range(N_PAIRS):
            ref[:, width * hp + off:width * hp + off + LANES] = (
                val[:, LANES * hp:LANES * (hp + 1)].astype(BF16))

    cqn = _rms(seg(W_CQ), qng_ref[...]).astype(BF16)
    ckvn = _rms(seg(W_CKV, MLA_KV_RANK), kvng_ref[...])
    kpe = seg(W_KPE, LANES)
    if states:
        put_state(sckv_ref, ckvn)
        put_state(skpe_ref, kpe.T[KPE_LANE:KPE_LANE + MLA_ROPE, :])
    ckvb = ckvn.astype(BF16)

    aq, ak, av = seg(W_AQ), seg(W_AK), seg(W_AV)
    if states:
        put_state(sak_ref, ak.T)
        put_state(sav_ref, av.T)
    if rope:
        aq = jnp.concatenate([_rope128(aq[:, :LANES], *ta), _rope128(aq[:, LANES:], *ta)], axis=1)
        ak = jnp.concatenate([_rope128(ak[:, :LANES], *ta), _rope128(ak[:, LANES:], *ta)], axis=1)
    put(tok_ref, TOK_W, T_QA, aq * (DA_SCALE * LOG2E))
    put(seq_ref, SEQ_W, S_KA, ak)
    put(seq_ref, SEQ_W, S_VA, av)
    put(tok_ref, TOK_W, T_SZA, _silu(seg(W_AZ)))

    conv_ref[:, B_BB:B_BB + GROUP_W] = seg(W_BB).astype(BF16)
    conv_ref[:, B_G:B_G + GROUP_W] = (seg(W_BC) * seg(W_BH)).astype(BF16)
    conv_ref[:, B_SBZ:B_SBZ + GROUP_W] = _silu(seg(W_BZ)).astype(BF16)

    dk, dv = seg(W_DK), seg(W_DV)
    if states:
        put_state(sdk_ref, dk.T)
        put_state(sdv_ref, dv.T)
    put(tok_ref, TOK_W, T_QD, seg(W_DQ) * (NA_SCALE * LOG2E))
    put(loc_ref, LOC_W, L_KD, dk)
    put(loc_ref, LOC_W, L_VD, dv)
    put(tok_ref, TOK_W, T_SDZ, _silu(seg(W_DZ)))
    put(tok_ref, TOK_W, T_SCZ, _silu(seg(W_CZ)))

    q = _dot(cqn, wuq_ref[...])
    kn = _dot(ckvb, wuk_ref[...])
    vc = _dot(ckvb, wuv_ref[...])
    kpe_r = _rope128(kpe, *tc) if rope else kpe
    for hd in range(N_HEADS_GRP):
        sl = slice(LANES * hd, LANES * (hd + 1))
        hp, e = divmod(hd, 2)
        qh = q[:, sl]
        if rope:
            qh = _rope128(qh, *tc)
        o = LANES * e
        tok_ref[:, TOK_W * hp + T_QC + o:TOK_W * hp + T_QC + o + LANES] = (
            qh * (MLA_SCALE * LOG2E)).astype(BF16)
        seq_ref[:, SEQ_W * hp + S_KC + o:SEQ_W * hp + S_KC + o + LANES] = (
            kn[:, sl] + kpe_r).astype(BF16)
    put(seq_ref, SEQ_W, S_VC, vc)


def _state_layout(batch, seq, tm):
    nblk = seq // tm
    lblk = lambda layer: DEPTH if layer is None else None
    lidx = lambda layer: 0 if layer is None else layer
    feat = lambda rows: ((batch, DEPTH, rows, seq),
                         lambda layer: (None, lblk(layer), rows, tm),
                         lambda layer: lambda i: (i // nblk, lidx(layer), 0, i % nblk))
    tokm = ((batch, DEPTH, seq, MLA_KV_RANK),
            lambda layer: (None, lblk(layer), tm, MLA_KV_RANK),
            lambda layer: lambda i: (i // nblk, lidx(layer), i % nblk, 0))
    return [feat(GROUP_W), feat(GROUP_W), tokm, feat(MLA_ROPE), feat(GROUP_W), feat(GROUP_W)]


def _in_call(x, mod, norm_g, w_in, q_norm_g, wuq, kv_norm_g, wuk, wuv, tables, *, layer, seq, tm,
             mod_row0, states, name):
    t = x.shape[0]
    nblk = seq // tm
    rope = tables is not None
    in_specs = [
        pl.BlockSpec((tm, D_MODEL), lambda i: (i, 0)),
        pl.BlockSpec((None, None, 3, D_MODEL), _mod_map(layer, mod_row0, nblk)),
        _layer_spec((1, D_MODEL), layer),
        _layer_spec((D_MODEL, D_IN_P), layer),
        _layer_spec((1, MLA_Q_RANK), layer),
        _layer_spec((MLA_Q_RANK, 4 * LANES), layer),
        _layer_spec((1, MLA_KV_RANK), layer),
        _layer_spec((MLA_KV_RANK, 4 * LANES), layer),
        _layer_spec((MLA_KV_RANK, GROUP_W), layer),
    ]
    args = [x, mod, norm_g, w_in, q_norm_g, wuq, kv_norm_g, wuk, wuv]
    if rope:
        in_specs += [pl.BlockSpec((tm, LANES), lambda i: (i % nblk, 0))] * 6
        args += list(tables)
    widths = [N_PAIRS * TOK_W, N_PAIRS * SEQ_W, N_PAIRS * LOC_W, CONV_W]
    out_specs = [pl.BlockSpec((tm, w), lambda i: (i, 0)) for w in widths]
    out_shape = [jax.ShapeDtypeStruct((t, w), BF16) for w in widths]
    aliases, mode = {}, False
    if states is not None:
        mode = "update" if states else "create"
        assert states or layer == 0
        at = layer if states else None
        for k, (shape, block, index) in enumerate(_state_layout(t // seq, seq, tm)):
            out_specs.append(pl.BlockSpec(block(at), index(at)))
            out_shape.append(jax.ShapeDtypeStruct(shape, F32))
            if states:
                aliases[len(args)] = len(widths) + k
                in_specs.append(pl.BlockSpec(memory_space=pl.ANY))
                args.append(states[k])
    return pl.pallas_call(
        functools.partial(_in_kernel, rope=rope, states=mode),
        grid=(t // tm,),
        in_specs=in_specs,
        out_specs=out_specs,
        out_shape=out_shape,
        input_output_aliases=aliases,
        compiler_params=_params(1),
        name=name,
    )(*args)


def _attn_kernel(*refs, cached, lam_init, final):
    it = iter(refs)
    tok_ref, seq_ref = next(it), next(it)
    loc_refs = [next(it) for _ in range(LOCAL_CHUNKS if cached else 1)]
    if cached:
        x_ref, bias_ref = next(it), next(it)
    lam_ref, g_ref = next(it), next(it)
    res_ref, mod_ref, bb_ref, gc_ref, gp_ref, gn_ref, sbz_ref, cw_ref, w_ref, fg_ref = (
        next(it) for _ in range(10))
    o_ref = next(it)

    lv = lam_ref[...]
    lam = (jnp.exp(jnp.sum(lv[0:1] * lv[1:2], keepdims=True))
           - jnp.exp(jnp.sum(lv[2:3] * lv[3:4], keepdims=True)) + lam_init)

    tq = res_ref.shape[0]
    i, nq = pl.program_id(1), pl.num_programs(1)
    has_prev = jnp.where(i != 0, 1.0, 0.0)
    has_next = jnp.where(i != nq - 1, 1.0, 0.0)
    gc = gc_ref[...].astype(F32)
    rows = lax.broadcasted_iota(jnp.int32, (tq, 1), 0)
    halo = gp_ref.shape[0]
    g_prev = jnp.where(rows == 0, gp_ref[halo - 1:halo, :].astype(F32) * has_prev,
                       pltpu.roll(gc, 1, 0))
    g_next = jnp.where(rows == tq - 1, gn_ref[0:1, :].astype(F32) * has_next,
                       pltpu.roll(gc, tq - 1, 0))
    cw = cw_ref[...]
    conv = g_prev * cw[0:1] + gc * cw[1:2] + g_next * cw[2:3]
    yb = (bb_ref[...].astype(F32) * conv * sbz_ref[...].astype(F32)).astype(BF16)

    first = _lane_mask(0, HEAD_DIM)
    proj = []

    def gate(hp, off):
        return tok_ref[:, TOK_W * hp + off:TOK_W * hp + off + LANES].astype(F32)

    def project(group, halves):
        y = jnp.concatenate(halves, axis=1).astype(BF16)
        proj.append(_dot(y, w_ref[GROUP_W * group:GROUP_W * (group + 1), :]))

    def after_a(outs):
        proj.append(_dot(yb, w_ref[GROUP_W:2 * GROUP_W, :]))
        halves = []
        for hp in range(N_PAIRS):
            m00, m01, m10, m11 = outs[4 * hp:4 * hp + 4]
            o = jnp.where(first, m00 - lam * m01, m10 - lam * m11)
            o2 = o * o
            ss0 = jnp.sum(jnp.where(first, o2, 0.0), axis=-1, keepdims=True)
            ss1 = jnp.sum(jnp.where(first, 0.0, o2), axis=-1, keepdims=True)
            ms = jnp.where(first, ss0, ss1) * (1.0 / HEAD_DIM)
            y = o * lax.rsqrt(ms + EPS) * g_ref[...] * (1.0 - lam_init)
            halves.append(y * gate(hp, T_SZA))
        project(0, halves)

    def after_group(group, base, off):
        def run(outs):
            project(group, [jnp.where(first, outs[base + 2 * hp], outs[base + 2 * hp + 1])
                            * gate(hp, off) for hp in range(N_PAIRS)])
        return run

    jobs_a, jobs_c, jobs_d = [], [], []
    for hp in range(N_PAIRS):
        def col(ref, width, off, w=LANES, hp=hp):
            return ref[:, width * hp + off:width * hp + off + w]

        xcol = lambda off, hp=hp: x_ref[hp, :, off:off + LANES]
        qa = col(tok_ref, TOK_W, T_QA)
        keys = [col(seq_ref, SEQ_W, S_KA)] + ([xcol(X_KA)] if cached else [])
        vals = [col(seq_ref, SEQ_W, S_VA)] + ([xcol(X_VA)] if cached else [])
        nob = [None] * len(keys)
        for e in range(2):
            for c in range(2):
                qm = jnp.where(_lane_mask(HEAD_DIM * e + DA_SUB * c, DA_SUB), qa, jnp.zeros_like(qa))
                jobs_a.append((qm, keys, nob, vals, HEAD_DIM * (1 - e)))
        vals = [col(seq_ref, SEQ_W, S_VC)] + ([xcol(X_VC)] if cached else [])
        for e in range(2):
            keys = [col(seq_ref, SEQ_W, S_KC + LANES * e)] + (
                [xcol(X_KC + LANES * e)] if cached else [])
            jobs_c.append((col(tok_ref, TOK_W, T_QC + LANES * e), keys, nob, vals,
                           HEAD_DIM * (1 - e)))
        qd = col(tok_ref, TOK_W, T_QD)
        keys = [col(r, LOC_W, L_KD) for r in loc_refs] + ([xcol(X_KD)] if cached else [])
        vals = [col(r, LOC_W, L_VD) for r in loc_refs] + ([xcol(X_VD)] if cached else [])
        for e in range(2):
            qm = jnp.where(_lane_mask(HEAD_DIM * e, HEAD_DIM), qd, jnp.zeros_like(qd))
            biases = [None] * len(keys)
            if cached:
                biases = [bias_ref[2 * hp + e, :, TQ * j:TQ * (j + 1)]
                          for j in range(len(loc_refs))] + [None]
            jobs_d.append((qm, keys, biases, vals, HEAD_DIM * (1 - e)))

    n_a, n = len(jobs_a), 2 * N_PAIRS
    after = {n_a - 1: after_a, n_a + n - 1: after_group(2, n_a, T_SCZ),
             n_a + 2 * n - 1: after_group(3, n_a + n, T_SDZ)}
    _attend_all(jobs_a + jobs_c + jobs_d, after)
    xn = res_ref[...] + mod_ref[2:3, :] * functools.reduce(jnp.add, proj)
    if final:
        xn = _rms(xn, fg_ref[...])
    o_ref[...] = xn


def _attn_call(x, mod, u_tok, u_seq, u_loc, u_conv, cache, bias, lam_vecs, subln_g2, conv_w, w_out,
               final_g, *, layer, lam_init, nb, seq, tq, mod_row0, final, name):
    cached = cache is not None
    nq = seq // tq
    in_specs = [pl.BlockSpec((tq, N_PAIRS * TOK_W), lambda b, i: (b * nq + i, 0)),
                pl.BlockSpec((seq, N_PAIRS * SEQ_W), lambda b, i: (b, 0))]
    args = [u_tok, u_seq]
    if cached:
        past = cache.shape[3]
        base = lambda i: jnp.clip(i - 1, 0, nq - LOCAL_CHUNKS)
        for j in range(LOCAL_CHUNKS):
            in_specs.append(pl.BlockSpec(
                (tq, N_PAIRS * LOC_W), lambda b, i, j=j: (b * nq + base(i) + j, 0)))
            args.append(u_loc)

        def bias_map(b, i):
            pattern = jnp.where(i == 0, 0, jnp.where(i == nq - 1, 2, 1))
            return (layer, pattern, 0, 0, 0)

        in_specs += [pl.BlockSpec((None, None, N_PAIRS, past, CACHE_W),
                                  lambda b, i: (b, layer, 0, 0, 0)),
                     pl.BlockSpec((None, None, N_HEADS_GRP, tq, LOCAL_CHUNKS * tq), bias_map)]
        args += [cache, bias]
    else:
        in_specs.append(pl.BlockSpec((seq, N_PAIRS * LOC_W), lambda b, i: (b, 0)))
        args.append(u_loc)
    in_specs += [_layer_spec((4, DA_SUB), layer), _layer_spec((1, LANES), layer)]
    args += [lam_vecs, subln_g2]

    halo = BF16_SUBLANES
    hb = tq // halo
    last = nb * seq // halo - 1
    blk = lambda b, i: b * nq + i
    conv = lambda off: pl.BlockSpec((tq, GROUP_W), lambda b, i: (blk(b, i), off // GROUP_W))
    mod_row = (lambda b: 0) if mod_row0 == 0 else (lambda b: mod_row0 + b)
    in_specs += [
        pl.BlockSpec((tq, D_MODEL), lambda b, i: (blk(b, i), 0)),
        pl.BlockSpec((None, None, 3, D_MODEL), lambda b, i: (layer, mod_row(b), 0, 0)),
        conv(B_BB), conv(B_G),
        pl.BlockSpec((halo, GROUP_W),
                     lambda b, i: (jnp.maximum(blk(b, i) * hb - 1, 0), B_G // GROUP_W)),
        pl.BlockSpec((halo, GROUP_W),
                     lambda b, i: (jnp.minimum((blk(b, i) + 1) * hb, last), B_G // GROUP_W)),
        conv(B_SBZ),
        _layer_spec((3, GROUP_W), layer),
        _layer_spec((D_MODEL, D_MODEL), layer),
        pl.BlockSpec((1, D_MODEL), lambda b, i: (0, 0)),
    ]
    args += [x, mod, u_conv, u_conv, u_conv, u_conv, u_conv, conv_w, w_out, final_g]
    return pl.pallas_call(
        functools.partial(_attn_kernel, cached=cached, lam_init=lam_init, final=final),
        grid=(nb, nq),
        in_specs=in_specs,
        out_specs=pl.BlockSpec((tq, D_MODEL), lambda b, i: (blk(b, i), 0)),
        out_shape=jax.ShapeDtypeStruct((nb * seq, D_MODEL), F32),
        compiler_params=_params(2),
        name=name,
    )(*args)


def _local_bias_tables(rpb, rows):
    nq = rows // Q_ROWS
    n_dy, n_dx = 2 * NA_WIN_H - 1, 2 * NA_WIN_W - 1
    qc = np.arange(GRID_W)[:, None]
    kc = np.arange(GRID_W)[None, :]
    ws = np.clip(qc - NA_WIN_W // 2, 0, GRID_W - NA_WIN_W)
    col_ok = (kc >= ws) & (kc < ws + NA_WIN_W)
    oh_dx = ((kc - qc + NA_WIN_W - 1)[None] == np.arange(n_dx)[:, None, None]) & col_ok[None]
    tile_idx = []
    for i in (0, 1, nq - 1):
        row0 = Q_ROWS * int(np.clip(i - 1, 0, nq - LOCAL_CHUNKS))
        r = (Q_ROWS * i + np.arange(Q_ROWS))[:, None]
        kr = (row0 + np.arange(LOCAL_ROWS))[None, :]
        rs = np.clip(r - NA_WIN_H // 2, 0, rows - NA_WIN_H)
        ok = (kr >= rs) & (kr < rs + NA_WIN_H)
        tile_idx.append(np.where(ok, kr - r + NA_WIN_H - 1, n_dy))
    tile_idx = np.stack(tile_idx)
    hi = lax.Precision.HIGHEST
    cols = jnp.einsum("lhyd,dqc->lhyqc", rpb.astype(F32), jnp.asarray(oh_dx, F32), precision=hi)
    cols = jnp.where(col_ok, cols * LOG2E, NEG_INF)
    cols = jnp.concatenate(
        [cols, jnp.full((DEPTH, N_HEADS_GRP, 1, GRID_W, GRID_W), NEG_INF, F32)], axis=2)
    cols = jnp.concatenate([cols, cols], axis=-1)

    def build(cols_ref, o_ref):
        left = _lane_mask(0, GRID_W)
        for p in range(3):
            for j in range(Q_ROWS):
                for m in range(LOCAL_ROWS // 2):
                    a, b = int(tile_idx[p, j, 2 * m]), int(tile_idx[p, j, 2 * m + 1])
                    o_ref[p, GRID_W * j:GRID_W * (j + 1), LANES * m:LANES * (m + 1)] = jnp.where(
                        left, cols_ref[a], cols_ref[b])

    return pl.pallas_call(
        build,
        grid=(DEPTH, N_HEADS_GRP),
        in_specs=[pl.BlockSpec((None, None, n_dy + 1, GRID_W, LANES), lambda l, h: (l, h, 0, 0, 0))],
        out_specs=pl.BlockSpec((None, 3, None, TQ, LOCAL_ROWS * GRID_W), lambda l, h: (l, 0, h, 0, 0)),
        out_shape=jax.ShapeDtypeStruct((DEPTH, 3, N_HEADS_GRP, TQ, LOCAL_ROWS * GRID_W), F32),
        compiler_params=_params(2),
        name="local_bias",
    )(cols)


def _w_in_prep_kernel(w_ref, o_ref):
    rows = w_ref.shape[1]
    kpe_end = W_KPE + MLA_ROPE
    o_ref[:, :W_KPE] = w_ref[:W_KPE, :].T.astype(BF16)
    kpe = jnp.concatenate([jnp.zeros((KPE_LANE, rows), F32), w_ref[W_KPE:kpe_end, :],
                           jnp.zeros((LANES - KPE_LANE - MLA_ROPE, rows), F32)], axis=0)
    o_ref[:, W_KPE:W_CZ] = kpe.T.astype(BF16)
    o_ref[:, W_CZ:] = w_ref[kpe_end:, :].T.astype(BF16)


def _w_in_prep(w_in):
    d_in = w_in.shape[-1]
    rows = W_PREP_ROWS
    return pl.pallas_call(
        _w_in_prep_kernel,
        grid=(DEPTH, D_MODEL // rows),
        in_specs=[pl.BlockSpec((None, d_in, rows), lambda l, r: (l, 0, r))],
        out_specs=pl.BlockSpec((None, rows, D_IN_P), lambda l, r: (l, r, 0)),
        out_shape=jax.ShapeDtypeStruct((DEPTH, D_MODEL, D_IN_P), BF16),
        compiler_params=_params(2),
        name="w_in_prep",
    )(jnp.swapaxes(w_in, 1, 2))


def _rope_tables(seq):
    f32 = np.float32
    t = np.arange(seq)
    rows = (t // GRID_W).astype(f32)
    cols = (t % GRID_W).astype(f32)
    half = DA_SUB // 2
    inv = (f32(1.0) / (f32(ROPE_BASE) ** (np.arange(0, half, 2, dtype=f32) / f32(half)))).astype(f32)
    ar = rows[:, None] * inv
    ac = cols[:, None] * inv
    ang = np.concatenate([ar, ar, ac, ac], axis=-1)
    cos, sin = np.cos(ang).astype(f32), np.sin(ang).astype(f32)
    first = (np.arange(DA_SUB) % 16 < 8)[None, :]
    sin_neg = np.where(first, -sin, f32(0.0))
    sin_pos = np.where(first, f32(0.0), sin)
    tile = lambda a: jnp.asarray(np.tile(a, (1, LANES // DA_SUB)), F32)

    def pad(a, fill):
        return jnp.asarray(np.concatenate(
            [np.full((seq, KPE_LANE), fill, f32), a,
             np.full((seq, LANES - KPE_LANE - MLA_ROPE), fill, f32)], axis=1), F32)

    return ([tile(cos), tile(sin_neg), tile(sin_pos)],
            [pad(cos, 1.0), pad(sin_neg, 0.0), pad(sin_pos, 0.0)])


def _pad_heads(w, width, take):
    d, k, _ = w.shape
    w = w.reshape(d, k, N_HEADS_GRP, width)[..., :take]
    return jnp.pad(w, ((0, 0), (0, 0), (0, 0), (0, LANES - take))).reshape(d, k, N_HEADS_GRP * LANES)


def kernel(x_prompt, x_sample, cache_a_k, cache_a_v, cache_c_kv, cache_c_kpe, cache_d_k, cache_d_v,
           c, c_ctx, ada_w, ada_b, norm_g, w_in, da_lambda, da_subln_g, conv_w,
           mla_q_norm_g, mla_w_uq, mla_kv_norm_g, mla_w_ukv, na_rpb, w_out, final_norm_g):
    batch, seq, _ = x_prompt.shape
    dec_batch, dec_seq, _ = x_sample.shape
    past = cache_a_k.shape[2]
    assert dec_seq % TQ == 0 and dec_seq // TQ >= LOCAL_CHUNKS and seq % LANES == 0
    assert dec_seq % IN_PROJ_ROWS == 0 and D_MODEL % W_PREP_ROWS == 0

    w_in_p = _w_in_prep(w_in)
    w_out_b = w_out.astype(BF16)
    wuq = _pad_heads(mla_w_uq, MLA_NOPE + MLA_ROPE, MLA_NOPE + MLA_ROPE).astype(BF16)
    wuk = _pad_heads(mla_w_ukv, MLA_NOPE + MLA_V, MLA_NOPE).astype(BF16)
    wuv = mla_w_ukv.reshape(DEPTH, MLA_KV_RANK, N_HEADS_GRP, MLA_NOPE + MLA_V)[..., MLA_NOPE:]
    wuv = wuv.reshape(DEPTH, MLA_KV_RANK, GROUP_W).astype(BF16)
    subln_g2 = jnp.tile(da_subln_g, (1, LANES // HEAD_DIM)).reshape(DEPTH, 1, LANES)
    ng = norm_g.reshape(DEPTH, 1, D_MODEL)
    qng = mla_q_norm_g.reshape(DEPTH, 1, MLA_Q_RANK)
    kvng = mla_kv_norm_g.reshape(DEPTH, 1, MLA_KV_RANK)
    fg = final_norm_g.reshape(1, D_MODEL)
    tables_a, tables_c = _rope_tables(dec_seq)
    tables = tables_a + tables_c

    cvec8 = jnp.concatenate([c_ctx[None], c, jnp.zeros((8 - 1 - dec_batch, D_MODEL), F32)], axis=0)
    mod = _ada_call(cvec8, ada_w, ada_b).reshape(DEPTH, 8, 3, D_MODEL)

    feat = lambda a: a.transpose(0, 1, 3, 4, 2).reshape(dec_batch, DEPTH, GROUP_W, past)
    cache = _cache_call(feat(cache_a_k), feat(cache_a_v), cache_c_kv,
                        cache_c_kpe.transpose(0, 1, 3, 2), feat(cache_d_k), feat(cache_d_v), wuk, wuv)
    bias = _local_bias_tables(na_rpb, dec_seq // GRID_W)

    xp = x_prompt.reshape(batch * seq, D_MODEL)
    xs = x_sample.reshape(dec_batch * dec_seq, D_MODEL)
    states = []
    for l in range(DEPTH):
        lam_init = 0.8 - 0.6 * math.exp(-0.3 * l)
        final = l == DEPTH - 1

        tok, sq, loc, cv, *states = _in_call(xp, mod, ng, w_in_p, qng, wuq, kvng, wuk, wuv, None,
                                             layer=l, seq=seq, tm=seq, mod_row0=0, states=states,
                                             name=f"ctx_in_{l}")
        xp = _attn_call(xp, mod, tok, sq, loc, cv, None, None, da_lambda, subln_g2, conv_w, w_out_b,
                        fg, layer=l, lam_init=lam_init, nb=batch, seq=seq, tq=seq, mod_row0=0,
                        final=final, name=f"ctx_attn_{l}")

        tok, sq, loc, cv = _in_call(xs, mod, ng, w_in_p, qng, wuq, kvng, wuk, wuv, tables,
                                    layer=l, seq=dec_seq, tm=IN_PROJ_ROWS, mod_row0=1, states=None,
                                    name=f"lat_in_{l}")
        xs = _attn_call(xs, mod, tok, sq, loc, cv, cache, bias, da_lambda, subln_g2, conv_w, w_out_b,
                        fg, layer=l, lam_init=lam_init, nb=dec_batch, seq=dec_seq, tq=TQ, mod_row0=1,
                        final=final, name=f"lat_attn_{l}")

    def heads(a):
        return a.reshape(batch, DEPTH, N_HEADS_GRP, HEAD_DIM, seq).transpose(0, 1, 4, 2, 3)

    s_ak, s_av, s_ckv, s_kpe, s_dk, s_dv = states
    return (xp.reshape(batch, seq, D_MODEL), xs.reshape(dec_batch, dec_seq, D_MODEL),
            heads(s_ak), heads(s_av), s_ckv, s_kpe.transpose(0, 1, 3, 2), heads(s_dk), heads(s_dv))
```

```python
import functools
import math

import jax
import jax.numpy as jnp
import numpy as np
from jax import lax
from jax.experimental import pallas as pl
from jax.experimental.pallas import tpu as pltpu

F32 = jnp.float32
BF16 = jnp.bfloat16

D_MODEL = 1024
DEPTH = 4
GRID_W = 64
HEAD_DIM = 64
GROUP_W = 256
N_HEADS_GRP = 4
DA_SUB = 32
MLA_Q_RANK = 256
MLA_KV_RANK = 128
MLA_NOPE = 64
MLA_ROPE = 32
MLA_V = 64
LOG2E = math.log2(math.e)
MLA_SCALE = (MLA_NOPE + MLA_ROPE) ** -0.5
DA_SCALE = DA_SUB ** -0.5
NA_SCALE = HEAD_DIM ** -0.5
NA_WIN_H = 8
NA_WIN_W = 16
ROPE_BASE = 10000.0
EPS = 1e-6
NEG_INF = -1e30

LANES = 128
BF16_SUBLANES = 16
N_PAIRS = GROUP_W // LANES
VMEM_LIMIT = 56 * 1024 * 1024
IN_PROJ_ROWS = 512
W_PREP_ROWS = 256

W_AQ, W_AK, W_AV, W_AZ = 0, 256, 512, 768
W_BB, W_BC, W_BH, W_BZ = 1024, 1280, 1536, 1792
W_CQ, W_CKV, W_KPE, W_CZ = 2048, 2304, 2432, 2560
W_DQ, W_DK, W_DV, W_DZ = 2816, 3072, 3328, 3584
D_IN_P = 3840
KPE_LANE = 64

T_QA, T_SZA, T_QC, T_SCZ, T_QD, T_SDZ, TOK_W = 0, 128, 256, 512, 640, 768, 896
S_KA, S_VA, S_KC, S_VC, SEQ_W = 0, 128, 256, 512, 640
L_KD, L_VD, LOC_W = 0, 128, 256
X_KA, X_VA, X_KC, X_VC, X_KD, X_VD, CACHE_W = 0, 128, 256, 512, 640, 768, 896
B_BB, B_G, B_SBZ, CONV_W = 0, 256, 512, 768

Q_ROWS = 4
TQ = Q_ROWS * GRID_W
LOCAL_CHUNKS = 3
LOCAL_ROWS = LOCAL_CHUNKS * Q_ROWS


def _dot(a, b):
    return jnp.dot(a, b, preferred_element_type=F32)


def _dot_nt(a, b):
    return lax.dot_general(a, b, (((1,), (1,)), ((), ())), preferred_element_type=F32)


def _silu(z):
    return z * (1.0 / (1.0 + jnp.exp(-z)))


def _rms(x, g):
    return x * lax.rsqrt(jnp.mean(x * x, axis=-1, keepdims=True) + EPS) * g


def _rope128(x, cos, sin_neg, sin_pos):
    return x * cos + pltpu.roll(x, LANES - 8, 1) * sin_neg + pltpu.roll(x, 8, 1) * sin_pos


def _scores(job):
    q, keys, biases, _, _ = job
    blocks = []
    for k, b in zip(keys, biases):
        s = _dot_nt(q, k)
        blocks.append(s if b is None else s + b)
    return blocks


def _weighted_values(job, blocks):
    _, _, _, values, sum_lane = job
    m = functools.reduce(jnp.maximum, [jnp.max(s, axis=-1, keepdims=True) for s in blocks])
    one_hot = _lane_mask(sum_lane, 1)
    o = functools.reduce(jnp.add, [
        _dot(jnp.exp2(s - m).astype(BF16), jnp.where(one_hot, jnp.ones_like(v), v))
        for s, v in zip(blocks, values)])
    l = jnp.sum(jnp.where(one_hot, o, 0.0), axis=-1, keepdims=True)
    return o / l


SCORES_IN_FLIGHT_BYTES = 6 * 1024 * 1024


def _attend_all(jobs, after):
    q, keys = jobs[0][0], jobs[0][1]
    score_bytes = 4 * q.shape[0] * sum(k.shape[0] for k in keys)
    ahead = max(1, min(len(jobs), SCORES_IN_FLIGHT_BYTES // score_bytes))
    outs = []
    pending = [_scores(job) for job in jobs[:ahead]]
    for k, job in enumerate(jobs):
        if k + ahead < len(jobs):
            pending.append(_scores(jobs[k + ahead]))
        outs.append(_weighted_values(job, pending.pop(0)))
        if k in after:
            after[k](outs)
    return outs


def _lane_mask(lo, width):
    lane = lax.broadcasted_iota(jnp.int32, (1, LANES), 1)
    return jnp.logical_and(lane >= lo, lane < lo + width)


def _params(n_axes):
    return pltpu.CompilerParams(dimension_semantics=("arbitrary",) * n_axes,
                                vmem_limit_bytes=VMEM_LIMIT)


def _mod_map(layer, mod_row0, nblk):
    if mod_row0 == 0:
        return lambda i: (layer, 0, 0, 0)
    return lambda i: (layer, mod_row0 + i // nblk, 0, 0)


def _layer_spec(shape, layer):
    zeros = (0,) * len(shape)
    return pl.BlockSpec((None,) + tuple(shape), lambda *_: (layer,) + zeros)


def _ada_kernel(c_ref, w_ref, b_ref, o_ref):
    s = _silu(c_ref[...]).astype(BF16)
    o_ref[...] = _dot(s, w_ref[...].astype(BF16)) + b_ref[...]


def _ada_call(cvec8, ada_w, ada_b):
    nj = 3
    return pl.pallas_call(
        _ada_kernel,
        grid=(DEPTH, nj),
        in_specs=[pl.BlockSpec((8, D_MODEL), lambda l, j: (0, 0)),
                  pl.BlockSpec((None, D_MODEL, D_MODEL), lambda l, j: (l, 0, j)),
                  pl.BlockSpec((None, None, 1, D_MODEL), lambda l, j: (l, j, 0, 0))],
        out_specs=pl.BlockSpec((None, 8, D_MODEL), lambda l, j: (l, 0, j)),
        out_shape=jax.ShapeDtypeStruct((DEPTH, 8, 3 * D_MODEL), F32),
        compiler_params=_params(2),
        name="adaln",
    )(cvec8, ada_w, ada_b.reshape(DEPTH, nj, 1, D_MODEL))


def _cache_kernel(ak_ref, av_ref, ckv_ref, kpe_ref, dk_ref, dv_ref, wuk_ref, wuv_ref, o_ref):
    ckv = ckv_ref[...].astype(BF16)
    kn = _dot(ckv, wuk_ref[...])
    vc = _dot(ckv, wuv_ref[...])
    past = ckv.shape[0]
    kpe = jnp.concatenate([jnp.zeros((KPE_LANE, past), F32), kpe_ref[...],
                           jnp.zeros((LANES - KPE_LANE - MLA_ROPE, past), F32)], axis=0).T
    for hp in range(N_PAIRS):
        half = slice(LANES * hp, LANES * (hp + 1))
        o_ref[hp, :, X_KA:X_KA + LANES] = ak_ref[half, :].T.astype(BF16)
        o_ref[hp, :, X_VA:X_VA + LANES] = av_ref[half, :].T.astype(BF16)
        for e in range(2):
            sl = slice(LANES * (2 * hp + e), LANES * (2 * hp + e + 1))
            o_ref[hp, :, X_KC + LANES * e:X_KC + LANES * (e + 1)] = (kn[:, sl] + kpe).astype(BF16)
        o_ref[hp, :, X_VC:X_VC + LANES] = vc[:, half].astype(BF16)
        o_ref[hp, :, X_KD:X_KD + LANES] = dk_ref[half, :].T.astype(BF16)
        o_ref[hp, :, X_VD:X_VD + LANES] = dv_ref[half, :].T.astype(BF16)


def _cache_call(ca_k, ca_v, c_kv, c_kpe, cd_k, cd_v, wuk, wuv):
    nb, _, past, _ = c_kv.shape
    cache = lambda w: pl.BlockSpec((None, None, past, w), lambda l, b: (b, l, 0, 0))
    feat = lambda rows: pl.BlockSpec((None, None, rows, past), lambda l, b: (b, l, 0, 0))
    return pl.pallas_call(
        _cache_kernel,
        grid=(DEPTH, nb),
        in_specs=[feat(GROUP_W), feat(GROUP_W), cache(MLA_KV_RANK), feat(MLA_ROPE),
                  feat(GROUP_W), feat(GROUP_W),
                  pl.BlockSpec((None, MLA_KV_RANK, 4 * LANES), lambda l, b: (l, 0, 0)),
                  pl.BlockSpec((None, MLA_KV_RANK, GROUP_W), lambda l, b: (l, 0, 0))],
        out_specs=pl.BlockSpec((None, None, N_PAIRS, past, CACHE_W), lambda l, b: (b, l, 0, 0, 0)),
        out_shape=jax.ShapeDtypeStruct((nb, DEPTH, N_PAIRS, past, CACHE_W), BF16),
        compiler_params=_params(2),
        name="cache_prep",
    )(ca_k, ca_v, c_kv, c_kpe, cd_k, cd_v, wuk, wuv)


def _in_kernel(*refs, rope, states):
    it = iter(refs)
    x_ref, mod_ref, ng_ref, w_ref, qng_ref, wuq_ref, kvng_ref, wuk_ref, wuv_ref = (
        next(it) for _ in range(9))
    if rope:
        ta = [next(it)[...] for _ in range(3)]
        tc = [next(it)[...] for _ in range(3)]
    if states == "update":
        for _ in range(6):
            next(it)
    tok_ref, seq_ref, loc_ref, conv_ref = (next(it) for _ in range(4))
    if states:
        sak_ref, sav_ref, sckv_ref, skpe_ref, sdk_ref, sdv_ref = (next(it) for _ in range(6))

    x = x_ref[...]
    h = _rms(x, ng_ref[...]) * (1.0 + mod_ref[1:2, :]) + mod_ref[0:1, :]
    hb = h.astype(BF16)

    def seg(off, n=GROUP_W):
        return _dot(hb, w_ref[:, off:off + n])

    def put_state(ref, val):
        if states == "create":
            ref[0] = val
            for later in range(1, DEPTH):
                ref[later] = jnp.zeros_like(val)
        else:
            ref[...] = val

    def put(ref, width, off, val):
        for hp in range(N_PAIRS):
            ref[:, width * hp + off:width * hp + off + LANES] = (
                val[:, LANES * hp:LANES * (hp + 1)].astype(BF16))

    cqn = _rms(seg(W_CQ), qng_ref[...]).astype(BF16)
    ckvn = _rms(seg(W_CKV, MLA_KV_RANK), kvng_ref[...])
    kpe = seg(W_KPE, LANES)
    if states:
        put_state(sckv_ref, ckvn)
        put_state(skpe_ref, kpe.T[KPE_LANE:KPE_LANE + MLA_ROPE, :])
    ckvb = ckvn.astype(BF16)

    aq, ak, av = seg(W_AQ), seg(W_AK), seg(W_AV)
    if states:
        put_state(sak_ref, ak.T)
        put_state(sav_ref, av.T)
    if rope:
        aq = jnp.concatenate([_rope128(aq[:, :LANES], *ta), _rope128(aq[:, LANES:], *ta)], axis=1)
        ak = jnp.concatenate([_rope128(ak[:, :LANES], *ta), _rope128(ak[:, LANES:], *ta)], axis=1)
    put(tok_ref, TOK_W, T_QA, aq * (DA_SCALE * LOG2E))
    put(seq_ref, SEQ_W, S_KA, ak)
    put(seq_ref, SEQ_W, S_VA, av)
    put(tok_ref, TOK_W, T_SZA, _silu(seg(W_AZ)))

    conv_ref[:, B_BB:B_BB + GROUP_W] = seg(W_BB).astype(BF16)
    conv_ref[:, B_G:B_G + GROUP_W] = (seg(W_BC) * seg(W_BH)).astype(BF16)
    conv_ref[:, B_SBZ:B_SBZ + GROUP_W] = _silu(seg(W_BZ)).astype(BF16)

    dk, dv = seg(W_DK), seg(W_DV)
    if states:
        put_state(sdk_ref, dk.T)
        put_state(sdv_ref, dv.T)
    put(tok_ref, TOK_W, T_QD, seg(W_DQ) * (NA_SCALE * LOG2E))
    put(loc_ref, LOC_W, L_KD, dk)
    put(loc_ref, LOC_W, L_VD, dv)
    put(tok_ref, TOK_W, T_SDZ, _silu(seg(W_DZ)))
    put(tok_ref, TOK_W, T_SCZ, _silu(seg(W_CZ)))

    q = _dot(cqn, wuq_ref[...])
    kn = _dot(ckvb, wuk_ref[...])
    vc = _dot(ckvb, wuv_ref[...])
    kpe_r = _rope128(kpe, *tc) if rope else kpe
    for hd in range(N_HEADS_GRP):
        sl = slice(LANES * hd, LANES * (hd + 1))
        hp, e = divmod(hd, 2)
        qh = q[:, sl]
        if rope:
            qh = _rope128(qh, *tc)
        o = LANES * e
        tok_ref[:, TOK_W * hp + T_QC + o:TOK_W * hp + T_QC + o + LANES] = (
            qh * (MLA_SCALE * LOG2E)).astype(BF16)
        seq_ref[:, SEQ_W * hp + S_KC + o:SEQ_W * hp + S_KC + o + LANES] = (
            kn[:, sl] + kpe_r).astype(BF16)
    put(seq_ref, SEQ_W, S_VC, vc)


def _state_layout(batch, seq, tm):
    nblk = seq // tm
    lblk = lambda layer: DEPTH if layer is None else None
    lidx = lambda layer: 0 if layer is None else layer
    feat = lambda rows: ((batch, DEPTH, rows, seq),
                         lambda layer: (None, lblk(layer), rows, tm),
                         lambda layer: lambda i: (i // nblk, lidx(layer), 0, i % nblk))
    tokm = ((batch, DEPTH, seq, MLA_KV_RANK),
            lambda layer: (None, lblk(layer), tm, MLA_KV_RANK),
            lambda layer: lambda i: (i // nblk, lidx(layer), i % nblk, 0))
    return [feat(GROUP_W), feat(GROUP_W), tokm, feat(MLA_ROPE), feat(GROUP_W), feat(GROUP_W)]


def _in_call(x, mod, norm_g, w_in, q_norm_g, wuq, kv_norm_g, wuk, wuv, tables, *, layer, seq, tm,
             mod_row0, states, name):
    t = x.shape[0]
    nblk = seq // tm
    rope = tables is not None
    in_specs = [
        pl.BlockSpec((tm, D_MODEL), lambda i: (i, 0)),
        pl.BlockSpec((None, None, 3, D_MODEL), _mod_map(layer, mod_row0, nblk)),
        _layer_spec((1, D_MODEL), layer),
        _layer_spec((D_MODEL, D_IN_P), layer),
        _layer_spec((1, MLA_Q_RANK), layer),
        _layer_spec((MLA_Q_RANK, 4 * LANES), layer),
        _layer_spec((1, MLA_KV_RANK), layer),
        _layer_spec((MLA_KV_RANK, 4 * LANES), layer),
        _layer_spec((MLA_KV_RANK, GROUP_W), layer),
    ]
    args = [x, mod, norm_g, w_in, q_norm_g, wuq, kv_norm_g, wuk, wuv]
    if rope:
        in_specs += [pl.BlockSpec((tm, LANES), lambda i: (i % nblk, 0))] * 6
        args += list(tables)
    widths = [N_PAIRS * TOK_W, N_PAIRS * SEQ_W, N_PAIRS * LOC_W, CONV_W]
    out_specs = [pl.BlockSpec((tm, w), lambda i: (i, 0)) for w in widths]
    out_shape = [jax.ShapeDtypeStruct((t, w), BF16) for w in widths]
    aliases, mode = {}, False
    if states is not None:
        mode = "update" if states else "create"
        assert states or layer == 0
        at = layer if states else None
        for k, (shape, block, index) in enumerate(_state_layout(t // seq, seq, tm)):
            out_specs.append(pl.BlockSpec(block(at), index(at)))
            out_shape.append(jax.ShapeDtypeStruct(shape, F32))
            if states:
                aliases[len(args)] = len(widths) + k
                in_specs.append(pl.BlockSpec(memory_space=pl.ANY))
                args.append(states[k])
    return pl.pallas_call(
        functools.partial(_in_kernel, rope=rope, states=mode),
        grid=(t // tm,),
        in_specs=in_specs,
        out_specs=out_specs,
        out_shape=out_shape,
        input_output_aliases=aliases,
        compiler_params=_params(1),
        name=name,
    )(*args)


def _attn_kernel(*refs, cached, lam_init, final):
    it = iter(refs)
    tok_ref, seq_ref = next(it), next(it)
    loc_refs = [next(it) for _ in range(LOCAL_CHUNKS if cached else 1)]
    if cached:
        x_ref, bias_ref = next(it), next(it)
    lam_ref, g_ref = next(it), next(it)
    res_ref, mod_ref, bb_ref, gc_ref, gp_ref, gn_ref, sbz_ref, cw_ref, w_ref, fg_ref = (
        next(it) for _ in range(10))
    o_ref = next(it)

    lv = lam_ref[...]
    lam = (jnp.exp(jnp.sum(lv[0:1] * lv[1:2], keepdims=True))
           - jnp.exp(jnp.sum(lv[2:3] * lv[3:4], keepdims=True)) + lam_init)

    tq = res_ref.shape[0]
    i, nq = pl.program_id(1), pl.num_programs(1)
    has_prev = jnp.where(i != 0, 1.0, 0.0)
    has_next = jnp.where(i != nq - 1, 1.0, 0.0)
    gc = gc_ref[...].astype(F32)
    rows = lax.broadcasted_iota(jnp.int32, (tq, 1), 0)
    halo = gp_ref.shape[0]
    g_prev = jnp.where(rows == 0, gp_ref[halo - 1:halo, :].astype(F32) * has_prev,
                       pltpu.roll(gc, 1, 0))
    g_next = jnp.where(rows == tq - 1, gn_ref[0:1, :].astype(F32) * has_next,
                       pltpu.roll(gc, tq - 1, 0))
    cw = cw_ref[...]
    conv = g_prev * cw[0:1] + gc * cw[1:2] + g_next * cw[2:3]
    yb = (bb_ref[...].astype(F32) * conv * sbz_ref[...].astype(F32)).astype(BF16)

    first = _lane_mask(0, HEAD_DIM)
    proj = []

    def gate(hp, off):
        return tok_ref[:, TOK_W * hp + off:TOK_W * hp + off + LANES].astype(F32)

    def project(group, halves):
        y = jnp.concatenate(halves, axis=1).astype(BF16)
        proj.append(_dot(y, w_ref[GROUP_W * group:GROUP_W * (group + 1), :]))

    def after_a(outs):
        proj.append(_dot(yb, w_ref[GROUP_W:2 * GROUP_W, :]))
        halves = []
        for hp in range(N_PAIRS):
            m00, m01, m10, m11 = outs[4 * hp:4 * hp + 4]
            o = jnp.where(first, m00 - lam * m01, m10 - lam * m11)
            o2 = o * o
            ss0 = jnp.sum(jnp.where(first, o2, 0.0), axis=-1, keepdims=True)
            ss1 = jnp.sum(jnp.where(first, 0.0, o2), axis=-1, keepdims=True)
            ms = jnp.where(first, ss0, ss1) * (1.0 / HEAD_DIM)
            y = o * lax.rsqrt(ms + EPS) * g_ref[...] * (1.0 - lam_init)
            halves.append(y * gate(hp, T_SZA))
        project(0, halves)

    def after_group(group, base, off):
        def run(outs):
            project(group, [jnp.where(first, outs[base + 2 * hp], outs[base + 2 * hp + 1])
                            * gate(hp, off) for hp in range(N_PAIRS)])
        return run

    jobs_a, jobs_c, jobs_d = [], [], []
    for hp in range(N_PAIRS):
        def col(ref, width, off, w=LANES, hp=hp):
            return ref[:, width * hp + off:width * hp + off + w]

        xcol = lambda off, hp=hp: x_ref[hp, :, off:off + LANES]
        qa = col(tok_ref, TOK_W, T_QA)
        keys = [col(seq_ref, SEQ_W, S_KA)] + ([xcol(X_KA)] if cached else [])
        vals = [col(seq_ref, SEQ_W, S_VA)] + ([xcol(X_VA)] if cached else [])
        nob = [None] * len(keys)
        for e in range(2):
            for c in range(2):
                qm = jnp.where(_lane_mask(HEAD_DIM * e + DA_SUB * c, DA_SUB), qa, jnp.zeros_like(qa))
                jobs_a.append((qm, keys, nob, vals, HEAD_DIM * (1 - e)))
        vals = [col(seq_ref, SEQ_W, S_VC)] + ([xcol(X_VC)] if cached else [])
        for e in range(2):
            keys = [col(seq_ref, SEQ_W, S_KC + LANES * e)] + (
                [xcol(X_KC + LANES * e)] if cached else [])
            jobs_c.append((col(tok_ref, TOK_W, T_QC + LANES * e), keys, nob, vals,
                           HEAD_DIM * (1 - e)))
        qd = col(tok_ref, TOK_W, T_QD)
        keys = [col(r, LOC_W, L_KD) for r in loc_refs] + ([xcol(X_KD)] if cached else [])
        vals = [col(r, LOC_W, L_VD) for r in loc_refs] + ([xcol(X_VD)] if cached else [])
        for e in range(2):
            qm = jnp.where(_lane_mask(HEAD_DIM * e, HEAD_DIM), qd, jnp.zeros_like(qd))
            biases = [None] * len(keys)
            if cached:
                biases = [bias_ref[2 * hp + e, :, TQ * j:TQ * (j + 1)]
                          for j in range(len(loc_refs))] + [None]
            jobs_d.append((qm, keys, biases, vals, HEAD_DIM * (1 - e)))

    n_a, n = len(jobs_a), 2 * N_PAIRS
    after = {n_a - 1: after_a, n_a + n - 1: after_group(2, n_a, T_SCZ),
             n_a + 2 * n - 1: after_group(3, n_a + n, T_SDZ)}
    _attend_all(jobs_a + jobs_c + jobs_d, after)
    xn = res_ref[...] + mod_ref[2:3, :] * functools.reduce(jnp.add, proj)
    if final:
        xn = _rms(xn, fg_ref[...])
    o_ref[...] = xn


def _attn_call(x, mod, u_tok, u_seq, u_loc, u_conv, cache, bias, lam_vecs, subln_g2, conv_w, w_out,
               final_g, *, layer, lam_init, nb, seq, tq, mod_row0, final, name):
    cached = cache is not None
    nq = seq // tq
    in_specs = [pl.BlockSpec((tq, N_PAIRS * TOK_W), lambda b, i: (b * nq + i, 0)),
                pl.BlockSpec((seq, N_PAIRS * SEQ_W), lambda b, i: (b, 0))]
    args = [u_tok, u_seq]
    if cached:
        past = cache.shape[3]
        base = lambda i: jnp.clip(i - 1, 0, nq - LOCAL_CHUNKS)
        for j in range(LOCAL_CHUNKS):
            in_specs.append(pl.BlockSpec(
                (tq, N_PAIRS * LOC_W), lambda b, i, j=j: (b * nq + base(i) + j, 0)))
            args.append(u_loc)

        def bias_map(b, i):
            pattern = jnp.where(i == 0, 0, jnp.where(i == nq - 1, 2, 1))
            return (layer, pattern, 0, 0, 0)

        in_specs += [pl.BlockSpec((None, None, N_PAIRS, past, CACHE_W),
                                  lambda b, i: (b, layer, 0, 0, 0)),
                     pl.BlockSpec((None, None, N_HEADS_GRP, tq, LOCAL_CHUNKS * tq), bias_map)]
        args += [cache, bias]
    else:
        in_specs.append(pl.BlockSpec((seq, N_PAIRS * LOC_W), lambda b, i: (b, 0)))
        args.append(u_loc)
    in_specs += [_layer_spec((4, DA_SUB), layer), _layer_spec((1, LANES), layer)]
    args += [lam_vecs, subln_g2]

    halo = BF16_SUBLANES
    hb = tq // halo
    last = nb * seq // halo - 1
    blk = lambda b, i: b * nq + i
    conv = lambda off: pl.BlockSpec((tq, GROUP_W), lambda b, i: (blk(b, i), off // GROUP_W))
    mod_row = (lambda b: 0) if mod_row0 == 0 else (lambda b: mod_row0 + b)
    in_specs += [
        pl.BlockSpec((tq, D_MODEL), lambda b, i: (blk(b, i), 0)),
        pl.BlockSpec((None, None, 3, D_MODEL), lambda b, i: (layer, mod_row(b), 0, 0)),
        conv(B_BB), conv(B_G),
        pl.BlockSpec((halo, GROUP_W),
                     lambda b, i: (jnp.maximum(blk(b, i) * hb - 1, 0), B_G // GROUP_W)),
        pl.BlockSpec((halo, GROUP_W),
                     lambda b, i: (jnp.minimum((blk(b, i) + 1) * hb, last), B_G // GROUP_W)),
        conv(B_SBZ),
        _layer_spec((3, GROUP_W), layer),
        _layer_spec((D_MODEL, D_MODEL), layer),
        pl.BlockSpec((1, D_MODEL), lambda b, i: (0, 0)),
    ]
    args += [x, mod, u_conv, u_conv, u_conv, u_conv, u_conv, conv_w, w_out, final_g]
    return pl.pallas_call(
        functools.partial(_attn_kernel, cached=cached, lam_init=lam_init, final=final),
        grid=(nb, nq),
        in_specs=in_specs,
        out_specs=pl.BlockSpec((tq, D_MODEL), lambda b, i: (blk(b, i), 0)),
        out_shape=jax.ShapeDtypeStruct((nb * seq, D_MODEL), F32),
        compiler_params=_params(2),
        name=name,
    )(*args)


def _local_bias_tables(rpb, rows):
    nq = rows // Q_ROWS
    n_dy, n_dx = 2 * NA_WIN_H - 1, 2 * NA_WIN_W - 1
    qc = np.arange(GRID_W)[:, None]
    kc = np.arange(GRID_W)[None, :]
    ws = np.clip(qc - NA_WIN_W // 2, 0, GRID_W - NA_WIN_W)
    col_ok = (kc >= ws) & (kc < ws + NA_WIN_W)
    oh_dx = ((kc - qc + NA_WIN_W - 1)[None] == np.arange(n_dx)[:, None, None]) & col_ok[None]
    tile_idx = []
    for i in (0, 1, nq - 1):
        row0 = Q_ROWS * int(np.clip(i - 1, 0, nq - LOCAL_CHUNKS))
        r = (Q_ROWS * i + np.arange(Q_ROWS))[:, None]
        kr = (row0 + np.arange(LOCAL_ROWS))[None, :]
        rs = np.clip(r - NA_WIN_H // 2, 0, rows - NA_WIN_H)
        ok = (kr >= rs) & (kr < rs + NA_WIN_H)
        tile_idx.append(np.where(ok, kr - r + NA_WIN_H - 1, n_dy))
    tile_idx = np.stack(tile_idx)
    oh_dx2, col_ok2 = np.concatenate([oh_dx, oh_dx], axis=-1), np.concatenate([col_ok, col_ok], axis=-1)
    hi = lax.Precision.HIGHEST
    cols = jnp.einsum("lhyd,dqc->lhyqc", rpb.astype(F32), jnp.asarray(oh_dx2, F32), precision=hi)
    cols = jnp.where(col_ok2, cols * LOG2E, NEG_INF)

    def build(cols_ref, o_ref):
        left = _lane_mask(0, GRID_W)
        masked = jnp.full((GRID_W, LANES), NEG_INF, F32)
        tile = lambda y: masked if y == n_dy else cols_ref[y]
        for p in range(3):
            for j in range(Q_ROWS):
                for m in range(LOCAL_ROWS // 2):
                    a, b = int(tile_idx[p, j, 2 * m]), int(tile_idx[p, j, 2 * m + 1])
                    o_ref[p, GRID_W * j:GRID_W * (j + 1), LANES * m:LANES * (m + 1)] = jnp.where(
                        left, tile(a), tile(b))

    return pl.pallas_call(
        build,
        grid=(DEPTH, N_HEADS_GRP),
        in_specs=[pl.BlockSpec((None, None, n_dy, GRID_W, LANES), lambda l, h: (l, h, 0, 0, 0))],
        out_specs=pl.BlockSpec((None, 3, None, TQ, LOCAL_ROWS * GRID_W), lambda l, h: (l, 0, h, 0, 0)),
        out_shape=jax.ShapeDtypeStruct((DEPTH, 3, N_HEADS_GRP, TQ, LOCAL_ROWS * GRID_W), F32),
        compiler_params=_params(2),
        name="local_bias",
    )(cols)


def _w_in_prep_kernel(w_ref, o_ref):
    rows = w_ref.shape[1]
    kpe_end = W_KPE + MLA_ROPE
    o_ref[:, :W_KPE] = w_ref[:W_KPE, :].T.astype(BF16)
    kpe = jnp.concatenate([jnp.zeros((KPE_LANE, rows), F32), w_ref[W_KPE:kpe_end, :],
                           jnp.zeros((LANES - KPE_LANE - MLA_ROPE, rows), F32)], axis=0)
    o_ref[:, W_KPE:W_CZ] = kpe.T.astype(BF16)
    o_ref[:, W_CZ:] = w_ref[kpe_end:, :].T.astype(BF16)


def _w_in_prep(w_in):
    d_in = w_in.shape[-1]
    rows = W_PREP_ROWS
    return pl.pallas_call(
        _w_in_prep_kernel,
        grid=(DEPTH, D_MODEL // rows),
        in_specs=[pl.BlockSpec((None, d_in, rows), lambda l, r: (l, 0, r))],
        out_specs=pl.BlockSpec((None, rows, D_IN_P), lambda l, r: (l, r, 0)),
        out_shape=jax.ShapeDtypeStruct((DEPTH, D_MODEL, D_IN_P), BF16),
        compiler_params=_params(2),
        name="w_in_prep",
    )(jnp.swapaxes(w_in, 1, 2))


def _rope_tables(seq):
    f32 = np.float32
    t = np.arange(seq)
    rows = (t // GRID_W).astype(f32)
    cols = (t % GRID_W).astype(f32)
    half = DA_SUB // 2
    inv = (f32(1.0) / (f32(ROPE_BASE) ** (np.arange(0, half, 2, dtype=f32) / f32(half)))).astype(f32)
    ar = rows[:, None] * inv
    ac = cols[:, None] * inv
    ang = np.concatenate([ar, ar, ac, ac], axis=-1)
    cos, sin = np.cos(ang).astype(f32), np.sin(ang).astype(f32)
    first = (np.arange(DA_SUB) % 16 < 8)[None, :]
    sin_neg = np.where(first, -sin, f32(0.0))
    sin_pos = np.where(first, f32(0.0), sin)
    tile = lambda a: jnp.asarray(np.tile(a, (1, LANES // DA_SUB)), F32)

    def pad(a, fill):
        return jnp.asarray(np.concatenate(
            [np.full((seq, KPE_LANE), fill, f32), a,
             np.full((seq, LANES - KPE_LANE - MLA_ROPE), fill, f32)], axis=1), F32)

    return ([tile(cos), tile(sin_neg), tile(sin_pos)],
            [pad(cos, 1.0), pad(sin_neg, 0.0), pad(sin_pos, 0.0)])


def _pad_heads(w, width, take):
    d, k, _ = w.shape
    w = w.reshape(d, k, N_HEADS_GRP, width)[..., :take]
    return jnp.pad(w, ((0, 0), (0, 0), (0, 0), (0, LANES - take))).reshape(d, k, N_HEADS_GRP * LANES)


def kernel(x_prompt, x_sample, cache_a_k, cache_a_v, cache_c_kv, cache_c_kpe, cache_d_k, cache_d_v,
           c, c_ctx, ada_w, ada_b, norm_g, w_in, da_lambda, da_subln_g, conv_w,
           mla_q_norm_g, mla_w_uq, mla_kv_norm_g, mla_w_ukv, na_rpb, w_out, final_norm_g):
    batch, seq, _ = x_prompt.shape
    dec_batch, dec_seq, _ = x_sample.shape
    past = cache_a_k.shape[2]
    assert dec_seq % TQ == 0 and dec_seq // TQ >= LOCAL_CHUNKS and seq % LANES == 0
    assert dec_seq % IN_PROJ_ROWS == 0 and D_MODEL % W_PREP_ROWS == 0

    w_in_p = _w_in_prep(w_in)
    w_out_b = w_out.astype(BF16)
    wuq = _pad_heads(mla_w_uq, MLA_NOPE + MLA_ROPE, MLA_NOPE + MLA_ROPE).astype(BF16)
    wuk = _pad_heads(mla_w_ukv, MLA_NOPE + MLA_V, MLA_NOPE).astype(BF16)
    wuv = mla_w_ukv.reshape(DEPTH, MLA_KV_RANK, N_HEADS_GRP, MLA_NOPE + MLA_V)[..., MLA_NOPE:]
    wuv = wuv.reshape(DEPTH, MLA_KV_RANK, GROUP_W).astype(BF16)
    subln_g2 = jnp.tile(da_subln_g, (1, LANES // HEAD_DIM)).reshape(DEPTH, 1, LANES)
    ng = norm_g.reshape(DEPTH, 1, D_MODEL)
    qng = mla_q_norm_g.reshape(DEPTH, 1, MLA_Q_RANK)
    kvng = mla_kv_norm_g.reshape(DEPTH, 1, MLA_KV_RANK)
    fg = final_norm_g.reshape(1, D_MODEL)
    tables_a, tables_c = _rope_tables(dec_seq)
    tables = tables_a + tables_c

    cvec8 = jnp.concatenate([c_ctx[None], c, jnp.zeros((8 - 1 - dec_batch, D_MODEL), F32)], axis=0)
    mod = _ada_call(cvec8, ada_w, ada_b).reshape(DEPTH, 8, 3, D_MODEL)

    feat = lambda a: a.transpose(0, 1, 3, 4, 2).reshape(dec_batch, DEPTH, GROUP_W, past)
    cache = _cache_call(feat(cache_a_k), feat(cache_a_v), cache_c_kv,
                        cache_c_kpe.transpose(0, 1, 3, 2), feat(cache_d_k), feat(cache_d_v), wuk, wuv)
    bias = _local_bias_tables(na_rpb, dec_seq // GRID_W)

    xp = x_prompt.reshape(batch * seq, D_MODEL)
    xs = x_sample.reshape(dec_batch * dec_seq, D_MODEL)
    states = []
    for l in range(DEPTH):
        lam_init = 0.8 - 0.6 * math.exp(-0.3 * l)
        final = l == DEPTH - 1

        tok, sq, loc, cv, *states = _in_call(xp, mod, ng, w_in_p, qng, wuq, kvng, wuk, wuv, None,
                                             layer=l, seq=seq, tm=seq, mod_row0=0, states=states,
                                             name=f"ctx_in_{l}")
        xp = _attn_call(xp, mod, tok, sq, loc, cv, None, None, da_lambda, subln_g2, conv_w, w_out_b,
                        fg, layer=l, lam_init=lam_init, nb=batch, seq=seq, tq=seq, mod_row0=0,
                        final=final, name=f"ctx_attn_{l}")

        tok, sq, loc, cv = _in_call(xs, mod, ng, w_in_p, qng, wuq, kvng, wuk, wuv, tables,
                                    layer=l, seq=dec_seq, tm=IN_PROJ_ROWS, mod_row0=1, states=None,
                                    name=f"lat_in_{l}")
        xs = _attn_call(xs, mod, tok, sq, loc, cv, cache, bias, da_lambda, subln_g2, conv_w, w_out_b,
                        fg, layer=l, lam_init=lam_init, nb=dec_batch, seq=dec_seq, tq=TQ, mod_row0=1,
                        final=final, name=f"lat_attn_{l}")

    def heads(a):
        return a.reshape(batch, DEPTH, N_HEADS_GRP, HEAD_DIM, seq).transpose(0, 1, 4, 2, 3)

    s_ak, s_av, s_ckv, s_kpe, s_dk, s_dv = states
    return (xp.reshape(batch, seq, D_MODEL), xs.reshape(dec_batch, dec_seq, D_MODEL),
            heads(s_ak), heads(s_av), s_ckv, s_kpe.transpose(0, 1, 3, 2), heads(s_dk), heads(s_dv))
```

```python
import functools
import math

import jax
import jax.numpy as jnp
import numpy as np
from jax import lax
from jax.experimental import pallas as pl
from jax.experimental.pallas import tpu as pltpu

F32 = jnp.float32
BF16 = jnp.bfloat16

D_MODEL = 1024
DEPTH = 4
GRID_W = 64
HEAD_DIM = 64
GROUP_W = 256
N_HEADS_GRP = 4
DA_SUB = 32
MLA_Q_RANK = 256
MLA_KV_RANK = 128
MLA_NOPE = 64
MLA_ROPE = 32
MLA_V = 64
LOG2E = math.log2(math.e)
MLA_SCALE = (MLA_NOPE + MLA_ROPE) ** -0.5
DA_SCALE = DA_SUB ** -0.5
NA_SCALE = HEAD_DIM ** -0.5
NA_WIN_H = 8
NA_WIN_W = 16
ROPE_BASE = 10000.0
EPS = 1e-6
NEG_INF = -1e30

LANES = 128
BF16_SUBLANES = 16
N_PAIRS = GROUP_W // LANES
VMEM_LIMIT = 56 * 1024 * 1024
IN_PROJ_ROWS = 512
W_PREP_ROWS = 256

W_AQ, W_AK, W_AV, W_AZ = 0, 256, 512, 768
W_BB, W_BC, W_BH, W_BZ = 1024, 1280, 1536, 1792
W_CQ, W_CKV, W_KPE, W_CZ = 2048, 2304, 2432, 2560
W_DQ, W_DK, W_DV, W_DZ = 2816, 3072, 3328, 3584
D_IN_P = 3840
KPE_LANE = 64

T_QA, T_SZA, T_QC, T_SCZ, T_QD, T_SDZ, TOK_W = 0, 128, 256, 512, 640, 768, 896
S_KA, S_VA, S_KC, S_VC, SEQ_W = 0, 128, 256, 512, 640
L_KD, L_VD, LOC_W = 0, 128, 256
X_KA, X_VA, X_KC, X_VC, X_KD, X_VD, CACHE_W = 0, 128, 256, 512, 640, 768, 896
B_BB, B_G, B_SBZ, CONV_W = 0, 256, 512, 768

Q_ROWS = 4
TQ = Q_ROWS * GRID_W
LOCAL_CHUNKS = 3
LOCAL_ROWS = LOCAL_CHUNKS * Q_ROWS


def _dot(a, b):
    return jnp.dot(a, b, preferred_element_type=F32)


def _dot_nt(a, b):
    return lax.dot_general(a, b, (((1,), (1,)), ((), ())), preferred_element_type=F32)


def _silu(z):
    return z * (1.0 / (1.0 + jnp.exp(-z)))


def _rms(x, g):
    return x * lax.rsqrt(jnp.mean(x * x, axis=-1, keepdims=True) + EPS) * g


def _rope128(x, cos, sin_neg, sin_pos):
    return x * cos + pltpu.roll(x, LANES - 8, 1) * sin_neg + pltpu.roll(x, 8, 1) * sin_pos


def _scores(job):
    q, keys, biases, _, _ = job
    blocks = []
    for k, b in zip(keys, biases):
        s = _dot_nt(q, k)
        blocks.append(s if b is None else s + b)
    return blocks


def _weighted_values(job, blocks):
    _, _, _, values, sum_lane = job
    m = functools.reduce(jnp.maximum, [jnp.max(s, axis=-1, keepdims=True) for s in blocks])
    one_hot = _lane_mask(sum_lane, 1)
    o = functools.reduce(jnp.add, [
        _dot(jnp.exp2(s - m).astype(BF16), jnp.where(one_hot, jnp.ones_like(v), v))
        for s, v in zip(blocks, values)])
    l = jnp.sum(jnp.where(one_hot, o, 0.0), axis=-1, keepdims=True)
    return o / l


SCORES_IN_FLIGHT_BYTES = 6 * 1024 * 1024


def _attend_all(jobs, after):
    q, keys = jobs[0][0], jobs[0][1]
    score_bytes = 4 * q.shape[0] * sum(k.shape[0] for k in keys)
    ahead = max(1, min(len(jobs), SCORES_IN_FLIGHT_BYTES // score_bytes))
    outs = []
    pending = [_scores(job) for job in jobs[:ahead]]
    for k, job in enumerate(jobs):
        if k + ahead < len(jobs):
            pending.append(_scores(jobs[k + ahead]))
        outs.append(_weighted_values(job, pending.pop(0)))
        if k in after:
            after[k](outs)
    return outs


def _lane_mask(lo, width):
    lane = lax.broadcasted_iota(jnp.int32, (1, LANES), 1)
    return jnp.logical_and(lane >= lo, lane < lo + width)


def _params(n_axes):
    return pltpu.CompilerParams(dimension_semantics=("arbitrary",) * n_axes,
                                vmem_limit_bytes=VMEM_LIMIT)


def _layer_spec(shape, layer):
    zeros = (0,) * len(shape)
    return pl.BlockSpec((None,) + tuple(shape), lambda *_: (layer,) + zeros)


def _ada_kernel(c_ref, w_ref, b_ref, o_ref):
    s = _silu(c_ref[...]).astype(BF16)
    o_ref[...] = _dot(s, w_ref[...].astype(BF16)) + b_ref[...]


def _ada_call(cvec8, ada_w, ada_b):
    nj = 3
    return pl.pallas_call(
        _ada_kernel,
        grid=(DEPTH, nj),
        in_specs=[pl.BlockSpec((8, D_MODEL), lambda l, j: (0, 0)),
                  pl.BlockSpec((None, D_MODEL, D_MODEL), lambda l, j: (l, 0, j)),
                  pl.BlockSpec((None, None, 1, D_MODEL), lambda l, j: (l, j, 0, 0))],
        out_specs=pl.BlockSpec((None, 8, D_MODEL), lambda l, j: (l, 0, j)),
        out_shape=jax.ShapeDtypeStruct((DEPTH, 8, 3 * D_MODEL), F32),
        compiler_params=_params(2),
        name="adaln",
    )(cvec8, ada_w, ada_b.reshape(DEPTH, nj, 1, D_MODEL))


def _cache_kernel(ak_ref, av_ref, ckv_ref, kpe_ref, dk_ref, dv_ref, wuk_ref, wuv_ref, o_ref):
    ckv = ckv_ref[...].astype(BF16)
    kn = _dot(ckv, wuk_ref[...])
    vc = _dot(ckv, wuv_ref[...])
    past = ckv.shape[0]
    kpe = jnp.concatenate([jnp.zeros((KPE_LANE, past), F32), kpe_ref[...],
                           jnp.zeros((LANES - KPE_LANE - MLA_ROPE, past), F32)], axis=0).T
    for hp in range(N_PAIRS):
        half = slice(LANES * hp, LANES * (hp + 1))
        o_ref[hp, :, X_KA:X_KA + LANES] = ak_ref[half, :].T.astype(BF16)
        o_ref[hp, :, X_VA:X_VA + LANES] = av_ref[half, :].T.astype(BF16)
        for e in range(2):
            sl = slice(LANES * (2 * hp + e), LANES * (2 * hp + e + 1))
            o_ref[hp, :, X_KC + LANES * e:X_KC + LANES * (e + 1)] = (kn[:, sl] + kpe).astype(BF16)
        o_ref[hp, :, X_VC:X_VC + LANES] = vc[:, half].astype(BF16)
        o_ref[hp, :, X_KD:X_KD + LANES] = dk_ref[half, :].T.astype(BF16)
        o_ref[hp, :, X_VD:X_VD + LANES] = dv_ref[half, :].T.astype(BF16)


def _cache_call(ca_k, ca_v, c_kv, c_kpe, cd_k, cd_v, wuk, wuv):
    nb, _, past, _ = c_kv.shape
    cache = lambda w: pl.BlockSpec((None, None, past, w), lambda l, b: (b, l, 0, 0))
    feat = lambda rows: pl.BlockSpec((None, None, rows, past), lambda l, b: (b, l, 0, 0))
    return pl.pallas_call(
        _cache_kernel,
        grid=(DEPTH, nb),
        in_specs=[feat(GROUP_W), feat(GROUP_W), cache(MLA_KV_RANK), feat(MLA_ROPE),
                  feat(GROUP_W), feat(GROUP_W),
                  pl.BlockSpec((None, MLA_KV_RANK, 4 * LANES), lambda l, b: (l, 0, 0)),
                  pl.BlockSpec((None, MLA_KV_RANK, GROUP_W), lambda l, b: (l, 0, 0))],
        out_specs=pl.BlockSpec((None, None, N_PAIRS, past, CACHE_W), lambda l, b: (b, l, 0, 0, 0)),
        out_shape=jax.ShapeDtypeStruct((nb, DEPTH, N_PAIRS, past, CACHE_W), BF16),
        compiler_params=_params(2),
        name="cache_prep",
    )(ca_k, ca_v, c_kv, c_kpe, cd_k, cd_v, wuk, wuv)


def _in_kernel(*refs, rope, states):
    it = iter(refs)
    x_ref, mod_ref, ng_ref, w_ref, qng_ref, wuq_ref, kvng_ref, wuk_ref, wuv_ref = (
        next(it) for _ in range(9))
    if rope:
        ta = [next(it)[...] for _ in range(3)]
        tc = [next(it)[...] for _ in range(3)]
    if states == "update":
        for _ in range(6):
            next(it)
    tok_ref, seq_ref, loc_ref, conv_ref = (next(it) for _ in range(4))
    if states:
        sak_ref, sav_ref, sckv_ref, skpe_ref, sdk_ref, sdv_ref = (next(it) for _ in range(6))

    x = x_ref[...]
    h = _rms(x, ng_ref[...]) * (1.0 + mod_ref[1:2, :]) + mod_ref[0:1, :]
    hb = h.astype(BF16)

    def seg(off, n=GROUP_W):
        return _dot(hb, w_ref[:, off:off + n])

    def put_state(ref, val):
        if states == "create":
            ref[0] = val
            for later in range(1, DEPTH):
                ref[later] = jnp.zeros_like(val)
        else:
            ref[...] = val

    def put(ref, width, off, val):
        for hp in range(N_PAIRS):
            ref[:, width * hp + off:width * hp + off + LANES] = (
                val[:, LANES * hp:LANES * (hp + 1)].astype(BF16))

    cqn = _rms(seg(W_CQ), qng_ref[...]).astype(BF16)
    ckvn = _rms(seg(W_CKV, MLA_KV_RANK), kvng_ref[...])
    kpe = seg(W_KPE, LANES)
    if states:
        put_state(sckv_ref, ckvn)
        put_state(skpe_ref, kpe.T[KPE_LANE:KPE_LANE + MLA_ROPE, :])
    ckvb = ckvn.astype(BF16)

    aq, ak, av = seg(W_AQ), seg(W_AK), seg(W_AV)
    if states:
        put_state(sak_ref, ak.T)
        put_state(sav_ref, av.T)
    if rope:
        aq = jnp.concatenate([_rope128(aq[:, :LANES], *ta), _rope128(aq[:, LANES:], *ta)], axis=1)
        ak = jnp.concatenate([_rope128(ak[:, :LANES], *ta), _rope128(ak[:, LANES:], *ta)], axis=1)
    put(tok_ref, TOK_W, T_QA, aq * (DA_SCALE * LOG2E))
    put(seq_ref, SEQ_W, S_KA, ak)
    put(seq_ref, SEQ_W, S_VA, av)
    put(tok_ref, TOK_W, T_SZA, _silu(seg(W_AZ)))

    conv_ref[:, B_BB:B_BB + GROUP_W] = seg(W_BB).astype(BF16)
    conv_ref[:, B_G:B_G + GROUP_W] = (seg(W_BC) * seg(W_BH)).astype(BF16)
    conv_ref[:, B_SBZ:B_SBZ + GROUP_W] = _silu(seg(W_BZ)).astype(BF16)

    dk, dv = seg(W_DK), seg(W_DV)
    if states:
        put_state(sdk_ref, dk.T)
        put_state(sdv_ref, dv.T)
    put(tok_ref, TOK_W, T_QD, seg(W_DQ) * (NA_SCALE * LOG2E))
    put(loc_ref, LOC_W, L_KD, dk)
    put(loc_ref, LOC_W, L_VD, dv)
    put(tok_ref, TOK_W, T_SDZ, _silu(seg(W_DZ)))
    put(tok_ref, TOK_W, T_SCZ, _silu(seg(W_CZ)))

    q = _dot(cqn, wuq_ref[...])
    kn = _dot(ckvb, wuk_ref[...])
    vc = _dot(ckvb, wuv_ref[...])
    kpe_r = _rope128(kpe, *tc) if rope else kpe
    for hd in range(N_HEADS_GRP):
        sl = slice(LANES * hd, LANES * (hd + 1))
        hp, e = divmod(hd, 2)
        qh = q[:, sl]
        if rope:
            qh = _rope128(qh, *tc)
        o = LANES * e
        tok_ref[:, TOK_W * hp + T_QC + o:TOK_W * hp + T_QC + o + LANES] = (
            qh * (MLA_SCALE * LOG2E)).astype(BF16)
        seq_ref[:, SEQ_W * hp + S_KC + o:SEQ_W * hp + S_KC + o + LANES] = (
            kn[:, sl] + kpe_r).astype(BF16)
    put(seq_ref, SEQ_W, S_VC, vc)


def _state_layout(batch, seq, tm):
    nblk = seq // tm
    lblk = lambda layer: DEPTH if layer is None else None
    lidx = lambda layer: 0 if layer is None else layer
    feat = lambda rows: ((batch, DEPTH, rows, seq),
                         lambda layer: (None, lblk(layer), rows, tm),
                         lambda layer: lambda i: (i // nblk, lidx(layer), 0, i % nblk))
    tokm = ((batch, DEPTH, seq, MLA_KV_RANK),
            lambda layer: (None, lblk(layer), tm, MLA_KV_RANK),
            lambda layer: lambda i: (i // nblk, lidx(layer), i % nblk, 0))
    return [feat(GROUP_W), feat(GROUP_W), tokm, feat(MLA_ROPE), feat(GROUP_W), feat(GROUP_W)]


def _in_call(x, mod, norm_g, w_in, q_norm_g, wuq, kv_norm_g, wuk, wuv, tables, *, layer, seq, tm,
             mod_row0, name):
    t = x.shape[0]
    nblk = seq // tm
    in_specs = [
        pl.BlockSpec((tm, D_MODEL), lambda i: (i, 0)),
        pl.BlockSpec((None, None, 3, D_MODEL), lambda i: (layer, mod_row0 + i // nblk, 0, 0)),
        _layer_spec((1, D_MODEL), layer),
        _layer_spec((D_MODEL, D_IN_P), layer),
        _layer_spec((1, MLA_Q_RANK), layer),
        _layer_spec((MLA_Q_RANK, 4 * LANES), layer),
        _layer_spec((1, MLA_KV_RANK), layer),
        _layer_spec((MLA_KV_RANK, 4 * LANES), layer),
        _layer_spec((MLA_KV_RANK, GROUP_W), layer),
    ]
    in_specs += [pl.BlockSpec((tm, LANES), lambda i: (i % nblk, 0))] * len(tables)
    args = [x, mod, norm_g, w_in, q_norm_g, wuq, kv_norm_g, wuk, wuv] + list(tables)
    widths = [N_PAIRS * TOK_W, N_PAIRS * SEQ_W, N_PAIRS * LOC_W, CONV_W]
    return pl.pallas_call(
        functools.partial(_in_kernel, rope=True, states=False),
        grid=(t // tm,),
        in_specs=in_specs,
        out_specs=[pl.BlockSpec((tm, w), lambda i: (i, 0)) for w in widths],
        out_shape=[jax.ShapeDtypeStruct((t, w), BF16) for w in widths],
        compiler_params=_params(1),
        name=name,
    )(*args)


def _attn_kernel(*refs, cached, lam_init, final):
    it = iter(refs)
    tok_ref, seq_ref = next(it), next(it)
    loc_refs = [next(it) for _ in range(LOCAL_CHUNKS if cached else 1)]
    if cached:
        x_ref, bias_ref = next(it), next(it)
    lam_ref, g_ref = next(it), next(it)
    res_ref, mod_ref, bb_ref, gc_ref = (next(it) for _ in range(4))
    if cached:
        gp_ref, gn_ref = next(it), next(it)
    sbz_ref, cw_ref, w_ref, fg_ref = (next(it) for _ in range(4))
    o_ref = next(it)

    lv = lam_ref[...]
    lam = (jnp.exp(jnp.sum(lv[0:1] * lv[1:2], keepdims=True))
           - jnp.exp(jnp.sum(lv[2:3] * lv[3:4], keepdims=True)) + lam_init)

    tq = res_ref.shape[0]
    gc = gc_ref[...].astype(F32)
    rows = lax.broadcasted_iota(jnp.int32, (tq, 1), 0)
    if cached:
        i, nq = pl.program_id(1), pl.num_programs(1)
        halo = gp_ref.shape[0]
        edge_prev = gp_ref[halo - 1:halo, :].astype(F32) * jnp.where(i != 0, 1.0, 0.0)
        edge_next = gn_ref[0:1, :].astype(F32) * jnp.where(i != nq - 1, 1.0, 0.0)
    else:
        edge_prev = edge_next = 0.0
    g_prev = jnp.where(rows == 0, edge_prev, pltpu.roll(gc, 1, 0))
    g_next = jnp.where(rows == tq - 1, edge_next, pltpu.roll(gc, tq - 1, 0))
    cw = cw_ref[...]
    conv = g_prev * cw[0:1] + gc * cw[1:2] + g_next * cw[2:3]
    yb = (bb_ref[...].astype(F32) * conv * sbz_ref[...].astype(F32)).astype(BF16)

    first = _lane_mask(0, HEAD_DIM)
    proj = []

    def gate(hp, off):
        return tok_ref[:, TOK_W * hp + off:TOK_W * hp + off + LANES].astype(F32)

    def project(group, halves):
        y = jnp.concatenate(halves, axis=1).astype(BF16)
        proj.append(_dot(y, w_ref[GROUP_W * group:GROUP_W * (group + 1), :]))

    def after_a(outs):
        proj.append(_dot(yb, w_ref[GROUP_W:2 * GROUP_W, :]))
        halves = []
        for hp in range(N_PAIRS):
            m00, m01, m10, m11 = outs[4 * hp:4 * hp + 4]
            o = jnp.where(first, m00 - lam * m01, m10 - lam * m11)
            o2 = o * o
            ss0 = jnp.sum(jnp.where(first, o2, 0.0), axis=-1, keepdims=True)
            ss1 = jnp.sum(jnp.where(first, 0.0, o2), axis=-1, keepdims=True)
            ms = jnp.where(first, ss0, ss1) * (1.0 / HEAD_DIM)
            y = o * lax.rsqrt(ms + EPS) * g_ref[...] * (1.0 - lam_init)
            halves.append(y * gate(hp, T_SZA))
        project(0, halves)

    def after_group(group, base, off):
        def run(outs):
            project(group, [jnp.where(first, outs[base + 2 * hp], outs[base + 2 * hp + 1])
                            * gate(hp, off) for hp in range(N_PAIRS)])
        return run

    jobs_a, jobs_c, jobs_d = [], [], []
    for hp in range(N_PAIRS):
        def col(ref, width, off, w=LANES, hp=hp):
            return ref[:, width * hp + off:width * hp + off + w]

        xcol = lambda off, hp=hp: x_ref[hp, :, off:off + LANES]
        qa = col(tok_ref, TOK_W, T_QA)
        keys = [col(seq_ref, SEQ_W, S_KA)] + ([xcol(X_KA)] if cached else [])
        vals = [col(seq_ref, SEQ_W, S_VA)] + ([xcol(X_VA)] if cached else [])
        nob = [None] * len(keys)
        for e in range(2):
            for c in range(2):
                qm = jnp.where(_lane_mask(HEAD_DIM * e + DA_SUB * c, DA_SUB), qa, jnp.zeros_like(qa))
                jobs_a.append((qm, keys, nob, vals, HEAD_DIM * (1 - e)))
        vals = [col(seq_ref, SEQ_W, S_VC)] + ([xcol(X_VC)] if cached else [])
        for e in range(2):
            keys = [col(seq_ref, SEQ_W, S_KC + LANES * e)] + (
                [xcol(X_KC + LANES * e)] if cached else [])
            jobs_c.append((col(tok_ref, TOK_W, T_QC + LANES * e), keys, nob, vals,
                           HEAD_DIM * (1 - e)))
        qd = col(tok_ref, TOK_W, T_QD)
        keys = [col(r, LOC_W, L_KD) for r in loc_refs] + ([xcol(X_KD)] if cached else [])
        vals = [col(r, LOC_W, L_VD) for r in loc_refs] + ([xcol(X_VD)] if cached else [])
        for e in range(2):
            qm = jnp.where(_lane_mask(HEAD_DIM * e, HEAD_DIM), qd, jnp.zeros_like(qd))
            biases = [None] * len(keys)
            if cached:
                biases = [bias_ref[2 * hp + e, :, TQ * j:TQ * (j + 1)]
                          for j in range(len(loc_refs))] + [None]
            jobs_d.append((qm, keys, biases, vals, HEAD_DIM * (1 - e)))

    n_a, n = len(jobs_a), 2 * N_PAIRS
    after = {n_a - 1: after_a, n_a + n - 1: after_group(2, n_a, T_SCZ),
             n_a + 2 * n - 1: after_group(3, n_a + n, T_SDZ)}
    _attend_all(jobs_a + jobs_c + jobs_d, after)
    xn = res_ref[...] + mod_ref[2:3, :] * functools.reduce(jnp.add, proj)
    if final:
        xn = _rms(xn, fg_ref[...])
    o_ref[...] = xn


def _ctx_layer_kernel(*refs, lam_init, final, states):
    it = iter(refs)
    in_refs = [next(it) for _ in range(9)]
    alias_refs = [next(it) for _ in range(6)] if states == "update" else []
    lam_ref, g_ref, cw_ref, w_out_ref, fg_ref = (next(it) for _ in range(5))
    o_ref = next(it)
    state_refs = [next(it) for _ in range(6)]
    tok_s, seq_s, loc_s, conv_s = (next(it) for _ in range(4))
    _in_kernel(*in_refs, *alias_refs, tok_s, seq_s, loc_s, conv_s, *state_refs,
               rope=False, states=states)
    view = lambda off: conv_s.at[:, off:off + GROUP_W]
    _attn_kernel(tok_s, seq_s, loc_s, lam_ref, g_ref, in_refs[0], in_refs[1], view(B_BB), view(B_G),
                 view(B_SBZ), cw_ref, w_out_ref, fg_ref, o_ref,
                 cached=False, lam_init=lam_init, final=final)


def _ctx_layer_call(x, mod, norm_g, w_in, q_norm_g, wuq, kv_norm_g, wuk, wuv, lam_vecs, subln_g2,
                    conv_w, w_out, final_g, states, *, layer, lam_init, seq, final, name):
    t = x.shape[0]
    in_specs = [
        pl.BlockSpec((seq, D_MODEL), lambda b: (b, 0)),
        pl.BlockSpec((None, None, 3, D_MODEL), lambda b: (layer, 0, 0, 0)),
        _layer_spec((1, D_MODEL), layer),
        _layer_spec((D_MODEL, D_IN_P), layer),
        _layer_spec((1, MLA_Q_RANK), layer),
        _layer_spec((MLA_Q_RANK, 4 * LANES), layer),
        _layer_spec((1, MLA_KV_RANK), layer),
        _layer_spec((MLA_KV_RANK, 4 * LANES), layer),
        _layer_spec((MLA_KV_RANK, GROUP_W), layer),
    ]
    args = [x, mod, norm_g, w_in, q_norm_g, wuq, kv_norm_g, wuk, wuv]
    mode = "update" if states else "create"
    assert states or layer == 0
    out_specs = [pl.BlockSpec((seq, D_MODEL), lambda b: (b, 0))]
    out_shape = [jax.ShapeDtypeStruct((t, D_MODEL), F32)]
    aliases = {}
    at = layer if states else None
    for k, (shape, block, index) in enumerate(_state_layout(t // seq, seq, seq)):
        out_specs.append(pl.BlockSpec(block(at), index(at)))
        out_shape.append(jax.ShapeDtypeStruct(shape, F32))
        if states:
            aliases[len(args)] = 1 + k
            in_specs.append(pl.BlockSpec(memory_space=pl.ANY))
            args.append(states[k])
    in_specs += [_layer_spec((4, DA_SUB), layer), _layer_spec((1, LANES), layer),
                 _layer_spec((3, GROUP_W), layer), _layer_spec((D_MODEL, D_MODEL), layer),
                 pl.BlockSpec((1, D_MODEL), lambda b: (0, 0))]
    args += [lam_vecs, subln_g2, conv_w, w_out, final_g]
    slabs = [N_PAIRS * TOK_W, N_PAIRS * SEQ_W, N_PAIRS * LOC_W, CONV_W]
    return pl.pallas_call(
        functools.partial(_ctx_layer_kernel, lam_init=lam_init, final=final, states=mode),
        grid=(t // seq,),
        in_specs=in_specs,
        out_specs=out_specs,
        out_shape=out_shape,
        scratch_shapes=[pltpu.VMEM((seq, w), BF16) for w in slabs],
        input_output_aliases=aliases,
        compiler_params=_params(1),
        name=name,
    )(*args)


def _attn_call(x, mod, u_tok, u_seq, u_loc, u_conv, cache, bias, lam_vecs, subln_g2, conv_w, w_out,
               final_g, *, layer, lam_init, nb, seq, tq, mod_row0, final, name):
    nq = seq // tq
    past = cache.shape[3]
    base = lambda i: jnp.clip(i - 1, 0, nq - LOCAL_CHUNKS)

    def bias_map(b, i):
        pattern = jnp.where(i == 0, 0, jnp.where(i == nq - 1, 2, 1))
        return (layer, pattern, 0, 0, 0)

    in_specs = [pl.BlockSpec((tq, N_PAIRS * TOK_W), lambda b, i: (b * nq + i, 0)),
                pl.BlockSpec((seq, N_PAIRS * SEQ_W), lambda b, i: (b, 0))]
    in_specs += [pl.BlockSpec((tq, N_PAIRS * LOC_W), lambda b, i, j=j: (b * nq + base(i) + j, 0))
                 for j in range(LOCAL_CHUNKS)]
    in_specs += [pl.BlockSpec((None, None, N_PAIRS, past, CACHE_W), lambda b, i: (b, layer, 0, 0, 0)),
                 pl.BlockSpec((None, None, N_HEADS_GRP, tq, LOCAL_CHUNKS * tq), bias_map),
                 _layer_spec((4, DA_SUB), layer), _layer_spec((1, LANES), layer)]
    args = [u_tok, u_seq] + [u_loc] * LOCAL_CHUNKS + [cache, bias, lam_vecs, subln_g2]

    halo = BF16_SUBLANES
    hb = tq // halo
    last = nb * seq // halo - 1
    blk = lambda b, i: b * nq + i
    conv = lambda off: pl.BlockSpec((tq, GROUP_W), lambda b, i: (blk(b, i), off // GROUP_W))
    in_specs += [
        pl.BlockSpec((tq, D_MODEL), lambda b, i: (blk(b, i), 0)),
        pl.BlockSpec((None, None, 3, D_MODEL), lambda b, i: (layer, mod_row0 + b, 0, 0)),
        conv(B_BB), conv(B_G),
        pl.BlockSpec((halo, GROUP_W),
                     lambda b, i: (jnp.maximum(blk(b, i) * hb - 1, 0), B_G // GROUP_W)),
        pl.BlockSpec((halo, GROUP_W),
                     lambda b, i: (jnp.minimum((blk(b, i) + 1) * hb, last), B_G // GROUP_W)),
        conv(B_SBZ),
        _layer_spec((3, GROUP_W), layer),
        _layer_spec((D_MODEL, D_MODEL), layer),
        pl.BlockSpec((1, D_MODEL), lambda b, i: (0, 0)),
    ]
    args += [x, mod, u_conv, u_conv, u_conv, u_conv, u_conv, conv_w, w_out, final_g]
    return pl.pallas_call(
        functools.partial(_attn_kernel, cached=True, lam_init=lam_init, final=final),
        grid=(nb, nq),
        in_specs=in_specs,
        out_specs=pl.BlockSpec((tq, D_MODEL), lambda b, i: (blk(b, i), 0)),
        out_shape=jax.ShapeDtypeStruct((nb * seq, D_MODEL), F32),
        compiler_params=_params(2),
        name=name,
    )(*args)


def _local_bias_tables(rpb, rows):
    nq = rows // Q_ROWS
    n_dy, n_dx = 2 * NA_WIN_H - 1, 2 * NA_WIN_W - 1
    qc = np.arange(GRID_W)[:, None]
    kc = np.arange(GRID_W)[None, :]
    ws = np.clip(qc - NA_WIN_W // 2, 0, GRID_W - NA_WIN_W)
    col_ok = (kc >= ws) & (kc < ws + NA_WIN_W)
    oh_dx = ((kc - qc + NA_WIN_W - 1)[None] == np.arange(n_dx)[:, None, None]) & col_ok[None]
    tile_idx = []
    for i in (0, 1, nq - 1):
        row0 = Q_ROWS * int(np.clip(i - 1, 0, nq - LOCAL_CHUNKS))
        r = (Q_ROWS * i + np.arange(Q_ROWS))[:, None]
        kr = (row0 + np.arange(LOCAL_ROWS))[None, :]
        rs = np.clip(r - NA_WIN_H // 2, 0, rows - NA_WIN_H)
        ok = (kr >= rs) & (kr < rs + NA_WIN_H)
        tile_idx.append(np.where(ok, kr - r + NA_WIN_H - 1, n_dy))
    tile_idx = np.stack(tile_idx)
    oh_dx2, col_ok2 = np.concatenate([oh_dx, oh_dx], axis=-1), np.concatenate([col_ok, col_ok], axis=-1)
    hi = lax.Precision.HIGHEST
    cols = jnp.einsum("lhyd,dqc->lhyqc", rpb.astype(F32), jnp.asarray(oh_dx2, F32), precision=hi)
    cols = jnp.where(col_ok2, cols * LOG2E, NEG_INF)

    def build(cols_ref, o_ref):
        left = _lane_mask(0, GRID_W)
        masked = jnp.full((GRID_W, LANES), NEG_INF, F32)
        tile = lambda y: masked if y == n_dy else cols_ref[y]
        for p in range(3):
            for j in range(Q_ROWS):
                for m in range(LOCAL_ROWS // 2):
                    a, b = int(tile_idx[p, j, 2 * m]), int(tile_idx[p, j, 2 * m + 1])
                    o_ref[p, GRID_W * j:GRID_W * (j + 1), LANES * m:LANES * (m + 1)] = jnp.where(
                        left, tile(a), tile(b))

    return pl.pallas_call(
        build,
        grid=(DEPTH, N_HEADS_GRP),
        in_specs=[pl.BlockSpec((None, None, n_dy, GRID_W, LANES), lambda l, h: (l, h, 0, 0, 0))],
        out_specs=pl.BlockSpec((None, 3, None, TQ, LOCAL_ROWS * GRID_W), lambda l, h: (l, 0, h, 0, 0)),
        out_shape=jax.ShapeDtypeStruct((DEPTH, 3, N_HEADS_GRP, TQ, LOCAL_ROWS * GRID_W), F32),
        compiler_params=_params(2),
        name="local_bias",
    )(cols)


def _w_in_prep_kernel(w_ref, o_ref):
    rows = w_ref.shape[1]
    kpe_end = W_KPE + MLA_ROPE
    o_ref[:, :W_KPE] = w_ref[:W_KPE, :].T.astype(BF16)
    kpe = jnp.concatenate([jnp.zeros((KPE_LANE, rows), F32), w_ref[W_KPE:kpe_end, :],
                           jnp.zeros((LANES - KPE_LANE - MLA_ROPE, rows), F32)], axis=0)
    o_ref[:, W_KPE:W_CZ] = kpe.T.astype(BF16)
    o_ref[:, W_CZ:] = w_ref[kpe_end:, :].T.astype(BF16)


def _w_in_prep(w_in):
    d_in = w_in.shape[-1]
    rows = W_PREP_ROWS
    return pl.pallas_call(
        _w_in_prep_kernel,
        grid=(DEPTH, D_MODEL // rows),
        in_specs=[pl.BlockSpec((None, d_in, rows), lambda l, r: (l, 0, r))],
        out_specs=pl.BlockSpec((None, rows, D_IN_P), lambda l, r: (l, r, 0)),
        out_shape=jax.ShapeDtypeStruct((DEPTH, D_MODEL, D_IN_P), BF16),
        compiler_params=_params(2),
        name="w_in_prep",
    )(jnp.swapaxes(w_in, 1, 2))


def _rope_tables(seq):
    f32 = np.float32
    t = np.arange(seq)
    rows = (t // GRID_W).astype(f32)
    cols = (t % GRID_W).astype(f32)
    half = DA_SUB // 2
    inv = (f32(1.0) / (f32(ROPE_BASE) ** (np.arange(0, half, 2, dtype=f32) / f32(half)))).astype(f32)
    ar = rows[:, None] * inv
    ac = cols[:, None] * inv
    ang = np.concatenate([ar, ar, ac, ac], axis=-1)
    cos, sin = np.cos(ang).astype(f32), np.sin(ang).astype(f32)
    first = (np.arange(DA_SUB) % 16 < 8)[None, :]
    sin_neg = np.where(first, -sin, f32(0.0))
    sin_pos = np.where(first, f32(0.0), sin)
    tile = lambda a: jnp.asarray(np.tile(a, (1, LANES // DA_SUB)), F32)

    def pad(a, fill):
        return jnp.asarray(np.concatenate(
            [np.full((seq, KPE_LANE), fill, f32), a,
             np.full((seq, LANES - KPE_LANE - MLA_ROPE), fill, f32)], axis=1), F32)

    return ([tile(cos), tile(sin_neg), tile(sin_pos)],
            [pad(cos, 1.0), pad(sin_neg, 0.0), pad(sin_pos, 0.0)])


def _pad_heads(w, width, take):
    d, k, _ = w.shape
    w = w.reshape(d, k, N_HEADS_GRP, width)[..., :take]
    return jnp.pad(w, ((0, 0), (0, 0), (0, 0), (0, LANES - take))).reshape(d, k, N_HEADS_GRP * LANES)


def kernel(x_prompt, x_sample, cache_a_k, cache_a_v, cache_c_kv, cache_c_kpe, cache_d_k, cache_d_v,
           c, c_ctx, ada_w, ada_b, norm_g, w_in, da_lambda, da_subln_g, conv_w,
           mla_q_norm_g, mla_w_uq, mla_kv_norm_g, mla_w_ukv, na_rpb, w_out, final_norm_g):
    batch, seq, _ = x_prompt.shape
    dec_batch, dec_seq, _ = x_sample.shape
    past = cache_a_k.shape[2]
    assert dec_seq % TQ == 0 and dec_seq // TQ >= LOCAL_CHUNKS and seq % LANES == 0
    assert dec_seq % IN_PROJ_ROWS == 0 and D_MODEL % W_PREP_ROWS == 0

    w_in_p = _w_in_prep(w_in)
    w_out_b = w_out.astype(BF16)
    wuq = _pad_heads(mla_w_uq, MLA_NOPE + MLA_ROPE, MLA_NOPE + MLA_ROPE).astype(BF16)
    wuk = _pad_heads(mla_w_ukv, MLA_NOPE + MLA_V, MLA_NOPE).astype(BF16)
    wuv = mla_w_ukv.reshape(DEPTH, MLA_KV_RANK, N_HEADS_GRP, MLA_NOPE + MLA_V)[..., MLA_NOPE:]
    wuv = wuv.reshape(DEPTH, MLA_KV_RANK, GROUP_W).astype(BF16)
    subln_g2 = jnp.tile(da_subln_g, (1, LANES // HEAD_DIM)).reshape(DEPTH, 1, LANES)
    ng = norm_g.reshape(DEPTH, 1, D_MODEL)
    qng = mla_q_norm_g.reshape(DEPTH, 1, MLA_Q_RANK)
    kvng = mla_kv_norm_g.reshape(DEPTH, 1, MLA_KV_RANK)
    fg = final_norm_g.reshape(1, D_MODEL)
    tables_a, tables_c = _rope_tables(dec_seq)
    tables = tables_a + tables_c

    cvec8 = jnp.concatenate([c_ctx[None], c, jnp.zeros((8 - 1 - dec_batch, D_MODEL), F32)], axis=0)
    mod = _ada_call(cvec8, ada_w, ada_b).reshape(DEPTH, 8, 3, D_MODEL)

    feat = lambda a: a.transpose(0, 1, 3, 4, 2).reshape(dec_batch, DEPTH, GROUP_W, past)
    cache = _cache_call(feat(cache_a_k), feat(cache_a_v), cache_c_kv,
                        cache_c_kpe.transpose(0, 1, 3, 2), feat(cache_d_k), feat(cache_d_v), wuk, wuv)
    bias = _local_bias_tables(na_rpb, dec_seq // GRID_W)

    xp = x_prompt.reshape(batch * seq, D_MODEL)
    xs = x_sample.reshape(dec_batch * dec_seq, D_MODEL)
    states = []
    for l in range(DEPTH):
        lam_init = 0.8 - 0.6 * math.exp(-0.3 * l)
        final = l == DEPTH - 1

        xp, *states = _ctx_layer_call(xp, mod, ng, w_in_p, qng, wuq, kvng, wuk, wuv, da_lambda,
                                      subln_g2, conv_w, w_out_b, fg, states, layer=l,
                                      lam_init=lam_init, seq=seq, final=final, name=f"ctx_layer_{l}")

        tok, sq, loc, cv = _in_call(xs, mod, ng, w_in_p, qng, wuq, kvng, wuk, wuv, tables,
                                    layer=l, seq=dec_seq, tm=IN_PROJ_ROWS, mod_row0=1,
                                    name=f"lat_in_{l}")
        xs = _attn_call(xs, mod, tok, sq, loc, cv, cache, bias, da_lambda, subln_g2, conv_w, w_out_b,
                        fg, layer=l, lam_init=lam_init, nb=dec_batch, seq=dec_seq, tq=TQ, mod_row0=1,
                        final=final, name=f"lat_attn_{l}")

    def heads(a):
        return a.reshape(batch, DEPTH, N_HEADS_GRP, HEAD_DIM, seq).transpose(0, 1, 4, 2, 3)

    s_ak, s_av, s_ckv, s_kpe, s_dk, s_dv = states
    return (xp.reshape(batch, seq, D_MODEL), xs.reshape(dec_batch, dec_seq, D_MODEL),
            heads(s_ak), heads(s_av), s_ckv, s_kpe.transpose(0, 1, 3, 2), heads(s_dk), heads(s_dv))
```

```python
import functools
import math

import jax
import jax.numpy as jnp
import numpy as np
from jax import lax
from jax.experimental import pallas as pl
from jax.experimental.pallas import tpu as pltpu

F32 = jnp.float32
BF16 = jnp.bfloat16

D_MODEL = 1024
DEPTH = 4
GRID_W = 64
HEAD_DIM = 64
GROUP_W = 256
N_HEADS_GRP = 4
DA_SUB = 32
MLA_Q_RANK = 256
MLA_KV_RANK = 128
MLA_NOPE = 64
MLA_ROPE = 32
MLA_V = 64
LOG2E = math.log2(math.e)
MLA_SCALE = (MLA_NOPE + MLA_ROPE) ** -0.5
DA_SCALE = DA_SUB ** -0.5
NA_SCALE = HEAD_DIM ** -0.5
NA_WIN_H = 8
NA_WIN_W = 16
ROPE_BASE = 10000.0
EPS = 1e-6
NEG_INF = -1e30

LANES = 128
BF16_SUBLANES = 16
N_PAIRS = GROUP_W // LANES
VMEM_LIMIT = 56 * 1024 * 1024
IN_PROJ_ROWS = 1024
W_PREP_ROWS = 256

W_AQ, W_AK, W_AV, W_AZ = 0, 256, 512, 768
W_BB, W_BC, W_BH, W_BZ = 1024, 1280, 1536, 1792
W_CQ, W_CKV, W_KPE, W_CZ = 2048, 2304, 2432, 2560
W_DQ, W_DK, W_DV, W_DZ = 2816, 3072, 3328, 3584
D_IN_P = 3840
KPE_LANE = 64

T_QA, T_SZA, T_QC, T_SCZ, T_QD, T_SDZ, TOK_W = 0, 128, 256, 512, 640, 768, 896
S_KA, S_VA, S_KC, S_VC, SEQ_W = 0, 128, 256, 512, 640
L_KD, L_VD, LOC_W = 0, 128, 256
X_KA, X_VA, X_KC, X_VC, X_KD, X_VD, CACHE_W = 0, 128, 256, 512, 640, 768, 896
B_BB, B_G, B_SBZ, CONV_W = 0, 256, 512, 768

Q_ROWS = 4
TQ = Q_ROWS * GRID_W
LOCAL_CHUNKS = 3
LOCAL_ROWS = LOCAL_CHUNKS * Q_ROWS


def _dot(a, b):
    return jnp.dot(a, b, preferred_element_type=F32)


def _dot_nt(a, b):
    return lax.dot_general(a, b, (((1,), (1,)), ((), ())), preferred_element_type=F32)


def _silu(z):
    return z * (1.0 / (1.0 + jnp.exp(-z)))


def _rms(x, g):
    return x * lax.rsqrt(jnp.mean(x * x, axis=-1, keepdims=True) + EPS) * g


def _rope128(x, cos, sin_neg, sin_pos):
    return x * cos + pltpu.roll(x, LANES - 8, 1) * sin_neg + pltpu.roll(x, 8, 1) * sin_pos


def _scores(job):
    q, keys, biases, _, _ = job
    blocks = []
    for k, b in zip(keys, biases):
        s = _dot_nt(q, k)
        blocks.append(s if b is None else s + b)
    return blocks


def _weighted_values(job, blocks):
    _, _, _, values, sum_lane = job
    m = functools.reduce(jnp.maximum, [jnp.max(s, axis=-1, keepdims=True) for s in blocks])
    one_hot = _lane_mask(sum_lane, 1)
    o = functools.reduce(jnp.add, [
        _dot(jnp.exp2(s - m).astype(BF16), jnp.where(one_hot, jnp.ones_like(v), v))
        for s, v in zip(blocks, values)])
    l = jnp.sum(jnp.where(one_hot, o, 0.0), axis=-1, keepdims=True)
    return o / l


SCORES_IN_FLIGHT_BYTES = 6 * 1024 * 1024


def _attend_all(jobs, after):
    q, keys = jobs[0][0], jobs[0][1]
    score_bytes = 4 * q.shape[0] * sum(k.shape[0] for k in keys)
    ahead = max(1, min(len(jobs), SCORES_IN_FLIGHT_BYTES // score_bytes))
    outs = []
    pending = [_scores(job) for job in jobs[:ahead]]
    for k, job in enumerate(jobs):
        if k + ahead < len(jobs):
            pending.append(_scores(jobs[k + ahead]))
        outs.append(_weighted_values(job, pending.pop(0)))
        if k in after:
            after[k](outs)
    return outs


def _lane_mask(lo, width):
    lane = lax.broadcasted_iota(jnp.int32, (1, LANES), 1)
    return jnp.logical_and(lane >= lo, lane < lo + width)


def _params(n_axes):
    return pltpu.CompilerParams(dimension_semantics=("arbitrary",) * n_axes,
                                vmem_limit_bytes=VMEM_LIMIT)


def _layer_spec(shape, layer):
    zeros = (0,) * len(shape)
    return pl.BlockSpec((None,) + tuple(shape), lambda *_: (layer,) + zeros)


def _ada_kernel(c_ref, w_ref, b_ref, o_ref):
    s = _silu(c_ref[...]).astype(BF16)
    o_ref[...] = _dot(s, w_ref[...].astype(BF16)) + b_ref[...]


def _ada_call(cvec8, ada_w, ada_b):
    nj = 3
    return pl.pallas_call(
        _ada_kernel,
        grid=(DEPTH, nj),
        in_specs=[pl.BlockSpec((8, D_MODEL), lambda l, j: (0, 0)),
                  pl.BlockSpec((None, D_MODEL, D_MODEL), lambda l, j: (l, 0, j)),
                  pl.BlockSpec((None, None, 1, D_MODEL), lambda l, j: (l, j, 0, 0))],
        out_specs=pl.BlockSpec((None, 8, D_MODEL), lambda l, j: (l, 0, j)),
        out_shape=jax.ShapeDtypeStruct((DEPTH, 8, 3 * D_MODEL), F32),
        compiler_params=_params(2),
        name="adaln",
    )(cvec8, ada_w, ada_b.reshape(DEPTH, nj, 1, D_MODEL))


def _cache_kernel(ak_ref, av_ref, ckv_ref, kpe_ref, dk_ref, dv_ref, wuk_ref, wuv_ref, o_ref):
    ckv = ckv_ref[...].astype(BF16)
    kn = _dot(ckv, wuk_ref[...])
    vc = _dot(ckv, wuv_ref[...])
    past = ckv.shape[0]
    kpe = jnp.concatenate([jnp.zeros((KPE_LANE, past), F32), kpe_ref[...],
                           jnp.zeros((LANES - KPE_LANE - MLA_ROPE, past), F32)], axis=0).T
    for hp in range(N_PAIRS):
        half = slice(LANES * hp, LANES * (hp + 1))
        o_ref[hp, :, X_KA:X_KA + LANES] = ak_ref[half, :].T.astype(BF16)
        o_ref[hp, :, X_VA:X_VA + LANES] = av_ref[half, :].T.astype(BF16)
        for e in range(2):
            sl = slice(LANES * (2 * hp + e), LANES * (2 * hp + e + 1))
            o_ref[hp, :, X_KC + LANES * e:X_KC + LANES * (e + 1)] = (kn[:, sl] + kpe).astype(BF16)
        o_ref[hp, :, X_VC:X_VC + LANES] = vc[:, half].astype(BF16)
        o_ref[hp, :, X_KD:X_KD + LANES] = dk_ref[half, :].T.astype(BF16)
        o_ref[hp, :, X_VD:X_VD + LANES] = dv_ref[half, :].T.astype(BF16)


def _cache_call(ca_k, ca_v, c_kv, c_kpe, cd_k, cd_v, wuk, wuv):
    nb, _, past, _ = c_kv.shape
    cache = lambda w: pl.BlockSpec((None, None, past, w), lambda l, b: (b, l, 0, 0))
    feat = lambda rows: pl.BlockSpec((None, None, rows, past), lambda l, b: (b, l, 0, 0))
    return pl.pallas_call(
        _cache_kernel,
        grid=(DEPTH, nb),
        in_specs=[feat(GROUP_W), feat(GROUP_W), cache(MLA_KV_RANK), feat(MLA_ROPE),
                  feat(GROUP_W), feat(GROUP_W),
                  pl.BlockSpec((None, MLA_KV_RANK, 4 * LANES), lambda l, b: (l, 0, 0)),
                  pl.BlockSpec((None, MLA_KV_RANK, GROUP_W), lambda l, b: (l, 0, 0))],
        out_specs=pl.BlockSpec((None, None, N_PAIRS, past, CACHE_W), lambda l, b: (b, l, 0, 0, 0)),
        out_shape=jax.ShapeDtypeStruct((nb, DEPTH, N_PAIRS, past, CACHE_W), BF16),
        compiler_params=_params(2),
        name="cache_prep",
    )(ca_k, ca_v, c_kv, c_kpe, cd_k, cd_v, wuk, wuv)


def _in_kernel(*refs, rope, states):
    it = iter(refs)
    x_ref, mod_ref, ng_ref, w_ref, qng_ref, wuq_ref, kvng_ref, wuk_ref, wuv_ref = (
        next(it) for _ in range(9))
    if rope:
        ta = [next(it)[...] for _ in range(3)]
        tc = [next(it)[...] for _ in range(3)]
    if states == "update":
        for _ in range(6):
            next(it)
    tok_ref, seq_ref, loc_ref, conv_ref = (next(it) for _ in range(4))
    if states:
        sak_ref, sav_ref, sckv_ref, skpe_ref, sdk_ref, sdv_ref = (next(it) for _ in range(6))

    x = x_ref[...]
    h = _rms(x, ng_ref[...]) * (1.0 + mod_ref[1:2, :]) + mod_ref[0:1, :]
    hb = h.astype(BF16)

    def seg(off, n=GROUP_W):
        return _dot(hb, w_ref[:, off:off + n])

    def put_state(ref, val):
        if states == "create":
            ref[0] = val
            for later in range(1, DEPTH):
                ref[later] = jnp.zeros_like(val)
        else:
            ref[...] = val

    def put(ref, width, off, val):
        for hp in range(N_PAIRS):
            ref[:, width * hp + off:width * hp + off + LANES] = (
                val[:, LANES * hp:LANES * (hp + 1)].astype(BF16))

    cqn = _rms(seg(W_CQ), qng_ref[...]).astype(BF16)
    ckvn = _rms(seg(W_CKV, MLA_KV_RANK), kvng_ref[...])
    kpe = seg(W_KPE, LANES)
    if states:
        put_state(sckv_ref, ckvn)
        put_state(skpe_ref, kpe.T[KPE_LANE:KPE_LANE + MLA_ROPE, :])
    ckvb = ckvn.astype(BF16)

    aq, ak, av = seg(W_AQ), seg(W_AK), seg(W_AV)
    if states:
        put_state(sak_ref, ak.T)
        put_state(sav_ref, av.T)
    if rope:
        aq = jnp.concatenate([_rope128(aq[:, :LANES], *ta), _rope128(aq[:, LANES:], *ta)], axis=1)
        ak = jnp.concatenate([_rope128(ak[:, :LANES], *ta), _rope128(ak[:, LANES:], *ta)], axis=1)
    put(tok_ref, TOK_W, T_QA, aq * (DA_SCALE * LOG2E))
    put(seq_ref, SEQ_W, S_KA, ak)
    put(seq_ref, SEQ_W, S_VA, av)
    put(tok_ref, TOK_W, T_SZA, _silu(seg(W_AZ)))

    conv_ref[:, B_BB:B_BB + GROUP_W] = seg(W_BB).astype(BF16)
    conv_ref[:, B_G:B_G + GROUP_W] = (seg(W_BC) * seg(W_BH)).astype(BF16)
    conv_ref[:, B_SBZ:B_SBZ + GROUP_W] = _silu(seg(W_BZ)).astype(BF16)

    dk, dv = seg(W_DK), seg(W_DV)
    if states:
        put_state(sdk_ref, dk.T)
        put_state(sdv_ref, dv.T)
    put(tok_ref, TOK_W, T_QD, seg(W_DQ) * (NA_SCALE * LOG2E))
    put(loc_ref, LOC_W, L_KD, dk)
    put(loc_ref, LOC_W, L_VD, dv)
    put(tok_ref, TOK_W, T_SDZ, _silu(seg(W_DZ)))
    put(tok_ref, TOK_W, T_SCZ, _silu(seg(W_CZ)))

    q = _dot(cqn, wuq_ref[...])
    kn = _dot(ckvb, wuk_ref[...])
    vc = _dot(ckvb, wuv_ref[...])
    kpe_r = _rope128(kpe, *tc) if rope else kpe
    for hd in range(N_HEADS_GRP):
        sl = slice(LANES * hd, LANES * (hd + 1))
        hp, e = divmod(hd, 2)
        qh = q[:, sl]
        if rope:
            qh = _rope128(qh, *tc)
        o = LANES * e
        tok_ref[:, TOK_W * hp + T_QC + o:TOK_W * hp + T_QC + o + LANES] = (
            qh * (MLA_SCALE * LOG2E)).astype(BF16)
        seq_ref[:, SEQ_W * hp + S_KC + o:SEQ_W * hp + S_KC + o + LANES] = (
            kn[:, sl] + kpe_r).astype(BF16)
    put(seq_ref, SEQ_W, S_VC, vc)


def _state_layout(batch, seq, tm):
    nblk = seq // tm
    lblk = lambda layer: DEPTH if layer is None else None
    lidx = lambda layer: 0 if layer is None else layer
    feat = lambda rows: ((batch, DEPTH, rows, seq),
                         lambda layer: (None, lblk(layer), rows, tm),
                         lambda layer: lambda i: (i // nblk, lidx(layer), 0, i % nblk))
    tokm = ((batch, DEPTH, seq, MLA_KV_RANK),
            lambda layer: (None, lblk(layer), tm, MLA_KV_RANK),
            lambda layer: lambda i: (i // nblk, lidx(layer), i % nblk, 0))
    return [feat(GROUP_W), feat(GROUP_W), tokm, feat(MLA_ROPE), feat(GROUP_W), feat(GROUP_W)]


def _in_call(x, mod, norm_g, w_in, q_norm_g, wuq, kv_norm_g, wuk, wuv, tables, *, layer, seq, tm,
             mod_row0, name):
    t = x.shape[0]
    nblk = seq // tm
    in_specs = [
        pl.BlockSpec((tm, D_MODEL), lambda i: (i, 0)),
        pl.BlockSpec((None, None, 3, D_MODEL), lambda i: (layer, mod_row0 + i // nblk, 0, 0)),
        _layer_spec((1, D_MODEL), layer),
        _layer_spec((D_MODEL, D_IN_P), layer),
        _layer_spec((1, MLA_Q_RANK), layer),
        _layer_spec((MLA_Q_RANK, 4 * LANES), layer),
        _layer_spec((1, MLA_KV_RANK), layer),
        _layer_spec((MLA_KV_RANK, 4 * LANES), layer),
        _layer_spec((MLA_KV_RANK, GROUP_W), layer),
    ]
    in_specs += [pl.BlockSpec((tm, LANES), lambda i: (i % nblk, 0))] * len(tables)
    args = [x, mod, norm_g, w_in, q_norm_g, wuq, kv_norm_g, wuk, wuv] + list(tables)
    widths = [N_PAIRS * TOK_W, N_PAIRS * SEQ_W, N_PAIRS * LOC_W, CONV_W]
    return pl.pallas_call(
        functools.partial(_in_kernel, rope=True, states=False),
        grid=(t // tm,),
        in_specs=in_specs,
        out_specs=[pl.BlockSpec((tm, w), lambda i: (i, 0)) for w in widths],
        out_shape=[jax.ShapeDtypeStruct((t, w), BF16) for w in widths],
        compiler_params=_params(1),
        name=name,
    )(*args)


def _attn_kernel(*refs, cached, lam_init, final):
    it = iter(refs)
    tok_ref, seq_ref = next(it), next(it)
    loc_refs = [next(it) for _ in range(LOCAL_CHUNKS if cached else 1)]
    if cached:
        x_ref, bias_ref = next(it), next(it)
    lam_ref, g_ref = next(it), next(it)
    res_ref, mod_ref, bb_ref, gc_ref = (next(it) for _ in range(4))
    if cached:
        gp_ref, gn_ref = next(it), next(it)
    sbz_ref, cw_ref, w_ref, fg_ref = (next(it) for _ in range(4))
    o_ref = next(it)

    lv = lam_ref[...]
    lam = (jnp.exp(jnp.sum(lv[0:1] * lv[1:2], keepdims=True))
           - jnp.exp(jnp.sum(lv[2:3] * lv[3:4], keepdims=True)) + lam_init)

    tq = res_ref.shape[0]
    gc = gc_ref[...].astype(F32)
    rows = lax.broadcasted_iota(jnp.int32, (tq, 1), 0)
    if cached:
        i, nq = pl.program_id(1), pl.num_programs(1)
        halo = gp_ref.shape[0]
        edge_prev = gp_ref[halo - 1:halo, :].astype(F32) * jnp.where(i != 0, 1.0, 0.0)
        edge_next = gn_ref[0:1, :].astype(F32) * jnp.where(i != nq - 1, 1.0, 0.0)
    else:
        edge_prev = edge_next = 0.0
    g_prev = jnp.where(rows == 0, edge_prev, pltpu.roll(gc, 1, 0))
    g_next = jnp.where(rows == tq - 1, edge_next, pltpu.roll(gc, tq - 1, 0))
    cw = cw_ref[...]
    conv = g_prev * cw[0:1] + gc * cw[1:2] + g_next * cw[2:3]
    yb = (bb_ref[...].astype(F32) * conv * sbz_ref[...].astype(F32)).astype(BF16)

    first = _lane_mask(0, HEAD_DIM)
    proj = []

    def gate(hp, off):
        return tok_ref[:, TOK_W * hp + off:TOK_W * hp + off + LANES].astype(F32)

    def project(group, halves):
        y = jnp.concatenate(halves, axis=1).astype(BF16)
        proj.append(_dot(y, w_ref[GROUP_W * group:GROUP_W * (group + 1), :]))

    def after_a(outs):
        proj.append(_dot(yb, w_ref[GROUP_W:2 * GROUP_W, :]))
        halves = []
        for hp in range(N_PAIRS):
            m00, m01, m10, m11 = outs[4 * hp:4 * hp + 4]
            o = jnp.where(first, m00 - lam * m01, m10 - lam * m11)
            o2 = o * o
            ss0 = jnp.sum(jnp.where(first, o2, 0.0), axis=-1, keepdims=True)
            ss1 = jnp.sum(jnp.where(first, 0.0, o2), axis=-1, keepdims=True)
            ms = jnp.where(first, ss0, ss1) * (1.0 / HEAD_DIM)
            y = o * lax.rsqrt(ms + EPS) * g_ref[...] * (1.0 - lam_init)
            halves.append(y * gate(hp, T_SZA))
        project(0, halves)

    def after_group(group, base, off):
        def run(outs):
            project(group, [jnp.where(first, outs[base + 2 * hp], outs[base + 2 * hp + 1])
                            * gate(hp, off) for hp in range(N_PAIRS)])
        return run

    jobs_a, jobs_c, jobs_d = [], [], []
    for hp in range(N_PAIRS):
        def col(ref, width, off, w=LANES, hp=hp):
            return ref[:, width * hp + off:width * hp + off + w]

        xcol = lambda off, hp=hp: x_ref[hp, :, off:off + LANES]
        qa = col(tok_ref, TOK_W, T_QA)
        keys = [col(seq_ref, SEQ_W, S_KA)] + ([xcol(X_KA)] if cached else [])
        vals = [col(seq_ref, SEQ_W, S_VA)] + ([xcol(X_VA)] if cached else [])
        nob = [None] * len(keys)
        for e in range(2):
            for c in range(2):
                qm = jnp.where(_lane_mask(HEAD_DIM * e + DA_SUB * c, DA_SUB), qa, jnp.zeros_like(qa))
                jobs_a.append((qm, keys, nob, vals, HEAD_DIM * (1 - e)))
        vals = [col(seq_ref, SEQ_W, S_VC)] + ([xcol(X_VC)] if cached else [])
        for e in range(2):
            keys = [col(seq_ref, SEQ_W, S_KC + LANES * e)] + (
                [xcol(X_KC + LANES * e)] if cached else [])
            jobs_c.append((col(tok_ref, TOK_W, T_QC + LANES * e), keys, nob, vals,
                           HEAD_DIM * (1 - e)))
        qd = col(tok_ref, TOK_W, T_QD)
        keys = [col(r, LOC_W, L_KD) for r in loc_refs] + ([xcol(X_KD)] if cached else [])
        vals = [col(r, LOC_W, L_VD) for r in loc_refs] + ([xcol(X_VD)] if cached else [])
        for e in range(2):
            qm = jnp.where(_lane_mask(HEAD_DIM * e, HEAD_DIM), qd, jnp.zeros_like(qd))
            biases = [None] * len(keys)
            if cached:
                biases = [bias_ref[2 * hp + e, :, TQ * j:TQ * (j + 1)]
                          for j in range(len(loc_refs))] + [None]
            jobs_d.append((qm, keys, biases, vals, HEAD_DIM * (1 - e)))

    n_a, n = len(jobs_a), 2 * N_PAIRS
    after = {n_a - 1: after_a, n_a + n - 1: after_group(2, n_a, T_SCZ),
             n_a + 2 * n - 1: after_group(3, n_a + n, T_SDZ)}
    _attend_all(jobs_a + jobs_c + jobs_d, after)
    xn = res_ref[...] + mod_ref[2:3, :] * functools.reduce(jnp.add, proj)
    if final:
        xn = _rms(xn, fg_ref[...])
    o_ref[...] = xn


def _ctx_layer_kernel(*refs, lam_init, final, states):
    it = iter(refs)
    in_refs = [next(it) for _ in range(9)]
    alias_refs = [next(it) for _ in range(6)] if states == "update" else []
    lam_ref, g_ref, cw_ref, w_out_ref, fg_ref = (next(it) for _ in range(5))
    o_ref = next(it)
    state_refs = [next(it) for _ in range(6)]
    tok_s, seq_s, loc_s, conv_s = (next(it) for _ in range(4))
    _in_kernel(*in_refs, *alias_refs, tok_s, seq_s, loc_s, conv_s, *state_refs,
               rope=False, states=states)
    view = lambda off: conv_s.at[:, off:off + GROUP_W]
    _attn_kernel(tok_s, seq_s, loc_s, lam_ref, g_ref, in_refs[0], in_refs[1], view(B_BB), view(B_G),
                 view(B_SBZ), cw_ref, w_out_ref, fg_ref, o_ref,
                 cached=False, lam_init=lam_init, final=final)


def _ctx_layer_call(x, mod, norm_g, w_in, q_norm_g, wuq, kv_norm_g, wuk, wuv, lam_vecs, subln_g2,
                    conv_w, w_out, final_g, states, *, layer, lam_init, seq, final, name):
    t = x.shape[0]
    in_specs = [
        pl.BlockSpec((seq, D_MODEL), lambda b: (b, 0)),
        pl.BlockSpec((None, None, 3, D_MODEL), lambda b: (layer, 0, 0, 0)),
        _layer_spec((1, D_MODEL), layer),
        _layer_spec((D_MODEL, D_IN_P), layer),
        _layer_spec((1, MLA_Q_RANK), layer),
        _layer_spec((MLA_Q_RANK, 4 * LANES), layer),
        _layer_spec((1, MLA_KV_RANK), layer),
        _layer_spec((MLA_KV_RANK, 4 * LANES), layer),
        _layer_spec((MLA_KV_RANK, GROUP_W), layer),
    ]
    args = [x, mod, norm_g, w_in, q_norm_g, wuq, kv_norm_g, wuk, wuv]
    mode = "update" if states else "create"
    assert states or layer == 0
    out_specs = [pl.BlockSpec((seq, D_MODEL), lambda b: (b, 0))]
    out_shape = [jax.ShapeDtypeStruct((t, D_MODEL), F32)]
    aliases = {}
    at = layer if states else None
    for k, (shape, block, index) in enumerate(_state_layout(t // seq, seq, seq)):
        out_specs.append(pl.BlockSpec(block(at), index(at)))
        out_shape.append(jax.ShapeDtypeStruct(shape, F32))
        if states:
            aliases[len(args)] = 1 + k
            in_specs.append(pl.BlockSpec(memory_space=pl.ANY))
            args.append(states[k])
    in_specs += [_layer_spec((4, DA_SUB), layer), _layer_spec((1, LANES), layer),
                 _layer_spec((3, GROUP_W), layer), _layer_spec((D_MODEL, D_MODEL), layer),
                 pl.BlockSpec((1, D_MODEL), lambda b: (0, 0))]
    args += [lam_vecs, subln_g2, conv_w, w_out, final_g]
    slabs = [N_PAIRS * TOK_W, N_PAIRS * SEQ_W, N_PAIRS * LOC_W, CONV_W]
    return pl.pallas_call(
        functools.partial(_ctx_layer_kernel, lam_init=lam_init, final=final, states=mode),
        grid=(t // seq,),
        in_specs=in_specs,
        out_specs=out_specs,
        out_shape=out_shape,
        scratch_shapes=[pltpu.VMEM((seq, w), BF16) for w in slabs],
        input_output_aliases=aliases,
        compiler_params=_params(1),
        name=name,
    )(*args)


def _attn_call(x, mod, u_tok, u_seq, u_loc, u_conv, cache, bias, lam_vecs, subln_g2, conv_w, w_out,
               final_g, *, layer, lam_init, nb, seq, tq, mod_row0, final, name):
    nq = seq // tq
    past = cache.shape[3]
    base = lambda i: jnp.clip(i - 1, 0, nq - LOCAL_CHUNKS)

    def bias_map(b, i):
        pattern = jnp.where(i == 0, 0, jnp.where(i == nq - 1, 2, 1))
        return (layer, pattern, 0, 0, 0)

    in_specs = [pl.BlockSpec((tq, N_PAIRS * TOK_W), lambda b, i: (b * nq + i, 0)),
                pl.BlockSpec((seq, N_PAIRS * SEQ_W), lambda b, i: (b, 0))]
    in_specs += [pl.BlockSpec((tq, N_PAIRS * LOC_W), lambda b, i, j=j: (b * nq + base(i) + j, 0))
                 for j in range(LOCAL_CHUNKS)]
    in_specs += [pl.BlockSpec((None, None, N_PAIRS, past, CACHE_W), lambda b, i: (b, layer, 0, 0, 0)),
                 pl.BlockSpec((None, None, N_HEADS_GRP, tq, LOCAL_CHUNKS * tq), bias_map),
                 _layer_spec((4, DA_SUB), layer), _layer_spec((1, LANES), layer)]
    args = [u_tok, u_seq] + [u_loc] * LOCAL_CHUNKS + [cache, bias, lam_vecs, subln_g2]

    halo = BF16_SUBLANES
    hb = tq // halo
    last = nb * seq // halo - 1
    blk = lambda b, i: b * nq + i
    conv = lambda off: pl.BlockSpec((tq, GROUP_W), lambda b, i: (blk(b, i), off // GROUP_W))
    in_specs += [
        pl.BlockSpec((tq, D_MODEL), lambda b, i: (blk(b, i), 0)),
        pl.BlockSpec((None, None, 3, D_MODEL), lambda b, i: (layer, mod_row0 + b, 0, 0)),
        conv(B_BB), conv(B_G),
        pl.BlockSpec((halo, GROUP_W),
                     lambda b, i: (jnp.maximum(blk(b, i) * hb - 1, 0), B_G // GROUP_W)),
        pl.BlockSpec((halo, GROUP_W),
                     lambda b, i: (jnp.minimum((blk(b, i) + 1) * hb, last), B_G // GROUP_W)),
        conv(B_SBZ),
        _layer_spec((3, GROUP_W), layer),
        _layer_spec((D_MODEL, D_MODEL), layer),
        pl.BlockSpec((1, D_MODEL), lambda b, i: (0, 0)),
    ]
    args += [x, mod, u_conv, u_conv, u_conv, u_conv, u_conv, conv_w, w_out, final_g]
    return pl.pallas_call(
        functools.partial(_attn_kernel, cached=True, lam_init=lam_init, final=final),
        grid=(nb, nq),
        in_specs=in_specs,
        out_specs=pl.BlockSpec((tq, D_MODEL), lambda b, i: (blk(b, i), 0)),
        out_shape=jax.ShapeDtypeStruct((nb * seq, D_MODEL), F32),
        compiler_params=_params(2),
        name=name,
    )(*args)


def _local_bias_tables(rpb, rows):
    nq = rows // Q_ROWS
    n_dy, n_dx = 2 * NA_WIN_H - 1, 2 * NA_WIN_W - 1
    qc = np.arange(GRID_W)[:, None]
    kc = np.arange(GRID_W)[None, :]
    ws = np.clip(qc - NA_WIN_W // 2, 0, GRID_W - NA_WIN_W)
    col_ok = (kc >= ws) & (kc < ws + NA_WIN_W)
    oh_dx = ((kc - qc + NA_WIN_W - 1)[None] == np.arange(n_dx)[:, None, None]) & col_ok[None]
    tile_idx = []
    for i in (0, 1, nq - 1):
        row0 = Q_ROWS * int(np.clip(i - 1, 0, nq - LOCAL_CHUNKS))
        r = (Q_ROWS * i + np.arange(Q_ROWS))[:, None]
        kr = (row0 + np.arange(LOCAL_ROWS))[None, :]
        rs = np.clip(r - NA_WIN_H // 2, 0, rows - NA_WIN_H)
        ok = (kr >= rs) & (kr < rs + NA_WIN_H)
        tile_idx.append(np.where(ok, kr - r + NA_WIN_H - 1, n_dy))
    tile_idx = np.stack(tile_idx)
    oh_dx2, col_ok2 = np.concatenate([oh_dx, oh_dx], axis=-1), np.concatenate([col_ok, col_ok], axis=-1)
    hi = lax.Precision.HIGHEST
    cols = jnp.einsum("lhyd,dqc->lhyqc", rpb.astype(F32), jnp.asarray(oh_dx2, F32), precision=hi)
    cols = jnp.where(col_ok2, cols * LOG2E, NEG_INF)

    def build(cols_ref, o_ref):
        left = _lane_mask(0, GRID_W)
        masked = jnp.full((GRID_W, LANES), NEG_INF, F32)
        tile = lambda y: masked if y == n_dy else cols_ref[y]
        for p in range(3):
            for j in range(Q_ROWS):
                for m in range(LOCAL_ROWS // 2):
                    a, b = int(tile_idx[p, j, 2 * m]), int(tile_idx[p, j, 2 * m + 1])
                    o_ref[p, GRID_W * j:GRID_W * (j + 1), LANES * m:LANES * (m + 1)] = jnp.where(
                        left, tile(a), tile(b))

    return pl.pallas_call(
        build,
        grid=(DEPTH, N_HEADS_GRP),
        in_specs=[pl.BlockSpec((None, None, n_dy, GRID_W, LANES), lambda l, h: (l, h, 0, 0, 0))],
        out_specs=pl.BlockSpec((None, 3, None, TQ, LOCAL_ROWS * GRID_W), lambda l, h: (l, 0, h, 0, 0)),
        out_shape=jax.ShapeDtypeStruct((DEPTH, 3, N_HEADS_GRP, TQ, LOCAL_ROWS * GRID_W), F32),
        compiler_params=_params(2),
        name="local_bias",
    )(cols)


def _w_in_prep_kernel(w_ref, o_ref):
    rows = w_ref.shape[1]
    kpe_end = W_KPE + MLA_ROPE
    o_ref[:, :W_KPE] = w_ref[:W_KPE, :].T.astype(BF16)
    kpe = jnp.concatenate([jnp.zeros((KPE_LANE, rows), F32), w_ref[W_KPE:kpe_end, :],
                           jnp.zeros((LANES - KPE_LANE - MLA_ROPE, rows), F32)], axis=0)
    o_ref[:, W_KPE:W_CZ] = kpe.T.astype(BF16)
    o_ref[:, W_CZ:] = w_ref[kpe_end:, :].T.astype(BF16)


def _w_in_prep(w_in):
    d_in = w_in.shape[-1]
    rows = W_PREP_ROWS
    return pl.pallas_call(
        _w_in_prep_kernel,
        grid=(DEPTH, D_MODEL // rows),
        in_specs=[pl.BlockSpec((None, d_in, rows), lambda l, r: (l, 0, r))],
        out_specs=pl.BlockSpec((None, rows, D_IN_P), lambda l, r: (l, r, 0)),
        out_shape=jax.ShapeDtypeStruct((DEPTH, D_MODEL, D_IN_P), BF16),
        compiler_params=_params(2),
        name="w_in_prep",
    )(jnp.swapaxes(w_in, 1, 2))


def _rope_tables(seq):
    f32 = np.float32
    t = np.arange(seq)
    rows = (t // GRID_W).astype(f32)
    cols = (t % GRID_W).astype(f32)
    half = DA_SUB // 2
    inv = (f32(1.0) / (f32(ROPE_BASE) ** (np.arange(0, half, 2, dtype=f32) / f32(half)))).astype(f32)
    ar = rows[:, None] * inv
    ac = cols[:, None] * inv
    ang = np.concatenate([ar, ar, ac, ac], axis=-1)
    cos, sin = np.cos(ang).astype(f32), np.sin(ang).astype(f32)
    first = (np.arange(DA_SUB) % 16 < 8)[None, :]
    sin_neg = np.where(first, -sin, f32(0.0))
    sin_pos = np.where(first, f32(0.0), sin)
    tile = lambda a: jnp.asarray(np.tile(a, (1, LANES // DA_SUB)), F32)

    def pad(a, fill):
        return jnp.asarray(np.concatenate(
            [np.full((seq, KPE_LANE), fill, f32), a,
             np.full((seq, LANES - KPE_LANE - MLA_ROPE), fill, f32)], axis=1), F32)

    return ([tile(cos), tile(sin_neg), tile(sin_pos)],
            [pad(cos, 1.0), pad(sin_neg, 0.0), pad(sin_pos, 0.0)])


def _pad_heads(w, width, take):
    d, k, _ = w.shape
    w = w.reshape(d, k, N_HEADS_GRP, width)[..., :take]
    return jnp.pad(w, ((0, 0), (0, 0), (0, 0), (0, LANES - take))).reshape(d, k, N_HEADS_GRP * LANES)


def kernel(x_prompt, x_sample, cache_a_k, cache_a_v, cache_c_kv, cache_c_kpe, cache_d_k, cache_d_v,
           c, c_ctx, ada_w, ada_b, norm_g, w_in, da_lambda, da_subln_g, conv_w,
           mla_q_norm_g, mla_w_uq, mla_kv_norm_g, mla_w_ukv, na_rpb, w_out, final_norm_g):
    batch, seq, _ = x_prompt.shape
    dec_batch, dec_seq, _ = x_sample.shape
    past = cache_a_k.shape[2]
    assert dec_seq % TQ == 0 and dec_seq // TQ >= LOCAL_CHUNKS and seq % LANES == 0
    assert dec_seq % IN_PROJ_ROWS == 0 and D_MODEL % W_PREP_ROWS == 0

    w_in_p = _w_in_prep(w_in)
    w_out_b = w_out.astype(BF16)
    wuq = _pad_heads(mla_w_uq, MLA_NOPE + MLA_ROPE, MLA_NOPE + MLA_ROPE).astype(BF16)
    wuk = _pad_heads(mla_w_ukv, MLA_NOPE + MLA_V, MLA_NOPE).astype(BF16)
    wuv = mla_w_ukv.reshape(DEPTH, MLA_KV_RANK, N_HEADS_GRP, MLA_NOPE + MLA_V)[..., MLA_NOPE:]
    wuv = wuv.reshape(DEPTH, MLA_KV_RANK, GROUP_W).astype(BF16)
    subln_g2 = jnp.tile(da_subln_g, (1, LANES // HEAD_DIM)).reshape(DEPTH, 1, LANES)
    ng = norm_g.reshape(DEPTH, 1, D_MODEL)
    qng = mla_q_norm_g.reshape(DEPTH, 1, MLA_Q_RANK)
    kvng = mla_kv_norm_g.reshape(DEPTH, 1, MLA_KV_RANK)
    fg = final_norm_g.reshape(1, D_MODEL)
    tables_a, tables_c = _rope_tables(dec_seq)
    tables = tables_a + tables_c

    cvec8 = jnp.concatenate([c_ctx[None], c, jnp.zeros((8 - 1 - dec_batch, D_MODEL), F32)], axis=0)
    mod = _ada_call(cvec8, ada_w, ada_b).reshape(DEPTH, 8, 3, D_MODEL)

    feat = lambda a: a.transpose(0, 1, 3, 4, 2).reshape(dec_batch, DEPTH, GROUP_W, past)
    cache = _cache_call(feat(cache_a_k), feat(cache_a_v), cache_c_kv,
                        cache_c_kpe.transpose(0, 1, 3, 2), feat(cache_d_k), feat(cache_d_v), wuk, wuv)
    bias = _local_bias_tables(na_rpb, dec_seq // GRID_W)

    xp = x_prompt.reshape(batch * seq, D_MODEL)
    xs = x_sample.reshape(dec_batch * dec_seq, D_MODEL)
    states = []
    for l in range(DEPTH):
        lam_init = 0.8 - 0.6 * math.exp(-0.3 * l)
        final = l == DEPTH - 1

        xp, *states = _ctx_layer_call(xp, mod, ng, w_in_p, qng, wuq, kvng, wuk, wuv, da_lambda,
                                      subln_g2, conv_w, w_out_b, fg, states, layer=l,
                                      lam_init=lam_init, seq=seq, final=final, name=f"ctx_layer_{l}")

        tok, sq, loc, cv = _in_call(xs, mod, ng, w_in_p, qng, wuq, kvng, wuk, wuv, tables,
                                    layer=l, seq=dec_seq, tm=IN_PROJ_ROWS, mod_row0=1,
                                    name=f"lat_in_{l}")
        xs = _attn_call(xs, mod, tok, sq, loc, cv, cache, bias, da_lambda, subln_g2, conv_w, w_out_b,
                        fg, layer=l, lam_init=lam_init, nb=dec_batch, seq=dec_seq, tq=TQ, mod_row0=1,
                        final=final, name=f"lat_attn_{l}")

    def heads(a):
        return a.reshape(batch, DEPTH, N_HEADS_GRP, HEAD_DIM, seq).transpose(0, 1, 4, 2, 3)

    s_ak, s_av, s_ckv, s_kpe, s_dk, s_dv = states
    return (xp.reshape(batch, seq, D_MODEL), xs.reshape(dec_batch, dec_seq, D_MODEL),
            heads(s_ak), heads(s_av), s_ckv, s_kpe.transpose(0, 1, 3, 2), heads(s_dk), heads(s_dv))
```

```python
import functools
import math

import jax
import jax.numpy as jnp
import numpy as np
from jax import lax
from jax.experimental import pallas as pl
from jax.experimental.pallas import tpu as pltpu

F32 = jnp.float32
BF16 = jnp.bfloat16

D_MODEL = 1024
DEPTH = 4
GRID_W = 64
HEAD_DIM = 64
GROUP_W = 256
N_HEADS_GRP = 4
DA_SUB = 32
MLA_Q_RANK = 256
MLA_KV_RANK = 128
MLA_NOPE = 64
MLA_ROPE = 32
MLA_V = 64
LOG2E = math.log2(math.e)
MLA_SCALE = (MLA_NOPE + MLA_ROPE) ** -0.5
DA_SCALE = DA_SUB ** -0.5
NA_SCALE = HEAD_DIM ** -0.5
NA_WIN_H = 8
NA_WIN_W = 16
ROPE_BASE = 10000.0
EPS = 1e-6
NEG_INF = -1e30

LANES = 128
BF16_SUBLANES = 16
N_PAIRS = GROUP_W // LANES
VMEM_LIMIT = 56 * 1024 * 1024
IN_PROJ_ROWS = 1024
W_PREP_ROWS = 256

W_AQ, W_AK, W_AV, W_AZ = 0, 256, 512, 768
W_BB, W_BC, W_BH, W_BZ = 1024, 1280, 1536, 1792
W_CQ, W_CKV, W_KPE, W_CZ = 2048, 2304, 2432, 2560
W_DQ, W_DK, W_DV, W_DZ = 2816, 3072, 3328, 3584
D_IN_P = 3840
KPE_LANE = 64

T_QA, T_SZA, T_QC, T_SCZ, T_QD, T_SDZ, TOK_W = 0, 128, 256, 512, 640, 768, 896
S_KA, S_VA, S_KC, S_VC, SEQ_W = 0, 128, 256, 512, 640
L_KD, L_VD, LOC_W = 0, 128, 256
X_KA, X_VA, X_KC, X_VC, X_KD, X_VD, CACHE_W = 0, 128, 256, 512, 640, 768, 896
B_BB, B_G, B_SBZ, CONV_W = 0, 256, 512, 768

Q_ROWS = 4
TQ = Q_ROWS * GRID_W
LOCAL_CHUNKS = 3
LOCAL_ROWS = LOCAL_CHUNKS * Q_ROWS


def _dot(a, b):
    return jnp.dot(a, b, preferred_element_type=F32)


def _dot_nt(a, b):
    return lax.dot_general(a, b, (((1,), (1,)), ((), ())), preferred_element_type=F32)


def _silu(z):
    return z * (1.0 / (1.0 + jnp.exp(-z)))


def _rms(x, g):
    return x * lax.rsqrt(jnp.mean(x * x, axis=-1, keepdims=True) + EPS) * g


def _rope128(x, cos, sin_neg, sin_pos):
    return x * cos + pltpu.roll(x, LANES - 8, 1) * sin_neg + pltpu.roll(x, 8, 1) * sin_pos


def _scores(job):
    q, keys, biases, _, _ = job
    blocks = []
    for k, b in zip(keys, biases):
        s = _dot_nt(q, k)
        blocks.append(s if b is None else s + b)
    return blocks


def _weighted_values(job, blocks):
    _, _, _, values, sum_lane = job
    m = functools.reduce(jnp.maximum, [jnp.max(s, axis=-1, keepdims=True) for s in blocks])
    one_hot = _lane_mask(sum_lane, 1)
    o = functools.reduce(jnp.add, [
        _dot(jnp.exp2(s - m).astype(BF16), jnp.where(one_hot, jnp.ones_like(v), v))
        for s, v in zip(blocks, values)])
    l = jnp.sum(jnp.where(one_hot, o, 0.0), axis=-1, keepdims=True)
    return o / l


SCORES_IN_FLIGHT_BYTES = 6 * 1024 * 1024


def _attend_all(jobs, after):
    q, keys = jobs[0][0], jobs[0][1]
    score_bytes = 4 * q.shape[0] * sum(k.shape[0] for k in keys)
    ahead = max(1, min(len(jobs), SCORES_IN_FLIGHT_BYTES // score_bytes))
    outs = []
    pending = [_scores(job) for job in jobs[:ahead]]
    for k, job in enumerate(jobs):
        if k + ahead < len(jobs):
            pending.append(_scores(jobs[k + ahead]))
        outs.append(_weighted_values(job, pending.pop(0)))
        if k in after:
            after[k](outs)
    return outs


def _lane_mask(lo, width):
    lane = lax.broadcasted_iota(jnp.int32, (1, LANES), 1)
    return jnp.logical_and(lane >= lo, lane < lo + width)


def _params(n_axes):
    return pltpu.CompilerParams(dimension_semantics=("arbitrary",) * n_axes,
                                vmem_limit_bytes=VMEM_LIMIT)


def _layer_spec(shape, layer):
    zeros = (0,) * len(shape)
    return pl.BlockSpec((None,) + tuple(shape), lambda *_: (layer,) + zeros,
                        pipeline_mode=pl.Buffered(1))


def _ada_kernel(c_ref, w_ref, b_ref, o_ref):
    s = _silu(c_ref[...]).astype(BF16)
    o_ref[...] = _dot(s, w_ref[...].astype(BF16)) + b_ref[...]


def _ada_call(cvec8, ada_w, ada_b):
    nj = 3
    return pl.pallas_call(
        _ada_kernel,
        grid=(DEPTH, nj),
        in_specs=[pl.BlockSpec((8, D_MODEL), lambda l, j: (0, 0)),
                  pl.BlockSpec((None, D_MODEL, D_MODEL), lambda l, j: (l, 0, j)),
                  pl.BlockSpec((None, None, 1, D_MODEL), lambda l, j: (l, j, 0, 0))],
        out_specs=pl.BlockSpec((None, 8, D_MODEL), lambda l, j: (l, 0, j)),
        out_shape=jax.ShapeDtypeStruct((DEPTH, 8, 3 * D_MODEL), F32),
        compiler_params=_params(2),
        name="adaln",
    )(cvec8, ada_w, ada_b.reshape(DEPTH, nj, 1, D_MODEL))


def _cache_kernel(ak_ref, av_ref, ckv_ref, kpe_ref, dk_ref, dv_ref, wuk_ref, wuv_ref, o_ref):
    ckv = ckv_ref[...].astype(BF16)
    kn = _dot(ckv, wuk_ref[...])
    vc = _dot(ckv, wuv_ref[...])
    past = ckv.shape[0]
    kpe = jnp.concatenate([jnp.zeros((KPE_LANE, past), F32), kpe_ref[...],
                           jnp.zeros((LANES - KPE_LANE - MLA_ROPE, past), F32)], axis=0).T
    for hp in range(N_PAIRS):
        half = slice(LANES * hp, LANES * (hp + 1))
        o_ref[hp, :, X_KA:X_KA + LANES] = ak_ref[half, :].T.astype(BF16)
        o_ref[hp, :, X_VA:X_VA + LANES] = av_ref[half, :].T.astype(BF16)
        for e in range(2):
            sl = slice(LANES * (2 * hp + e), LANES * (2 * hp + e + 1))
            o_ref[hp, :, X_KC + LANES * e:X_KC + LANES * (e + 1)] = (kn[:, sl] + kpe).astype(BF16)
        o_ref[hp, :, X_VC:X_VC + LANES] = vc[:, half].astype(BF16)
        o_ref[hp, :, X_KD:X_KD + LANES] = dk_ref[half, :].T.astype(BF16)
        o_ref[hp, :, X_VD:X_VD + LANES] = dv_ref[half, :].T.astype(BF16)


def _cache_call(ca_k, ca_v, c_kv, c_kpe, cd_k, cd_v, wuk, wuv):
    nb, _, past, _ = c_kv.shape
    cache = lambda w: pl.BlockSpec((None, None, past, w), lambda l, b: (b, l, 0, 0))
    feat = lambda rows: pl.BlockSpec((None, None, rows, past), lambda l, b: (b, l, 0, 0))
    return pl.pallas_call(
        _cache_kernel,
        grid=(DEPTH, nb),
        in_specs=[feat(GROUP_W), feat(GROUP_W), cache(MLA_KV_RANK), feat(MLA_ROPE),
                  feat(GROUP_W), feat(GROUP_W),
                  pl.BlockSpec((None, MLA_KV_RANK, 4 * LANES), lambda l, b: (l, 0, 0)),
                  pl.BlockSpec((None, MLA_KV_RANK, GROUP_W), lambda l, b: (l, 0, 0))],
        out_specs=pl.BlockSpec((None, None, N_PAIRS, past, CACHE_W), lambda l, b: (b, l, 0, 0, 0)),
        out_shape=jax.ShapeDtypeStruct((nb, DEPTH, N_PAIRS, past, CACHE_W), BF16),
        compiler_params=_params(2),
        name="cache_prep",
    )(ca_k, ca_v, c_kv, c_kpe, cd_k, cd_v, wuk, wuv)


def _in_kernel(*refs, rope, states):
    it = iter(refs)
    x_ref, mod_ref, ng_ref, w_ref, qng_ref, wuq_ref, kvng_ref, wuk_ref, wuv_ref = (
        next(it) for _ in range(9))
    if rope:
        ta = [next(it)[...] for _ in range(3)]
        tc = [next(it)[...] for _ in range(3)]
    if states == "update":
        for _ in range(6):
            next(it)
    tok_ref, seq_ref, loc_ref, conv_ref = (next(it) for _ in range(4))
    if states:
        sak_ref, sav_ref, sckv_ref, skpe_ref, sdk_ref, sdv_ref = (next(it) for _ in range(6))

    x = x_ref[...]
    h = _rms(x, ng_ref[...]) * (1.0 + mod_ref[1:2, :]) + mod_ref[0:1, :]
    hb = h.astype(BF16)

    def seg(off, n=GROUP_W):
        return _dot(hb, w_ref[:, off:off + n])

    def put_state(ref, val):
        if states == "create":
            ref[0] = val
            for later in range(1, DEPTH):
                ref[later] = jnp.zeros_like(val)
        else:
            ref[...] = val

    def put(ref, width, off, val):
        for hp in range(N_PAIRS):
            ref[:, width * hp + off:width * hp + off + LANES] = (
                val[:, LANES * hp:LANES * (hp + 1)].astype(BF16))

    cqn = _rms(seg(W_CQ), qng_ref[...]).astype(BF16)
    ckvn = _rms(seg(W_CKV, MLA_KV_RANK), kvng_ref[...])
    kpe = seg(W_KPE, LANES)
    if states:
        put_state(sckv_ref, ckvn)
        put_state(skpe_ref, kpe.T[KPE_LANE:KPE_LANE + MLA_ROPE, :])
    ckvb = ckvn.astype(BF16)

    aq, ak, av = seg(W_AQ), seg(W_AK), seg(W_AV)
    if states:
        put_state(sak_ref, ak.T)
        put_state(sav_ref, av.T)
    if rope:
        aq = jnp.concatenate([_rope128(aq[:, :LANES], *ta), _rope128(aq[:, LANES:], *ta)], axis=1)
        ak = jnp.concatenate([_rope128(ak[:, :LANES], *ta), _rope128(ak[:, LANES:], *ta)], axis=1)
    put(tok_ref, TOK_W, T_QA, aq * (DA_SCALE * LOG2E))
    put(seq_ref, SEQ_W, S_KA, ak)
    put(seq_ref, SEQ_W, S_VA, av)
    put(tok_ref, TOK_W, T_SZA, _silu(seg(W_AZ)))

    conv_ref[:, B_BB:B_BB + GROUP_W] = seg(W_BB).astype(BF16)
    conv_ref[:, B_G:B_G + GROUP_W] = (seg(W_BC) * seg(W_BH)).astype(BF16)
    conv_ref[:, B_SBZ:B_SBZ + GROUP_W] = _silu(seg(W_BZ)).astype(BF16)

    dk, dv = seg(W_DK), seg(W_DV)
    if states:
        put_state(sdk_ref, dk.T)
        put_state(sdv_ref, dv.T)
    put(tok_ref, TOK_W, T_QD, seg(W_DQ) * (NA_SCALE * LOG2E))
    put(loc_ref, LOC_W, L_KD, dk)
    put(loc_ref, LOC_W, L_VD, dv)
    put(tok_ref, TOK_W, T_SDZ, _silu(seg(W_DZ)))
    put(tok_ref, TOK_W, T_SCZ, _silu(seg(W_CZ)))

    q = _dot(cqn, wuq_ref[...])
    kn = _dot(ckvb, wuk_ref[...])
    vc = _dot(ckvb, wuv_ref[...])
    kpe_r = _rope128(kpe, *tc) if rope else kpe
    for hd in range(N_HEADS_GRP):
        sl = slice(LANES * hd, LANES * (hd + 1))
        hp, e = divmod(hd, 2)
        qh = q[:, sl]
        if rope:
            qh = _rope128(qh, *tc)
        o = LANES * e
        tok_ref[:, TOK_W * hp + T_QC + o:TOK_W * hp + T_QC + o + LANES] = (
            qh * (MLA_SCALE * LOG2E)).astype(BF16)
        seq_ref[:, SEQ_W * hp + S_KC + o:SEQ_W * hp + S_KC + o + LANES] = (
            kn[:, sl] + kpe_r).astype(BF16)
    put(seq_ref, SEQ_W, S_VC, vc)


def _state_layout(batch, seq, tm):
    nblk = seq // tm
    lblk = lambda layer: DEPTH if layer is None else None
    lidx = lambda layer: 0 if layer is None else layer
    feat = lambda rows: ((batch, DEPTH, rows, seq),
                         lambda layer: (None, lblk(layer), rows, tm),
                         lambda layer: lambda i: (i // nblk, lidx(layer), 0, i % nblk))
    tokm = ((batch, DEPTH, seq, MLA_KV_RANK),
            lambda layer: (None, lblk(layer), tm, MLA_KV_RANK),
            lambda layer: lambda i: (i // nblk, lidx(layer), i % nblk, 0))
    return [feat(GROUP_W), feat(GROUP_W), tokm, feat(MLA_ROPE), feat(GROUP_W), feat(GROUP_W)]


def _in_call(x, mod, norm_g, w_in, q_norm_g, wuq, kv_norm_g, wuk, wuv, tables, *, layer, seq, tm,
             mod_row0, name):
    t = x.shape[0]
    nblk = seq // tm
    in_specs = [
        pl.BlockSpec((tm, D_MODEL), lambda i: (i, 0)),
        pl.BlockSpec((None, None, 3, D_MODEL), lambda i: (layer, mod_row0 + i // nblk, 0, 0)),
        _layer_spec((1, D_MODEL), layer),
        _layer_spec((D_MODEL, D_IN_P), layer),
        _layer_spec((1, MLA_Q_RANK), layer),
        _layer_spec((MLA_Q_RANK, 4 * LANES), layer),
        _layer_spec((1, MLA_KV_RANK), layer),
        _layer_spec((MLA_KV_RANK, 4 * LANES), layer),
        _layer_spec((MLA_KV_RANK, GROUP_W), layer),
    ]
    in_specs += [pl.BlockSpec((tm, LANES), lambda i: (i % nblk, 0))] * len(tables)
    args = [x, mod, norm_g, w_in, q_norm_g, wuq, kv_norm_g, wuk, wuv] + list(tables)
    widths = [N_PAIRS * TOK_W, N_PAIRS * SEQ_W, N_PAIRS * LOC_W, CONV_W]
    return pl.pallas_call(
        functools.partial(_in_kernel, rope=True, states=False),
        grid=(t // tm,),
        in_specs=in_specs,
        out_specs=[pl.BlockSpec((tm, w), lambda i: (i, 0)) for w in widths],
        out_shape=[jax.ShapeDtypeStruct((t, w), BF16) for w in widths],
        compiler_params=_params(1),
        name=name,
    )(*args)


def _attn_kernel(*refs, cached, lam_init, final):
    it = iter(refs)
    tok_ref, seq_ref = next(it), next(it)
    loc_refs = [next(it) for _ in range(LOCAL_CHUNKS if cached else 1)]
    if cached:
        x_ref, bias_ref = next(it), next(it)
    lam_ref, g_ref = next(it), next(it)
    res_ref, mod_ref, bb_ref, gc_ref = (next(it) for _ in range(4))
    if cached:
        gp_ref, gn_ref = next(it), next(it)
    sbz_ref, cw_ref, w_ref, fg_ref = (next(it) for _ in range(4))
    o_ref = next(it)

    lv = lam_ref[...]
    lam = (jnp.exp(jnp.sum(lv[0:1] * lv[1:2], keepdims=True))
           - jnp.exp(jnp.sum(lv[2:3] * lv[3:4], keepdims=True)) + lam_init)

    tq = res_ref.shape[0]
    gc = gc_ref[...].astype(F32)
    rows = lax.broadcasted_iota(jnp.int32, (tq, 1), 0)
    if cached:
        i, nq = pl.program_id(1), pl.num_programs(1)
        halo = gp_ref.shape[0]
        edge_prev = gp_ref[halo - 1:halo, :].astype(F32) * jnp.where(i != 0, 1.0, 0.0)
        edge_next = gn_ref[0:1, :].astype(F32) * jnp.where(i != nq - 1, 1.0, 0.0)
    else:
        edge_prev = edge_next = 0.0
    g_prev = jnp.where(rows == 0, edge_prev, pltpu.roll(gc, 1, 0))
    g_next = jnp.where(rows == tq - 1, edge_next, pltpu.roll(gc, tq - 1, 0))
    cw = cw_ref[...]
    conv = g_prev * cw[0:1] + gc * cw[1:2] + g_next * cw[2:3]
    yb = (bb_ref[...].astype(F32) * conv * sbz_ref[...].astype(F32)).astype(BF16)

    first = _lane_mask(0, HEAD_DIM)
    proj = []

    def gate(hp, off):
        return tok_ref[:, TOK_W * hp + off:TOK_W * hp + off + LANES].astype(F32)

    def project(group, halves):
        y = jnp.concatenate(halves, axis=1).astype(BF16)
        proj.append(_dot(y, w_ref[GROUP_W * group:GROUP_W * (group + 1), :]))

    def after_a(outs):
        proj.append(_dot(yb, w_ref[GROUP_W:2 * GROUP_W, :]))
        halves = []
        for hp in range(N_PAIRS):
            m00, m01, m10, m11 = outs[4 * hp:4 * hp + 4]
            o = jnp.where(first, m00 - lam * m01, m10 - lam * m11)
            o2 = o * o
            ss0 = jnp.sum(jnp.where(first, o2, 0.0), axis=-1, keepdims=True)
            ss1 = jnp.sum(jnp.where(first, 0.0, o2), axis=-1, keepdims=True)
            ms = jnp.where(first, ss0, ss1) * (1.0 / HEAD_DIM)
            y = o * lax.rsqrt(ms + EPS) * g_ref[...] * (1.0 - lam_init)
            halves.append(y * gate(hp, T_SZA))
        project(0, halves)

    def after_group(group, base, off):
        def run(outs):
            project(group, [jnp.where(first, outs[base + 2 * hp], outs[base + 2 * hp + 1])
                            * gate(hp, off) for hp in range(N_PAIRS)])
        return run

    jobs_a, jobs_c, jobs_d = [], [], []
    for hp in range(N_PAIRS):
        def col(ref, width, off, w=LANES, hp=hp):
            return ref[:, width * hp + off:width * hp + off + w]

        xcol = lambda off, hp=hp: x_ref[hp, :, off:off + LANES]
        qa = col(tok_ref, TOK_W, T_QA)
        keys = [col(seq_ref, SEQ_W, S_KA)] + ([xcol(X_KA)] if cached else [])
        vals = [col(seq_ref, SEQ_W, S_VA)] + ([xcol(X_VA)] if cached else [])
        nob = [None] * len(keys)
        for e in range(2):
            for c in range(2):
                qm = jnp.where(_lane_mask(HEAD_DIM * e + DA_SUB * c, DA_SUB), qa, jnp.zeros_like(qa))
                jobs_a.append((qm, keys, nob, vals, HEAD_DIM * (1 - e)))
        vals = [col(seq_ref, SEQ_W, S_VC)] + ([xcol(X_VC)] if cached else [])
        for e in range(2):
            keys = [col(seq_ref, SEQ_W, S_KC + LANES * e)] + (
                [xcol(X_KC + LANES * e)] if cached else [])
            jobs_c.append((col(tok_ref, TOK_W, T_QC + LANES * e), keys, nob, vals,
                           HEAD_DIM * (1 - e)))
        qd = col(tok_ref, TOK_W, T_QD)
        keys = [col(r, LOC_W, L_KD) for r in loc_refs] + ([xcol(X_KD)] if cached else [])
        vals = [col(r, LOC_W, L_VD) for r in loc_refs] + ([xcol(X_VD)] if cached else [])
        for e in range(2):
            qm = jnp.where(_lane_mask(HEAD_DIM * e, HEAD_DIM), qd, jnp.zeros_like(qd))
            biases = [None] * len(keys)
            if cached:
                biases = [bias_ref[2 * hp + e, :, TQ * j:TQ * (j + 1)]
                          for j in range(len(loc_refs))] + [None]
            jobs_d.append((qm, keys, biases, vals, HEAD_DIM * (1 - e)))

    n_a, n = len(jobs_a), 2 * N_PAIRS
    after = {n_a - 1: after_a, n_a + n - 1: after_group(2, n_a, T_SCZ),
             n_a + 2 * n - 1: after_group(3, n_a + n, T_SDZ)}
    _attend_all(jobs_a + jobs_c + jobs_d, after)
    xn = res_ref[...] + mod_ref[2:3, :] * functools.reduce(jnp.add, proj)
    if final:
        xn = _rms(xn, fg_ref[...])
    o_ref[...] = xn


def _ctx_layer_kernel(*refs, lam_init, final, states):
    it = iter(refs)
    in_refs = [next(it) for _ in range(9)]
    alias_refs = [next(it) for _ in range(6)] if states == "update" else []
    lam_ref, g_ref, cw_ref, w_out_ref, fg_ref = (next(it) for _ in range(5))
    o_ref = next(it)
    state_refs = [next(it) for _ in range(6)]
    tok_s, seq_s, loc_s, conv_s = (next(it) for _ in range(4))
    _in_kernel(*in_refs, *alias_refs, tok_s, seq_s, loc_s, conv_s, *state_refs,
               rope=False, states=states)
    view = lambda off: conv_s.at[:, off:off + GROUP_W]
    _attn_kernel(tok_s, seq_s, loc_s, lam_ref, g_ref, in_refs[0], in_refs[1], view(B_BB), view(B_G),
                 view(B_SBZ), cw_ref, w_out_ref, fg_ref, o_ref,
                 cached=False, lam_init=lam_init, final=final)


def _ctx_layer_call(x, mod, norm_g, w_in, q_norm_g, wuq, kv_norm_g, wuk, wuv, lam_vecs, subln_g2,
                    conv_w, w_out, final_g, states, *, layer, lam_init, seq, final, name):
    t = x.shape[0]
    in_specs = [
        pl.BlockSpec((seq, D_MODEL), lambda b: (b, 0)),
        pl.BlockSpec((None, None, 3, D_MODEL), lambda b: (layer, 0, 0, 0)),
        _layer_spec((1, D_MODEL), layer),
        _layer_spec((D_MODEL, D_IN_P), layer),
        _layer_spec((1, MLA_Q_RANK), layer),
        _layer_spec((MLA_Q_RANK, 4 * LANES), layer),
        _layer_spec((1, MLA_KV_RANK), layer),
        _layer_spec((MLA_KV_RANK, 4 * LANES), layer),
        _layer_spec((MLA_KV_RANK, GROUP_W), layer),
    ]
    args = [x, mod, norm_g, w_in, q_norm_g, wuq, kv_norm_g, wuk, wuv]
    mode = "update" if states else "create"
    assert states or layer == 0
    out_specs = [pl.BlockSpec((seq, D_MODEL), lambda b: (b, 0))]
    out_shape = [jax.ShapeDtypeStruct((t, D_MODEL), F32)]
    aliases = {}
    at = layer if states else None
    for k, (shape, block, index) in enumerate(_state_layout(t // seq, seq, seq)):
        out_specs.append(pl.BlockSpec(block(at), index(at)))
        out_shape.append(jax.ShapeDtypeStruct(shape, F32))
        if states:
            aliases[len(args)] = 1 + k
            in_specs.append(pl.BlockSpec(memory_space=pl.ANY))
            args.append(states[k])
    in_specs += [_layer_spec((4, DA_SUB), layer), _layer_spec((1, LANES), layer),
                 _layer_spec((3, GROUP_W), layer), _layer_spec((D_MODEL, D_MODEL), layer),
                 pl.BlockSpec((1, D_MODEL), lambda b: (0, 0))]
    args += [lam_vecs, subln_g2, conv_w, w_out, final_g]
    slabs = [N_PAIRS * TOK_W, N_PAIRS * SEQ_W, N_PAIRS * LOC_W, CONV_W]
    return pl.pallas_call(
        functools.partial(_ctx_layer_kernel, lam_init=lam_init, final=final, states=mode),
        grid=(t // seq,),
        in_specs=in_specs,
        out_specs=out_specs,
        out_shape=out_shape,
        scratch_shapes=[pltpu.VMEM((seq, w), BF16) for w in slabs],
        input_output_aliases=aliases,
        compiler_params=_params(1),
        name=name,
    )(*args)


def _attn_call(x, mod, u_tok, u_seq, u_loc, u_conv, cache, bias, lam_vecs, subln_g2, conv_w, w_out,
               final_g, *, layer, lam_init, nb, seq, tq, mod_row0, final, name):
    nq = seq // tq
    past = cache.shape[3]
    base = lambda i: jnp.clip(i - 1, 0, nq - LOCAL_CHUNKS)

    def bias_map(b, i):
        pattern = jnp.where(i == 0, 0, jnp.where(i == nq - 1, 2, 1))
        return (layer, pattern, 0, 0, 0)

    in_specs = [pl.BlockSpec((tq, N_PAIRS * TOK_W), lambda b, i: (b * nq + i, 0)),
                pl.BlockSpec((seq, N_PAIRS * SEQ_W), lambda b, i: (b, 0))]
    in_specs += [pl.BlockSpec((tq, N_PAIRS * LOC_W), lambda b, i, j=j: (b * nq + base(i) + j, 0))
                 for j in range(LOCAL_CHUNKS)]
    in_specs += [pl.BlockSpec((None, None, N_PAIRS, past, CACHE_W), lambda b, i: (b, layer, 0, 0, 0)),
                 pl.BlockSpec((None, None, N_HEADS_GRP, tq, LOCAL_CHUNKS * tq), bias_map),
                 _layer_spec((4, DA_SUB), layer), _layer_spec((1, LANES), layer)]
    args = [u_tok, u_seq] + [u_loc] * LOCAL_CHUNKS + [cache, bias, lam_vecs, subln_g2]

    halo = BF16_SUBLANES
    hb = tq // halo
    last = nb * seq // halo - 1
    blk = lambda b, i: b * nq + i
    conv = lambda off: pl.BlockSpec((tq, GROUP_W), lambda b, i: (blk(b, i), off // GROUP_W))
    in_specs += [
        pl.BlockSpec((tq, D_MODEL), lambda b, i: (blk(b, i), 0)),
        pl.BlockSpec((None, None, 3, D_MODEL), lambda b, i: (layer, mod_row0 + b, 0, 0)),
        conv(B_BB), conv(B_G),
        pl.BlockSpec((halo, GROUP_W),
                     lambda b, i: (jnp.maximum(blk(b, i) * hb - 1, 0), B_G // GROUP_W)),
        pl.BlockSpec((halo, GROUP_W),
                     lambda b, i: (jnp.minimum((blk(b, i) + 1) * hb, last), B_G // GROUP_W)),
        conv(B_SBZ),
        _layer_spec((3, GROUP_W), layer),
        _layer_spec((D_MODEL, D_MODEL), layer),
        pl.BlockSpec((1, D_MODEL), lambda b, i: (0, 0)),
    ]
    args += [x, mod, u_conv, u_conv, u_conv, u_conv, u_conv, conv_w, w_out, final_g]
    return pl.pallas_call(
        functools.partial(_attn_kernel, cached=True, lam_init=lam_init, final=final),
        grid=(nb, nq),
        in_specs=in_specs,
        out_specs=pl.BlockSpec((tq, D_MODEL), lambda b, i: (blk(b, i), 0)),
        out_shape=jax.ShapeDtypeStruct((nb * seq, D_MODEL), F32),
        compiler_params=_params(2),
        name=name,
    )(*args)


def _local_bias_tables(rpb, rows):
    nq = rows // Q_ROWS
    n_dy, n_dx = 2 * NA_WIN_H - 1, 2 * NA_WIN_W - 1
    qc = np.arange(GRID_W)[:, None]
    kc = np.arange(GRID_W)[None, :]
    ws = np.clip(qc - NA_WIN_W // 2, 0, GRID_W - NA_WIN_W)
    col_ok = (kc >= ws) & (kc < ws + NA_WIN_W)
    oh_dx = ((kc - qc + NA_WIN_W - 1)[None] == np.arange(n_dx)[:, None, None]) & col_ok[None]
    tile_idx = []
    for i in (0, 1, nq - 1):
        row0 = Q_ROWS * int(np.clip(i - 1, 0, nq - LOCAL_CHUNKS))
        r = (Q_ROWS * i + np.arange(Q_ROWS))[:, None]
        kr = (row0 + np.arange(LOCAL_ROWS))[None, :]
        rs = np.clip(r - NA_WIN_H // 2, 0, rows - NA_WIN_H)
        ok = (kr >= rs) & (kr < rs + NA_WIN_H)
        tile_idx.append(np.where(ok, kr - r + NA_WIN_H - 1, n_dy))
    tile_idx = np.stack(tile_idx)
    oh_dx2, col_ok2 = np.concatenate([oh_dx, oh_dx], axis=-1), np.concatenate([col_ok, col_ok], axis=-1)
    hi = lax.Precision.HIGHEST
    cols = jnp.einsum("lhyd,dqc->lhyqc", rpb.astype(F32), jnp.asarray(oh_dx2, F32), precision=hi)
    cols = jnp.where(col_ok2, cols * LOG2E, NEG_INF)

    def build(cols_ref, o_ref):
        left = _lane_mask(0, GRID_W)
        masked = jnp.full((GRID_W, LANES), NEG_INF, F32)
        tile = lambda y: masked if y == n_dy else cols_ref[y]
        for p in range(3):
            for j in range(Q_ROWS):
                for m in range(LOCAL_ROWS // 2):
                    a, b = int(tile_idx[p, j, 2 * m]), int(tile_idx[p, j, 2 * m + 1])
                    o_ref[p, GRID_W * j:GRID_W * (j + 1), LANES * m:LANES * (m + 1)] = jnp.where(
                        left, tile(a), tile(b))

    return pl.pallas_call(
        build,
        grid=(DEPTH, N_HEADS_GRP),
        in_specs=[pl.BlockSpec((None, None, n_dy, GRID_W, LANES), lambda l, h: (l, h, 0, 0, 0))],
        out_specs=pl.BlockSpec((None, 3, None, TQ, LOCAL_ROWS * GRID_W), lambda l, h: (l, 0, h, 0, 0)),
        out_shape=jax.ShapeDtypeStruct((DEPTH, 3, N_HEADS_GRP, TQ, LOCAL_ROWS * GRID_W), F32),
        compiler_params=_params(2),
        name="local_bias",
    )(cols)


def _w_in_prep_kernel(w_ref, o_ref):
    rows = w_ref.shape[1]
    kpe_end = W_KPE + MLA_ROPE
    o_ref[:, :W_KPE] = w_ref[:W_KPE, :].T.astype(BF16)
    kpe = jnp.concatenate([jnp.zeros((KPE_LANE, rows), F32), w_ref[W_KPE:kpe_end, :],
                           jnp.zeros((LANES - KPE_LANE - MLA_ROPE, rows), F32)], axis=0)
    o_ref[:, W_KPE:W_CZ] = kpe.T.astype(BF16)
    o_ref[:, W_CZ:] = w_ref[kpe_end:, :].T.astype(BF16)


def _w_in_prep(w_in):
    d_in = w_in.shape[-1]
    rows = W_PREP_ROWS
    return pl.pallas_call(
        _w_in_prep_kernel,
        grid=(DEPTH, D_MODEL // rows),
        in_specs=[pl.BlockSpec((None, d_in, rows), lambda l, r: (l, 0, r))],
        out_specs=pl.BlockSpec((None, rows, D_IN_P), lambda l, r: (l, r, 0)),
        out_shape=jax.ShapeDtypeStruct((DEPTH, D_MODEL, D_IN_P), BF16),
        compiler_params=_params(2),
        name="w_in_prep",
    )(jnp.swapaxes(w_in, 1, 2))


def _rope_tables(seq):
    f32 = np.float32
    t = np.arange(seq)
    rows = (t // GRID_W).astype(f32)
    cols = (t % GRID_W).astype(f32)
    half = DA_SUB // 2
    inv = (f32(1.0) / (f32(ROPE_BASE) ** (np.arange(0, half, 2, dtype=f32) / f32(half)))).astype(f32)
    ar = rows[:, None] * inv
    ac = cols[:, None] * inv
    ang = np.concatenate([ar, ar, ac, ac], axis=-1)
    cos, sin = np.cos(ang).astype(f32), np.sin(ang).astype(f32)
    first = (np.arange(DA_SUB) % 16 < 8)[None, :]
    sin_neg = np.where(first, -sin, f32(0.0))
    sin_pos = np.where(first, f32(0.0), sin)
    tile = lambda a: jnp.asarray(np.tile(a, (1, LANES // DA_SUB)), F32)

    def pad(a, fill):
        return jnp.asarray(np.concatenate(
            [np.full((seq, KPE_LANE), fill, f32), a,
             np.full((seq, LANES - KPE_LANE - MLA_ROPE), fill, f32)], axis=1), F32)

    return ([tile(cos), tile(sin_neg), tile(sin_pos)],
            [pad(cos, 1.0), pad(sin_neg, 0.0), pad(sin_pos, 0.0)])


def _pad_heads(w, width, take):
    d, k, _ = w.shape
    w = w.reshape(d, k, N_HEADS_GRP, width)[..., :take]
    return jnp.pad(w, ((0, 0), (0, 0), (0, 0), (0, LANES - take))).reshape(d, k, N_HEADS_GRP * LANES)


def kernel(x_prompt, x_sample, cache_a_k, cache_a_v, cache_c_kv, cache_c_kpe, cache_d_k, cache_d_v,
           c, c_ctx, ada_w, ada_b, norm_g, w_in, da_lambda, da_subln_g, conv_w,
           mla_q_norm_g, mla_w_uq, mla_kv_norm_g, mla_w_ukv, na_rpb, w_out, final_norm_g):
    batch, seq, _ = x_prompt.shape
    dec_batch, dec_seq, _ = x_sample.shape
    past = cache_a_k.shape[2]
    assert dec_seq % TQ == 0 and dec_seq // TQ >= LOCAL_CHUNKS and seq % LANES == 0
    assert dec_seq % IN_PROJ_ROWS == 0 and D_MODEL % W_PREP_ROWS == 0

    w_in_p = _w_in_prep(w_in)
    w_out_b = w_out.astype(BF16)
    wuq = _pad_heads(mla_w_uq, MLA_NOPE + MLA_ROPE, MLA_NOPE + MLA_ROPE).astype(BF16)
    wuk = _pad_heads(mla_w_ukv, MLA_NOPE + MLA_V, MLA_NOPE).astype(BF16)
    wuv = mla_w_ukv.reshape(DEPTH, MLA_KV_RANK, N_HEADS_GRP, MLA_NOPE + MLA_V)[..., MLA_NOPE:]
    wuv = wuv.reshape(DEPTH, MLA_KV_RANK, GROUP_W).astype(BF16)
    subln_g2 = jnp.tile(da_subln_g, (1, LANES // HEAD_DIM)).reshape(DEPTH, 1, LANES)
    ng = norm_g.reshape(DEPTH, 1, D_MODEL)
    qng = mla_q_norm_g.reshape(DEPTH, 1, MLA_Q_RANK)
    kvng = mla_kv_norm_g.reshape(DEPTH, 1, MLA_KV_RANK)
    fg = final_norm_g.reshape(1, D_MODEL)
    tables_a, tables_c = _rope_tables(dec_seq)
    tables = tables_a + tables_c

    cvec8 = jnp.concatenate([c_ctx[None], c, jnp.zeros((8 - 1 - dec_batch, D_MODEL), F32)], axis=0)
    mod = _ada_call(cvec8, ada_w, ada_b).reshape(DEPTH, 8, 3, D_MODEL)

    feat = lambda a: a.transpose(0, 1, 3, 4, 2).reshape(dec_batch, DEPTH, GROUP_W, past)
    cache = _cache_call(feat(cache_a_k), feat(cache_a_v), cache_c_kv,
                        cache_c_kpe.transpose(0, 1, 3, 2), feat(cache_d_k), feat(cache_d_v), wuk, wuv)
    bias = _local_bias_tables(na_rpb, dec_seq // GRID_W)

    xp = x_prompt.reshape(batch * seq, D_MODEL)
    xs = x_sample.reshape(dec_batch * dec_seq, D_MODEL)
    states = []
    for l in range(DEPTH):
        lam_init = 0.8 - 0.6 * math.exp(-0.3 * l)
        final = l == DEPTH - 1

        xp, *states = _ctx_layer_call(xp, mod, ng, w_in_p, qng, wuq, kvng, wuk, wuv, da_lambda,
                                      subln_g2, conv_w, w_out_b, fg, states, layer=l,
                                      lam_init=lam_init, seq=seq, final=final, name=f"ctx_layer_{l}")

        tok, sq, loc, cv = _in_call(xs, mod, ng, w_in_p, qng, wuq, kvng, wuk, wuv, tables,
                                    layer=l, seq=dec_seq, tm=IN_PROJ_ROWS, mod_row0=1,
                                    name=f"lat_in_{l}")
        xs = _attn_call(xs, mod, tok, sq, loc, cv, cache, bias, da_lambda, subln_g2, conv_w, w_out_b,
                        fg, layer=l, lam_init=lam_init, nb=dec_batch, seq=dec_seq, tq=TQ, mod_row0=1,
                        final=final, name=f"lat_attn_{l}")

    def heads(a):
        return a.reshape(batch, DEPTH, N_HEADS_GRP, HEAD_DIM, seq).transpose(0, 1, 4, 2, 3)

    s_ak, s_av, s_ckv, s_kpe, s_dk, s_dv = states
    return (xp.reshape(batch, seq, D_MODEL), xs.reshape(dec_batch, dec_seq, D_MODEL),
            heads(s_ak), heads(s_av), s_ckv, s_kpe.transpose(0, 1, 3, 2), heads(s_dk), heads(s_dv))
```

```python
import functools
import math

import jax
import jax.numpy as jnp
import numpy as np
from jax import lax
from jax.experimental import pallas as pl
from jax.experimental.pallas import tpu as pltpu

F32 = jnp.float32
BF16 = jnp.bfloat16

D_MODEL = 1024
DEPTH = 4
GRID_W = 64
HEAD_DIM = 64
GROUP_W = 256
N_HEADS_GRP = 4
DA_SUB = 32
MLA_Q_RANK = 256
MLA_KV_RANK = 128
MLA_NOPE = 64
MLA_ROPE = 32
MLA_V = 64
LOG2E = math.log2(math.e)
MLA_SCALE = (MLA_NOPE + MLA_ROPE) ** -0.5
DA_SCALE = DA_SUB ** -0.5
NA_SCALE = HEAD_DIM ** -0.5
NA_WIN_H = 8
NA_WIN_W = 16
ROPE_BASE = 10000.0
EPS = 1e-6
NEG_INF = -1e30

LANES = 128
BF16_SUBLANES = 16
N_PAIRS = GROUP_W // LANES
VMEM_LIMIT = 56 * 1024 * 1024
IN_PROJ_ROWS = 1024
W_PREP_ROWS = 256

W_AQ, W_AK, W_AV, W_AZ = 0, 256, 512, 768
W_BB, W_BC, W_BH, W_BZ = 1024, 1280, 1536, 1792
W_CQ, W_CKV, W_KPE, W_CZ = 2048, 2304, 2432, 2560
W_DQ, W_DK, W_DV, W_DZ = 2816, 3072, 3328, 3584
D_IN_P = 3840
KPE_LANE = 64

T_QA, T_SZA, T_QC, T_SCZ, T_QD, T_SDZ, TOK_W = 0, 128, 256, 512, 640, 768, 896
S_KA, S_VA, S_KC, S_VC, SEQ_W = 0, 128, 256, 512, 640
L_KD, L_VD, LOC_W = 0, 128, 256
X_KA, X_VA, X_KC, X_VC, X_KD, X_VD, CACHE_W = 0, 128, 256, 512, 640, 768, 896
B_BB, B_G, B_SBZ, CONV_W = 0, 256, 512, 768

Q_ROWS = 4
TQ = Q_ROWS * GRID_W
LOCAL_CHUNKS = 3
LOCAL_ROWS = LOCAL_CHUNKS * Q_ROWS


def _dot(a, b):
    return jnp.dot(a, b, preferred_element_type=F32)


def _dot_nt(a, b):
    return lax.dot_general(a, b, (((1,), (1,)), ((), ())), preferred_element_type=F32)


def _silu(z):
    return z * (1.0 / (1.0 + jnp.exp(-z)))


def _rms(x, g):
    return x * lax.rsqrt(jnp.mean(x * x, axis=-1, keepdims=True) + EPS) * g


def _rope128(x, cos, sin_neg, sin_pos):
    return x * cos + pltpu.roll(x, LANES - 8, 1) * sin_neg + pltpu.roll(x, 8, 1) * sin_pos


def _scores(job):
    q, keys, biases, _, _ = job
    blocks = []
    for k, b in zip(keys, biases):
        s = _dot_nt(q, k)
        blocks.append(s if b is None else s + b)
    return blocks


def _weighted_values(job, blocks):
    _, _, _, values, sum_lane = job
    m = functools.reduce(jnp.maximum, [jnp.max(s, axis=-1, keepdims=True) for s in blocks])
    one_hot = _lane_mask(sum_lane, 1)
    o = functools.reduce(jnp.add, [
        _dot(jnp.exp2(s - m).astype(BF16), jnp.where(one_hot, jnp.ones_like(v), v))
        for s, v in zip(blocks, values)])
    l = jnp.sum(jnp.where(one_hot, o, 0.0), axis=-1, keepdims=True)
    return o / l


SCORES_IN_FLIGHT_BYTES = 6 * 1024 * 1024


def _attend_all(jobs, after):
    q, keys = jobs[0][0], jobs[0][1]
    score_bytes = 4 * q.shape[0] * sum(k.shape[0] for k in keys)
    ahead = max(1, min(len(jobs), SCORES_IN_FLIGHT_BYTES // score_bytes))
    outs = []
    pending = [_scores(job) for job in jobs[:ahead]]
    for k, job in enumerate(jobs):
        if k + ahead < len(jobs):
            pending.append(_scores(jobs[k + ahead]))
        outs.append(_weighted_values(job, pending.pop(0)))
        if k in after:
            after[k](outs)
    return outs


def _lane_mask(lo, width):
    lane = lax.broadcasted_iota(jnp.int32, (1, LANES), 1)
    return jnp.logical_and(lane >= lo, lane < lo + width)


def _params(n_axes):
    return pltpu.CompilerParams(dimension_semantics=("parallel",) * n_axes,
                                vmem_limit_bytes=VMEM_LIMIT)


def _layer_spec(shape, layer):
    zeros = (0,) * len(shape)
    return pl.BlockSpec((None,) + tuple(shape), lambda *_: (layer,) + zeros)


def _ada_kernel(c_ref, w_ref, b_ref, o_ref):
    s = _silu(c_ref[...]).astype(BF16)
    o_ref[...] = _dot(s, w_ref[...].astype(BF16)) + b_ref[...]


def _ada_call(cvec8, ada_w, ada_b):
    nj = 3
    return pl.pallas_call(
        _ada_kernel,
        grid=(DEPTH, nj),
        in_specs=[pl.BlockSpec((8, D_MODEL), lambda l, j: (0, 0)),
                  pl.BlockSpec((None, D_MODEL, D_MODEL), lambda l, j: (l, 0, j)),
                  pl.BlockSpec((None, None, 1, D_MODEL), lambda l, j: (l, j, 0, 0))],
        out_specs=pl.BlockSpec((None, 8, D_MODEL), lambda l, j: (l, 0, j)),
        out_shape=jax.ShapeDtypeStruct((DEPTH, 8, 3 * D_MODEL), F32),
        compiler_params=_params(2),
        name="adaln",
    )(cvec8, ada_w, ada_b.reshape(DEPTH, nj, 1, D_MODEL))


def _cache_kernel(ak_ref, av_ref, ckv_ref, kpe_ref, dk_ref, dv_ref, wuk_ref, wuv_ref, o_ref):
    ckv = ckv_ref[...].astype(BF16)
    kn = _dot(ckv, wuk_ref[...])
    vc = _dot(ckv, wuv_ref[...])
    past = ckv.shape[0]
    kpe = jnp.concatenate([jnp.zeros((KPE_LANE, past), F32), kpe_ref[...],
                           jnp.zeros((LANES - KPE_LANE - MLA_ROPE, past), F32)], axis=0).T
    for hp in range(N_PAIRS):
        half = slice(LANES * hp, LANES * (hp + 1))
        o_ref[hp, :, X_KA:X_KA + LANES] = ak_ref[half, :].T.astype(BF16)
        o_ref[hp, :, X_VA:X_VA + LANES] = av_ref[half, :].T.astype(BF16)
        for e in range(2):
            sl = slice(LANES * (2 * hp + e), LANES * (2 * hp + e + 1))
            o_ref[hp, :, X_KC + LANES * e:X_KC + LANES * (e + 1)] = (kn[:, sl] + kpe).astype(BF16)
        o_ref[hp, :, X_VC:X_VC + LANES] = vc[:, half].astype(BF16)
        o_ref[hp, :, X_KD:X_KD + LANES] = dk_ref[half, :].T.astype(BF16)
        o_ref[hp, :, X_VD:X_VD + LANES] = dv_ref[half, :].T.astype(BF16)


def _cache_call(ca_k, ca_v, c_kv, c_kpe, cd_k, cd_v, wuk, wuv):
    nb, _, past, _ = c_kv.shape
    cache = lambda w: pl.BlockSpec((None, None, past, w), lambda l, b: (b, l, 0, 0))
    feat = lambda rows: pl.BlockSpec((None, None, rows, past), lambda l, b: (b, l, 0, 0))
    return pl.pallas_call(
        _cache_kernel,
        grid=(DEPTH, nb),
        in_specs=[feat(GROUP_W), feat(GROUP_W), cache(MLA_KV_RANK), feat(MLA_ROPE),
                  feat(GROUP_W), feat(GROUP_W),
                  pl.BlockSpec((None, MLA_KV_RANK, 4 * LANES), lambda l, b: (l, 0, 0)),
                  pl.BlockSpec((None, MLA_KV_RANK, GROUP_W), lambda l, b: (l, 0, 0))],
        out_specs=pl.BlockSpec((None, None, N_PAIRS, past, CACHE_W), lambda l, b: (b, l, 0, 0, 0)),
        out_shape=jax.ShapeDtypeStruct((nb, DEPTH, N_PAIRS, past, CACHE_W), BF16),
        compiler_params=_params(2),
        name="cache_prep",
    )(ca_k, ca_v, c_kv, c_kpe, cd_k, cd_v, wuk, wuv)


def _in_kernel(*refs, rope, states):
    it = iter(refs)
    x_ref, mod_ref, ng_ref, w_ref, qng_ref, wuq_ref, kvng_ref, wuk_ref, wuv_ref = (
        next(it) for _ in range(9))
    if rope:
        ta = [next(it)[...] for _ in range(3)]
        tc = [next(it)[...] for _ in range(3)]
    if states == "update":
        for _ in range(6):
            next(it)
    tok_ref, seq_ref, loc_ref, conv_ref = (next(it) for _ in range(4))
    if states:
        sak_ref, sav_ref, sckv_ref, skpe_ref, sdk_ref, sdv_ref = (next(it) for _ in range(6))

    x = x_ref[...]
    h = _rms(x, ng_ref[...]) * (1.0 + mod_ref[1:2, :]) + mod_ref[0:1, :]
    hb = h.astype(BF16)

    def seg(off, n=GROUP_W):
        return _dot(hb, w_ref[:, off:off + n])

    def put_state(ref, val):
        if states == "create":
            ref[0] = val
            for later in range(1, DEPTH):
                ref[later] = jnp.zeros_like(val)
        else:
            ref[...] = val

    def put(ref, width, off, val):
        for hp in range(N_PAIRS):
            ref[:, width * hp + off:width * hp + off + LANES] = (
                val[:, LANES * hp:LANES * (hp + 1)].astype(BF16))

    cqn = _rms(seg(W_CQ), qng_ref[...]).astype(BF16)
    ckvn = _rms(seg(W_CKV, MLA_KV_RANK), kvng_ref[...])
    kpe = seg(W_KPE, LANES)
    if states:
        put_state(sckv_ref, ckvn)
        put_state(skpe_ref, kpe.T[KPE_LANE:KPE_LANE + MLA_ROPE, :])
    ckvb = ckvn.astype(BF16)

    aq, ak, av = seg(W_AQ), seg(W_AK), seg(W_AV)
    if states:
        put_state(sak_ref, ak.T)
        put_state(sav_ref, av.T)
    if rope:
        aq = jnp.concatenate([_rope128(aq[:, :LANES], *ta), _rope128(aq[:, LANES:], *ta)], axis=1)
        ak = jnp.concatenate([_rope128(ak[:, :LANES], *ta), _rope128(ak[:, LANES:], *ta)], axis=1)
    put(tok_ref, TOK_W, T_QA, aq * (DA_SCALE * LOG2E))
    put(seq_ref, SEQ_W, S_KA, ak)
    put(seq_ref, SEQ_W, S_VA, av)
    put(tok_ref, TOK_W, T_SZA, _silu(seg(W_AZ)))

    conv_ref[:, B_BB:B_BB + GROUP_W] = seg(W_BB).astype(BF16)
    conv_ref[:, B_G:B_G + GROUP_W] = (seg(W_BC) * seg(W_BH)).astype(BF16)
    conv_ref[:, B_SBZ:B_SBZ + GROUP_W] = _silu(seg(W_BZ)).astype(BF16)

    dk, dv = seg(W_DK), seg(W_DV)
    if states:
        put_state(sdk_ref, dk.T)
        put_state(sdv_ref, dv.T)
    put(tok_ref, TOK_W, T_QD, seg(W_DQ) * (NA_SCALE * LOG2E))
    put(loc_ref, LOC_W, L_KD, dk)
    put(loc_ref, LOC_W, L_VD, dv)
    put(tok_ref, TOK_W, T_SDZ, _silu(seg(W_DZ)))
    put(tok_ref, TOK_W, T_SCZ, _silu(seg(W_CZ)))

    q = _dot(cqn, wuq_ref[...])
    kn = _dot(ckvb, wuk_ref[...])
    vc = _dot(ckvb, wuv_ref[...])
    kpe_r = _rope128(kpe, *tc) if rope else kpe
    for hd in range(N_HEADS_GRP):
        sl = slice(LANES * hd, LANES * (hd + 1))
        hp, e = divmod(hd, 2)
        qh = q[:, sl]
        if rope:
            qh = _rope128(qh, *tc)
        o = LANES * e
        tok_ref[:, TOK_W * hp + T_QC + o:TOK_W * hp + T_QC + o + LANES] = (
            qh * (MLA_SCALE * LOG2E)).astype(BF16)
        seq_ref[:, SEQ_W * hp + S_KC + o:SEQ_W * hp + S_KC + o + LANES] = (
            kn[:, sl] + kpe_r).astype(BF16)
    put(seq_ref, SEQ_W, S_VC, vc)


def _state_layout(batch, seq, tm):
    nblk = seq // tm
    lblk = lambda layer: DEPTH if layer is None else None
    lidx = lambda layer: 0 if layer is None else layer
    feat = lambda rows: ((batch, DEPTH, rows, seq),
                         lambda layer: (None, lblk(layer), rows, tm),
                         lambda layer: lambda i: (i // nblk, lidx(layer), 0, i % nblk))
    tokm = ((batch, DEPTH, seq, MLA_KV_RANK),
            lambda layer: (None, lblk(layer), tm, MLA_KV_RANK),
            lambda layer: lambda i: (i // nblk, lidx(layer), i % nblk, 0))
    return [feat(GROUP_W), feat(GROUP_W), tokm, feat(MLA_ROPE), feat(GROUP_W), feat(GROUP_W)]


def _in_call(x, mod, norm_g, w_in, q_norm_g, wuq, kv_norm_g, wuk, wuv, tables, *, layer, seq, tm,
             mod_row0, name):
    t = x.shape[0]
    nblk = seq // tm
    in_specs = [
        pl.BlockSpec((tm, D_MODEL), lambda i: (i, 0)),
        pl.BlockSpec((None, None, 3, D_MODEL), lambda i: (layer, mod_row0 + i // nblk, 0, 0)),
        _layer_spec((1, D_MODEL), layer),
        _layer_spec((D_MODEL, D_IN_P), layer),
        _layer_spec((1, MLA_Q_RANK), layer),
        _layer_spec((MLA_Q_RANK, 4 * LANES), layer),
        _layer_spec((1, MLA_KV_RANK), layer),
        _layer_spec((MLA_KV_RANK, 4 * LANES), layer),
        _layer_spec((MLA_KV_RANK, GROUP_W), layer),
    ]
    in_specs += [pl.BlockSpec((tm, LANES), lambda i: (i % nblk, 0))] * len(tables)
    args = [x, mod, norm_g, w_in, q_norm_g, wuq, kv_norm_g, wuk, wuv] + list(tables)
    widths = [N_PAIRS * TOK_W, N_PAIRS * SEQ_W, N_PAIRS * LOC_W, CONV_W]
    return pl.pallas_call(
        functools.partial(_in_kernel, rope=True, states=False),
        grid=(t // tm,),
        in_specs=in_specs,
        out_specs=[pl.BlockSpec((tm, w), lambda i: (i, 0)) for w in widths],
        out_shape=[jax.ShapeDtypeStruct((t, w), BF16) for w in widths],
        compiler_params=_params(1),
        name=name,
    )(*args)


def _attn_kernel(*refs, cached, lam_init, final):
    it = iter(refs)
    tok_ref, seq_ref = next(it), next(it)
    loc_refs = [next(it) for _ in range(LOCAL_CHUNKS if cached else 1)]
    if cached:
        x_ref, bias_ref = next(it), next(it)
    lam_ref, g_ref = next(it), next(it)
    res_ref, mod_ref, bb_ref, gc_ref = (next(it) for _ in range(4))
    if cached:
        gp_ref, gn_ref = next(it), next(it)
    sbz_ref, cw_ref, w_ref, fg_ref = (next(it) for _ in range(4))
    o_ref = next(it)

    lv = lam_ref[...]
    lam = (jnp.exp(jnp.sum(lv[0:1] * lv[1:2], keepdims=True))
           - jnp.exp(jnp.sum(lv[2:3] * lv[3:4], keepdims=True)) + lam_init)

    tq = res_ref.shape[0]
    gc = gc_ref[...].astype(F32)
    rows = lax.broadcasted_iota(jnp.int32, (tq, 1), 0)
    if cached:
        i, nq = pl.program_id(1), pl.num_programs(1)
        halo = gp_ref.shape[0]
        edge_prev = gp_ref[halo - 1:halo, :].astype(F32) * jnp.where(i != 0, 1.0, 0.0)
        edge_next = gn_ref[0:1, :].astype(F32) * jnp.where(i != nq - 1, 1.0, 0.0)
    else:
        edge_prev = edge_next = 0.0
    g_prev = jnp.where(rows == 0, edge_prev, pltpu.roll(gc, 1, 0))
    g_next = jnp.where(rows == tq - 1, edge_next, pltpu.roll(gc, tq - 1, 0))
    cw = cw_ref[...]
    conv = g_prev * cw[0:1] + gc * cw[1:2] + g_next * cw[2:3]
    yb = (bb_ref[...].astype(F32) * conv * sbz_ref[...].astype(F32)).astype(BF16)

    first = _lane_mask(0, HEAD_DIM)
    proj = []

    def gate(hp, off):
        return tok_ref[:, TOK_W * hp + off:TOK_W * hp + off + LANES].astype(F32)

    def project(group, halves):
        y = jnp.concatenate(halves, axis=1).astype(BF16)
        proj.append(_dot(y, w_ref[GROUP_W * group:GROUP_W * (group + 1), :]))

    def after_a(outs):
        proj.append(_dot(yb, w_ref[GROUP_W:2 * GROUP_W, :]))
        halves = []
        for hp in range(N_PAIRS):
            m00, m01, m10, m11 = outs[4 * hp:4 * hp + 4]
            o = jnp.where(first, m00 - lam * m01, m10 - lam * m11)
            o2 = o * o
            ss0 = jnp.sum(jnp.where(first, o2, 0.0), axis=-1, keepdims=True)
            ss1 = jnp.sum(jnp.where(first, 0.0, o2), axis=-1, keepdims=True)
            ms = jnp.where(first, ss0, ss1) * (1.0 / HEAD_DIM)
            y = o * lax.rsqrt(ms + EPS) * g_ref[...] * (1.0 - lam_init)
            halves.append(y * gate(hp, T_SZA))
        project(0, halves)

    def after_group(group, base, off):
        def run(outs):
            project(group, [jnp.where(first, outs[base + 2 * hp], outs[base + 2 * hp + 1])
                            * gate(hp, off) for hp in range(N_PAIRS)])
        return run

    jobs_a, jobs_c, jobs_d = [], [], []
    for hp in range(N_PAIRS):
        def col(ref, width, off, w=LANES, hp=hp):
            return ref[:, width * hp + off:width * hp + off + w]

        xcol = lambda off, hp=hp: x_ref[hp, :, off:off + LANES]
        qa = col(tok_ref, TOK_W, T_QA)
        keys = [col(seq_ref, SEQ_W, S_KA)] + ([xcol(X_KA)] if cached else [])
        vals = [col(seq_ref, SEQ_W, S_VA)] + ([xcol(X_VA)] if cached else [])
        nob = [None] * len(keys)
        for e in range(2):
            for c in range(2):
                qm = jnp.where(_lane_mask(HEAD_DIM * e + DA_SUB * c, DA_SUB), qa, jnp.zeros_like(qa))
                jobs_a.append((qm, keys, nob, vals, HEAD_DIM * (1 - e)))
        vals = [col(seq_ref, SEQ_W, S_VC)] + ([xcol(X_VC)] if cached else [])
        for e in range(2):
            keys = [col(seq_ref, SEQ_W, S_KC + LANES * e)] + (
                [xcol(X_KC + LANES * e)] if cached else [])
            jobs_c.append((col(tok_ref, TOK_W, T_QC + LANES * e), keys, nob, vals,
                           HEAD_DIM * (1 - e)))
        qd = col(tok_ref, TOK_W, T_QD)
        keys = [col(r, LOC_W, L_KD) for r in loc_refs] + ([xcol(X_KD)] if cached else [])
        vals = [col(r, LOC_W, L_VD) for r in loc_refs] + ([xcol(X_VD)] if cached else [])
        for e in range(2):
            qm = jnp.where(_lane_mask(HEAD_DIM * e, HEAD_DIM), qd, jnp.zeros_like(qd))
            biases = [None] * len(keys)
            if cached:
                biases = [bias_ref[2 * hp + e, :, TQ * j:TQ * (j + 1)]
                          for j in range(len(loc_refs))] + [None]
            jobs_d.append((qm, keys, biases, vals, HEAD_DIM * (1 - e)))

    n_a, n = len(jobs_a), 2 * N_PAIRS
    after = {n_a - 1: after_a, n_a + n - 1: after_group(2, n_a, T_SCZ),
             n_a + 2 * n - 1: after_group(3, n_a + n, T_SDZ)}
    _attend_all(jobs_a + jobs_c + jobs_d, after)
    xn = res_ref[...] + mod_ref[2:3, :] * functools.reduce(jnp.add, proj)
    if final:
        xn = _rms(xn, fg_ref[...])
    o_ref[...] = xn


def _ctx_layer_kernel(*refs, lam_init, final, states):
    it = iter(refs)
    in_refs = [next(it) for _ in range(9)]
    alias_refs = [next(it) for _ in range(6)] if states == "update" else []
    lam_ref, g_ref, cw_ref, w_out_ref, fg_ref = (next(it) for _ in range(5))
    o_ref = next(it)
    state_refs = [next(it) for _ in range(6)]
    tok_s, seq_s, loc_s, conv_s = (next(it) for _ in range(4))
    _in_kernel(*in_refs, *alias_refs, tok_s, seq_s, loc_s, conv_s, *state_refs,
               rope=False, states=states)
    view = lambda off: conv_s.at[:, off:off + GROUP_W]
    _attn_kernel(tok_s, seq_s, loc_s, lam_ref, g_ref, in_refs[0], in_refs[1], view(B_BB), view(B_G),
                 view(B_SBZ), cw_ref, w_out_ref, fg_ref, o_ref,
                 cached=False, lam_init=lam_init, final=final)


def _ctx_layer_call(x, mod, norm_g, w_in, q_norm_g, wuq, kv_norm_g, wuk, wuv, lam_vecs, subln_g2,
                    conv_w, w_out, final_g, states, *, layer, lam_init, seq, final, name):
    t = x.shape[0]
    in_specs = [
        pl.BlockSpec((seq, D_MODEL), lambda b: (b, 0)),
        pl.BlockSpec((None, None, 3, D_MODEL), lambda b: (layer, 0, 0, 0)),
        _layer_spec((1, D_MODEL), layer),
        _layer_spec((D_MODEL, D_IN_P), layer),
        _layer_spec((1, MLA_Q_RANK), layer),
        _layer_spec((MLA_Q_RANK, 4 * LANES), layer),
        _layer_spec((1, MLA_KV_RANK), layer),
        _layer_spec((MLA_KV_RANK, 4 * LANES), layer),
        _layer_spec((MLA_KV_RANK, GROUP_W), layer),
    ]
    args = [x, mod, norm_g, w_in, q_norm_g, wuq, kv_norm_g, wuk, wuv]
    mode = "update" if states else "create"
    assert states or layer == 0
    out_specs = [pl.BlockSpec((seq, D_MODEL), lambda b: (b, 0))]
    out_shape = [jax.ShapeDtypeStruct((t, D_MODEL), F32)]
    aliases = {}
    at = layer if states else None
    for k, (shape, block, index) in enumerate(_state_layout(t // seq, seq, seq)):
        out_specs.append(pl.BlockSpec(block(at), index(at)))
        out_shape.append(jax.ShapeDtypeStruct(shape, F32))
        if states:
            aliases[len(args)] = 1 + k
            in_specs.append(pl.BlockSpec(memory_space=pl.ANY))
            args.append(states[k])
    in_specs += [_layer_spec((4, DA_SUB), layer), _layer_spec((1, LANES), layer),
                 _layer_spec((3, GROUP_W), layer), _layer_spec((D_MODEL, D_MODEL), layer),
                 pl.BlockSpec((1, D_MODEL), lambda b: (0, 0))]
    args += [lam_vecs, subln_g2, conv_w, w_out, final_g]
    slabs = [N_PAIRS * TOK_W, N_PAIRS * SEQ_W, N_PAIRS * LOC_W, CONV_W]
    return pl.pallas_call(
        functools.partial(_ctx_layer_kernel, lam_init=lam_init, final=final, states=mode),
        grid=(t // seq,),
        in_specs=in_specs,
        out_specs=out_specs,
        out_shape=out_shape,
        scratch_shapes=[pltpu.VMEM((seq, w), BF16) for w in slabs],
        input_output_aliases=aliases,
        compiler_params=_params(1),
        name=name,
    )(*args)


def _attn_call(x, mod, u_tok, u_seq, u_loc, u_conv, cache, bias, lam_vecs, subln_g2, conv_w, w_out,
               final_g, *, layer, lam_init, nb, seq, tq, mod_row0, final, name):
    nq = seq // tq
    past = cache.shape[3]
    base = lambda i: jnp.clip(i - 1, 0, nq - LOCAL_CHUNKS)

    def bias_map(b, i):
        pattern = jnp.where(i == 0, 0, jnp.where(i == nq - 1, 2, 1))
        return (layer, pattern, 0, 0, 0)

    in_specs = [pl.BlockSpec((tq, N_PAIRS * TOK_W), lambda b, i: (b * nq + i, 0)),
                pl.BlockSpec((seq, N_PAIRS * SEQ_W), lambda b, i: (b, 0))]
    in_specs += [pl.BlockSpec((tq, N_PAIRS * LOC_W), lambda b, i, j=j: (b * nq + base(i) + j, 0))
                 for j in range(LOCAL_CHUNKS)]
    in_specs += [pl.BlockSpec((None, None, N_PAIRS, past, CACHE_W), lambda b, i: (b, layer, 0, 0, 0)),
                 pl.BlockSpec((None, None, N_HEADS_GRP, tq, LOCAL_CHUNKS * tq), bias_map),
                 _layer_spec((4, DA_SUB), layer), _layer_spec((1, LANES), layer)]
    args = [u_tok, u_seq] + [u_loc] * LOCAL_CHUNKS + [cache, bias, lam_vecs, subln_g2]

    halo = BF16_SUBLANES
    hb = tq // halo
    last = nb * seq // halo - 1
    blk = lambda b, i: b * nq + i
    conv = lambda off: pl.BlockSpec((tq, GROUP_W), lambda b, i: (blk(b, i), off // GROUP_W))
    in_specs += [
        pl.BlockSpec((tq, D_MODEL), lambda b, i: (blk(b, i), 0)),
        pl.BlockSpec((None, None, 3, D_MODEL), lambda b, i: (layer, mod_row0 + b, 0, 0)),
        conv(B_BB), conv(B_G),
        pl.BlockSpec((halo, GROUP_W),
                     lambda b, i: (jnp.maximum(blk(b, i) * hb - 1, 0), B_G // GROUP_W)),
        pl.BlockSpec((halo, GROUP_W),
                     lambda b, i: (jnp.minimum((blk(b, i) + 1) * hb, last), B_G // GROUP_W)),
        conv(B_SBZ),
        _layer_spec((3, GROUP_W), layer),
        _layer_spec((D_MODEL, D_MODEL), layer),
        pl.BlockSpec((1, D_MODEL), lambda b, i: (0, 0)),
    ]
    args += [x, mod, u_conv, u_conv, u_conv, u_conv, u_conv, conv_w, w_out, final_g]
    return pl.pallas_call(
        functools.partial(_attn_kernel, cached=True, lam_init=lam_init, final=final),
        grid=(nb, nq),
        in_specs=in_specs,
        out_specs=pl.BlockSpec((tq, D_MODEL), lambda b, i: (blk(b, i), 0)),
        out_shape=jax.ShapeDtypeStruct((nb * seq, D_MODEL), F32),
        compiler_params=_params(2),
        name=name,
    )(*args)


def _local_bias_tables(rpb, rows):
    nq = rows // Q_ROWS
    n_dy, n_dx = 2 * NA_WIN_H - 1, 2 * NA_WIN_W - 1
    qc = np.arange(GRID_W)[:, None]
    kc = np.arange(GRID_W)[None, :]
    ws = np.clip(qc - NA_WIN_W // 2, 0, GRID_W - NA_WIN_W)
    col_ok = (kc >= ws) & (kc < ws + NA_WIN_W)
    oh_dx = ((kc - qc + NA_WIN_W - 1)[None] == np.arange(n_dx)[:, None, None]) & col_ok[None]
    tile_idx = []
    for i in (0, 1, nq - 1):
        row0 = Q_ROWS * int(np.clip(i - 1, 0, nq - LOCAL_CHUNKS))
        r = (Q_ROWS * i + np.arange(Q_ROWS))[:, None]
        kr = (row0 + np.arange(LOCAL_ROWS))[None, :]
        rs = np.clip(r - NA_WIN_H // 2, 0, rows - NA_WIN_H)
        ok = (kr >= rs) & (kr < rs + NA_WIN_H)
        tile_idx.append(np.where(ok, kr - r + NA_WIN_H - 1, n_dy))
    tile_idx = np.stack(tile_idx)
    oh_dx2, col_ok2 = np.concatenate([oh_dx, oh_dx], axis=-1), np.concatenate([col_ok, col_ok], axis=-1)
    hi = lax.Precision.HIGHEST
    cols = jnp.einsum("lhyd,dqc->lhyqc", rpb.astype(F32), jnp.asarray(oh_dx2, F32), precision=hi)
    cols = jnp.where(col_ok2, cols * LOG2E, NEG_INF)

    def build(cols_ref, o_ref):
        left = _lane_mask(0, GRID_W)
        masked = jnp.full((GRID_W, LANES), NEG_INF, F32)
        tile = lambda y: masked if y == n_dy else cols_ref[y]
        for p in range(3):
            for j in range(Q_ROWS):
                for m in range(LOCAL_ROWS // 2):
                    a, b = int(tile_idx[p, j, 2 * m]), int(tile_idx[p, j, 2 * m + 1])
                    o_ref[p, GRID_W * j:GRID_W * (j + 1), LANES * m:LANES * (m + 1)] = jnp.where(
                        left, tile(a), tile(b))

    return pl.pallas_call(
        build,
        grid=(DEPTH, N_HEADS_GRP),
        in_specs=[pl.BlockSpec((None, None, n_dy, GRID_W, LANES), lambda l, h: (l, h, 0, 0, 0))],
        out_specs=pl.BlockSpec((None, 3, None, TQ, LOCAL_ROWS * GRID_W), lambda l, h: (l, 0, h, 0, 0)),
        out_shape=jax.ShapeDtypeStruct((DEPTH, 3, N_HEADS_GRP, TQ, LOCAL_ROWS * GRID_W), F32),
        compiler_params=_params(2),
        name="local_bias",
    )(cols)


def _w_in_prep_kernel(w_ref, o_ref):
    rows = w_ref.shape[1]
    kpe_end = W_KPE + MLA_ROPE
    o_ref[:, :W_KPE] = w_ref[:W_KPE, :].T.astype(BF16)
    kpe = jnp.concatenate([jnp.zeros((KPE_LANE, rows), F32), w_ref[W_KPE:kpe_end, :],
                           jnp.zeros((LANES - KPE_LANE - MLA_ROPE, rows), F32)], axis=0)
    o_ref[:, W_KPE:W_CZ] = kpe.T.astype(BF16)
    o_ref[:, W_CZ:] = w_ref[kpe_end:, :].T.astype(BF16)


def _w_in_prep(w_in):
    d_in = w_in.shape[-1]
    rows = W_PREP_ROWS
    return pl.pallas_call(
        _w_in_prep_kernel,
        grid=(DEPTH, D_MODEL // rows),
        in_specs=[pl.BlockSpec((None, d_in, rows), lambda l, r: (l, 0, r))],
        out_specs=pl.BlockSpec((None, rows, D_IN_P), lambda l, r: (l, r, 0)),
        out_shape=jax.ShapeDtypeStruct((DEPTH, D_MODEL, D_IN_P), BF16),
        compiler_params=_params(2),
        name="w_in_prep",
    )(jnp.swapaxes(w_in, 1, 2))


def _rope_tables(seq):
    f32 = np.float32
    t = np.arange(seq)
    rows = (t // GRID_W).astype(f32)
    cols = (t % GRID_W).astype(f32)
    half = DA_SUB // 2
    inv = (f32(1.0) / (f32(ROPE_BASE) ** (np.arange(0, half, 2, dtype=f32) / f32(half)))).astype(f32)
    ar = rows[:, None] * inv
    ac = cols[:, None] * inv
    ang = np.concatenate([ar, ar, ac, ac], axis=-1)
    cos, sin = np.cos(ang).astype(f32), np.sin(ang).astype(f32)
    first = (np.arange(DA_SUB) % 16 < 8)[None, :]
    sin_neg = np.where(first, -sin, f32(0.0))
    sin_pos = np.where(first, f32(0.0), sin)
    tile = lambda a: jnp.asarray(np.tile(a, (1, LANES // DA_SUB)), F32)

    def pad(a, fill):
        return jnp.asarray(np.concatenate(
            [np.full((seq, KPE_LANE), fill, f32), a,
             np.full((seq, LANES - KPE_LANE - MLA_ROPE), fill, f32)], axis=1), F32)

    return ([tile(cos), tile(sin_neg), tile(sin_pos)],
            [pad(cos, 1.0), pad(sin_neg, 0.0), pad(sin_pos, 0.0)])


def _pad_heads(w, width, take):
    d, k, _ = w.shape
    w = w.reshape(d, k, N_HEADS_GRP, width)[..., :take]
    return jnp.pad(w, ((0, 0), (0, 0), (0, 0), (0, LANES - take))).reshape(d, k, N_HEADS_GRP * LANES)


def kernel(x_prompt, x_sample, cache_a_k, cache_a_v, cache_c_kv, cache_c_kpe, cache_d_k, cache_d_v,
           c, c_ctx, ada_w, ada_b, norm_g, w_in, da_lambda, da_subln_g, conv_w,
           mla_q_norm_g, mla_w_uq, mla_kv_norm_g, mla_w_ukv, na_rpb, w_out, final_norm_g):
    batch, seq, _ = x_prompt.shape
    dec_batch, dec_seq, _ = x_sample.shape
    past = cache_a_k.shape[2]
    assert dec_seq % TQ == 0 and dec_seq // TQ >= LOCAL_CHUNKS and seq % LANES == 0
    assert dec_seq % IN_PROJ_ROWS == 0 and D_MODEL % W_PREP_ROWS == 0

    w_in_p = _w_in_prep(w_in)
    w_out_b = w_out.astype(BF16)
    wuq = _pad_heads(mla_w_uq, MLA_NOPE + MLA_ROPE, MLA_NOPE + MLA_ROPE).astype(BF16)
    wuk = _pad_heads(mla_w_ukv, MLA_NOPE + MLA_V, MLA_NOPE).astype(BF16)
    wuv = mla_w_ukv.reshape(DEPTH, MLA_KV_RANK, N_HEADS_GRP, MLA_NOPE + MLA_V)[..., MLA_NOPE:]
    wuv = wuv.reshape(DEPTH, MLA_KV_RANK, GROUP_W).astype(BF16)
    subln_g2 = jnp.tile(da_subln_g, (1, LANES // HEAD_DIM)).reshape(DEPTH, 1, LANES)
    ng = norm_g.reshape(DEPTH, 1, D_MODEL)
    qng = mla_q_norm_g.reshape(DEPTH, 1, MLA_Q_RANK)
    kvng = mla_kv_norm_g.reshape(DEPTH, 1, MLA_KV_RANK)
    fg = final_norm_g.reshape(1, D_MODEL)
    tables_a, tables_c = _rope_tables(dec_seq)
    tables = tables_a + tables_c

    cvec8 = jnp.concatenate([c_ctx[None], c, jnp.zeros((8 - 1 - dec_batch, D_MODEL), F32)], axis=0)
    mod = _ada_call(cvec8, ada_w, ada_b).reshape(DEPTH, 8, 3, D_MODEL)

    feat = lambda a: a.transpose(0, 1, 3, 4, 2).reshape(dec_batch, DEPTH, GROUP_W, past)
    cache = _cache_call(feat(cache_a_k), feat(cache_a_v), cache_c_kv,
                        cache_c_kpe.transpose(0, 1, 3, 2), feat(cache_d_k), feat(cache_d_v), wuk, wuv)
    bias = _local_bias_tables(na_rpb, dec_seq // GRID_W)

    xp = x_prompt.reshape(batch * seq, D_MODEL)
    xs = x_sample.reshape(dec_batch * dec_seq, D_MODEL)
    states = []
    for l in range(DEPTH):
        lam_init = 0.8 - 0.6 * math.exp(-0.3 * l)
        final = l == DEPTH - 1

        xp, *states = _ctx_layer_call(xp, mod, ng, w_in_p, qng, wuq, kvng, wuk, wuv, da_lambda,
                                      subln_g2, conv_w, w_out_b, fg, states, layer=l,
                                      lam_init=lam_init, seq=seq, final=final, name=f"ctx_layer_{l}")

        tok, sq, loc, cv = _in_call(xs, mod, ng, w_in_p, qng, wuq, kvng, wuk, wuv, tables,
                                    layer=l, seq=dec_seq, tm=IN_PROJ_ROWS, mod_row0=1,
                                    name=f"lat_in_{l}")
        xs = _attn_call(xs, mod, tok, sq, loc, cv, cache, bias, da_lambda, subln_g2, conv_w, w_out_b,
                        fg, layer=l, lam_init=lam_init, nb=dec_batch, seq=dec_seq, tq=TQ, mod_row0=1,
                        final=final, name=f"lat_attn_{l}")

    def heads(a):
        return a.reshape(batch, DEPTH, N_HEADS_GRP, HEAD_DIM, seq).transpose(0, 1, 4, 2, 3)

    s_ak, s_av, s_ckv, s_kpe, s_dk, s_dv = states
    return (xp.reshape(batch, seq, D_MODEL), xs.reshape(dec_batch, dec_seq, D_MODEL),
            heads(s_ak), heads(s_av), s_ckv, s_kpe.transpose(0, 1, 3, 2), heads(s_dk), heads(s_dv))
```
